```python
import math
import jax, jax.numpy as jnp
from jax import lax
import numpy as np

D_MODEL = 1024
BATCH = 4
SEQ = 8192
DEPTH = 2

N_A_LAYERS = DEPTH // 2
N_B_LAYERS = DEPTH - N_A_LAYERS
HEAD_DIM = 64
A_HEADS = D_MODEL // HEAD_DIM
A_WIDTH = A_HEADS * HEAD_DIM
B_WINDOWS = (128, 512, 2048)
B_DILATIONS = (1, 4, 16)
B_GROUPS = len(B_DILATIONS)
B_HEADS_PER_GROUP = D_MODEL // (2 * HEAD_DIM)
B_HEADS = B_GROUPS * B_HEADS_PER_GROUP
B_WIDTH = B_HEADS * HEAD_DIM
B_OUT_WIDTH = B_HEADS_PER_GROUP * HEAD_DIM
BLOCK = 128
NUM_BUCKETS = 32
MAX_DISTANCE = 2048
N_GROUPS = 4
EXPERTS_PER_GROUP = 4
N_EXPERTS = N_GROUPS * EXPERTS_PER_GROUP
TOP_K = 2
D_EXPERT = D_MODEL // 2
EPS = 1e-6
NEG = -1e30
SCALE = HEAD_DIM ** -0.5

kernel_name = "fox_dilated_yoco_hmoe"


def rms_norm(x, g):
    xf = x.astype(jnp.float32)
    y = xf * lax.rsqrt(jnp.mean(xf * xf, axis=-1, keepdims=True) + EPS)
    return (y * g.astype(jnp.float32)).astype(x.dtype)


def fox_mixer(xn, w_in, b_f, q_gain, k_gain, w_out):
    B, S, _ = xn.shape
    nb = S // BLOCK
    proj = jnp.einsum('bsd,de->bse', xn, w_in)
    q = rms_norm(proj[..., :A_WIDTH].reshape(B, S, A_HEADS, HEAD_DIM), q_gain)
    k = rms_norm(proj[..., A_WIDTH:2 * A_WIDTH].reshape(B, S, A_HEADS, HEAD_DIM), k_gain)
    v = proj[..., 2 * A_WIDTH:3 * A_WIDTH].reshape(B, S, A_HEADS, HEAD_DIM)
    log_f = jax.nn.log_sigmoid((proj[..., 3 * A_WIDTH:] + b_f).astype(jnp.float32))
    cum = jnp.transpose(jnp.cumsum(log_f, axis=1), (0, 2, 1))
    qb = jnp.moveaxis(q.reshape(B, nb, BLOCK, A_HEADS, HEAD_DIM), 1, 0)
    cum_q = jnp.moveaxis(cum.reshape(B, A_HEADS, nb, BLOCK), 2, 0)
    key_pos = jnp.arange(S)

    def attend_block(args):
        blk, q_blk, cq = args
        s = jnp.einsum('bqhd,bkhd->bhqk', q_blk, k, preferred_element_type=jnp.float32) * SCALE
        s = s + cq[..., :, None] - cum[:, :, None, :]
        q_pos = blk * BLOCK + jnp.arange(BLOCK)
        s = jnp.where(key_pos[None, :] <= q_pos[:, None], s, NEG)
        p = jax.nn.softmax(s, axis=-1)
        return jnp.einsum('bhqk,bkhd->bqhd', p.astype(v.dtype), v)

    out = lax.map(attend_block, (jnp.arange(nb), qb, cum_q))
    out = jnp.moveaxis(out, 0, 1).reshape(B, S, A_WIDTH)
    return jnp.einsum('bse,ed->bsd', out, w_out)


def hier_moe(xn, w_group, b_group, w_expert, b_expert, w_gate, w_up, w_down):
    B, S, D = xn.shape
    N = B * S
    xt = xn.reshape(N, D)
    g_prob = jax.nn.softmax((xt @ w_group + b_group).astype(jnp.float32), axis=-1)
    g_val, g_idx = lax.top_k(g_prob, 1)
    e_logits = (xt @ w_expert + b_expert).astype(jnp.float32).reshape(N, N_GROUPS, EXPERTS_PER_GROUP)
    sel = jnp.broadcast_to(g_idx[:, :, None], (N, 1, EXPERTS_PER_GROUP))
    e_logits = jnp.take_along_axis(e_logits, sel, axis=1)[:, 0]
    e_val, e_idx = lax.top_k(jax.nn.softmax(e_logits, axis=-1), TOP_K)
    e_val = e_val / jnp.sum(e_val, axis=-1, keepdims=True)
    expert_id = g_idx * EXPERTS_PER_GROUP + e_idx
    gates = jnp.sum(jax.nn.one_hot(expert_id, N_EXPERTS, dtype=jnp.float32)
                    * (g_val * e_val)[..., None], axis=1)
    out = jnp.zeros((N, D), jnp.float32)
    for e in range(N_EXPERTS):
        hid = jax.nn.silu(xt @ w_gate[e]) * (xt @ w_up[e])
        out = out + gates[:, e:e + 1] * (hid @ w_down[e]).astype(jnp.float32)
    return out.astype(xn.dtype).reshape(B, S, D)


def to_strided_blocks(x, d):
    B, S = x.shape[:2]
    rest = x.shape[2:]
    span = d * BLOCK
    s_pad = -(-S // span) * span
    x = jnp.pad(x, [(0, 0), (0, s_pad - S)] + [(0, 0)] * len(rest))
    x = jnp.moveaxis(x.reshape(B, s_pad // d, d, *rest), 2, 1)
    return x.reshape(B, d, s_pad // span, BLOCK, *rest)


def from_strided_blocks(xb, S):
    B, d, nb = xb.shape[:3]
    rest = xb.shape[4:]
    x = jnp.moveaxis(xb.reshape(B, d, nb * BLOCK, *rest), 1, 2)
    return x.reshape(B, nb * BLOCK * d, *rest)[:, :S]


def with_prev_block(xb):
    prev = jnp.concatenate([jnp.zeros_like(xb[:, :, :1]), xb[:, :, :-1]], axis=2)
    return jnp.concatenate([prev, xb], axis=3)


def t5_bucket(dist):
    max_exact = NUM_BUCKETS // 2
    d_f = jnp.maximum(dist, max_exact).astype(jnp.float32)
    large = max_exact + (jnp.log(d_f / max_exact) / math.log(MAX_DISTANCE / max_exact)
                         * (NUM_BUCKETS - max_exact)).astype(jnp.int32)
    large = jnp.minimum(large, NUM_BUCKETS - 1)
    return jnp.where(dist < max_exact, dist, large)


def branch_bias(rel_bias, g, d):
    a = jnp.arange(BLOCK)[:, None]
    b = jnp.arange(2 * BLOCK)[None, :]
    n = BLOCK + a - b
    band = (n >= 0) & (n <= B_WINDOWS[g] // d)
    bucket = t5_bucket(jnp.maximum(n, 0) * d)
    bias = rel_bias[bucket][:, :, g * B_HEADS_PER_GROUP:(g + 1) * B_HEADS_PER_GROUP]
    return jnp.moveaxis(bias, 2, 0).astype(jnp.float32), band


def shared_kv(h, kv_norm, kv_w, k_gain):
    B, S, _ = h.shape
    kv = jnp.einsum('bsd,de->bse', rms_norm(h, kv_norm), kv_w)
    k = rms_norm(kv[..., :B_WIDTH].reshape(B, S, B_HEADS, HEAD_DIM), k_gain)
    v = kv[..., B_WIDTH:].reshape(B, S, B_HEADS, HEAD_DIM)
    branches = []
    for g, d in enumerate(B_DILATIONS):
        sl = slice(g * B_HEADS_PER_GROUP, (g + 1) * B_HEADS_PER_GROUP)
        branches.append((with_prev_block(to_strided_blocks(k[:, :, sl], d)),
                         with_prev_block(to_strided_blocks(v[:, :, sl], d))))
    return branches


def dilated_mixer(xn, w_q, q_gain, w_out, branches, biases):
    B, S, _ = xn.shape
    q = rms_norm(jnp.einsum('bsd,de->bse', xn, w_q).reshape(B, S, B_HEADS, HEAD_DIM), q_gain)
    outs, lses = [], []
    for g, d in enumerate(B_DILATIONS):
        kb, vb = branches[g]
        bias, band = biases[g]
        qb = to_strided_blocks(q[:, :, g * B_HEADS_PER_GROUP:(g + 1) * B_HEADS_PER_GROUP], d)
        nb = qb.shape[2]
        valid = band[None] & ((jnp.arange(nb)[:, None, None] > 0)
                              | (jnp.arange(2 * BLOCK) >= BLOCK)[None, None, :])
        s = jnp.einsum('brnqhd,brnkhd->brnhqk', qb, kb, preferred_element_type=jnp.float32) * SCALE
        s = jnp.where(valid[None, None, :, None], s + bias[None, None, None], NEG)
        m = jnp.max(s, axis=-1, keepdims=True)
        ex = jnp.exp(s - m)
        den = jnp.sum(ex, axis=-1, keepdims=True)
        o = jnp.einsum('brnhqk,brnkhd->brnqhd', (ex / den).astype(vb.dtype), vb)
        lse = jnp.swapaxes((m + jnp.log(den))[..., 0], 3, 4)
        outs.append(from_strided_blocks(o, S))
        lses.append(from_strided_blocks(lse, S))
    o = jnp.stack(outs, axis=0).astype(jnp.float32)
    alpha = jax.nn.softmax(jnp.stack(lses, axis=0), axis=0)
    merged = jnp.sum(alpha[..., None] * o, axis=0).astype(xn.dtype).reshape(B, S, B_OUT_WIDTH)
    return jnp.einsum('bse,ed->bsd', merged, w_out)


def setup_inputs(seed: int = 0) -> dict:
    key = jax.random.key(seed)
    ks = jax.random.split(key, 24)
    f32 = jnp.float32

    def nrm(k, shape, scale):
        return scale * jax.random.normal(k, shape, f32)

    def gain(k, shape):
        return 1.0 + 0.05 * jax.random.normal(k, shape, f32)

    res = (2 * DEPTH) ** -0.5
    return {
        "x": jax.random.normal(ks[0], (BATCH, SEQ, D_MODEL), f32),
        "a_norm": gain(ks[1], (N_A_LAYERS, D_MODEL)),
        "a_w_in": nrm(ks[2], (N_A_LAYERS, D_MODEL, 3 * A_WIDTH + A_HEADS), D_MODEL ** -0.5),
        "a_b_f": 2.0 + 0.5 * jax.random.normal(ks[3], (N_A_LAYERS, A_HEADS), f32),
        "a_q_gain": gain(ks[4], (N_A_LAYERS, HEAD_DIM)),
        "a_k_gain": gain(ks[5], (N_A_LAYERS, HEAD_DIM)),
        "a_w_out": nrm(ks[6], (N_A_LAYERS, A_WIDTH, D_MODEL), A_WIDTH ** -0.5 * res),
        "kv_norm": gain(ks[7], (D_MODEL,)),
        "kv_w": nrm(ks[8], (D_MODEL, 2 * B_WIDTH), D_MODEL ** -0.5),
        "kv_k_gain": gain(ks[9], (HEAD_DIM,)),
        "rel_bias": nrm(ks[10], (NUM_BUCKETS, B_HEADS), 0.5),
        "b_norm": gain(ks[11], (N_B_LAYERS, D_MODEL)),
        "b_w_q": nrm(ks[12], (N_B_LAYERS, D_MODEL, B_WIDTH), D_MODEL ** -0.5),
        "b_q_gain": gain(ks[13], (N_B_LAYERS, HEAD_DIM)),
        "b_w_out": nrm(ks[14], (N_B_LAYERS, B_OUT_WIDTH, D_MODEL), B_OUT_WIDTH ** -0.5 * res),
        "ffn_norm": gain(ks[15], (DEPTH, D_MODEL)),
        "moe_w_group": nrm(ks[16], (DEPTH, D_MODEL, N_GROUPS), D_MODEL ** -0.5),
        "moe_b_group": nrm(ks[17], (DEPTH, N_GROUPS), 0.01),
        "moe_w_expert": nrm(ks[18], (DEPTH, D_MODEL, N_EXPERTS), D_MODEL ** -0.5),
        "moe_b_expert": nrm(ks[19], (DEPTH, N_EXPERTS), 0.01),
        "moe_w_gate": nrm(ks[20], (DEPTH, N_EXPERTS, D_MODEL, D_EXPERT), D_MODEL ** -0.5),
        "moe_w_up": nrm(ks[21], (DEPTH, N_EXPERTS, D_MODEL, D_EXPERT), D_MODEL ** -0.5),
        "moe_w_down": nrm(ks[22], (DEPTH, N_EXPERTS, D_EXPERT, D_MODEL), D_EXPERT ** -0.5 * res),
    }


def reference(x, a_norm, a_w_in, a_b_f, a_q_gain, a_k_gain, a_w_out, kv_norm, kv_w, kv_k_gain,
              rel_bias, b_norm, b_w_q, b_q_gain, b_w_out, ffn_norm, moe_w_group, moe_b_group,
              moe_w_expert, moe_b_expert, moe_w_gate, moe_w_up, moe_w_down):
    h = x
    branches = None
    biases = [branch_bias(rel_bias, g, d) for g, d in enumerate(B_DILATIONS)]
    for layer in range(DEPTH):
        if layer < N_A_LAYERS:
            i = layer
            h = h + fox_mixer(rms_norm(h, a_norm[i]), a_w_in[i], a_b_f[i], a_q_gain[i],
                              a_k_gain[i], a_w_out[i])
        else:
            i = layer - N_A_LAYERS
            h = h + dilated_mixer(rms_norm(h, b_norm[i]), b_w_q[i], b_q_gain[i], b_w_out[i],
                                  branches, biases)
        h = h + hier_moe(rms_norm(h, ffn_norm[layer]), moe_w_group[layer], moe_b_group[layer],
                         moe_w_expert[layer], moe_b_expert[layer], moe_w_gate[layer],
                         moe_w_up[layer], moe_w_down[layer])
        if layer == N_A_LAYERS - 1:
            branches = shared_kv(h, kv_norm, kv_w, kv_k_gain)
    return h
```

```python
import functools
import math

import jax
import jax.numpy as jnp
from jax import lax
from jax.experimental import pallas as pl
from jax.experimental.pallas import tpu as pltpu

F32 = jnp.float32
BF16 = jnp.bfloat16

D_MODEL = 1024
HEAD_DIM = 64
A_HEADS = 16
B_GROUPS = 3
B_HEADS_PER_GROUP = 8
B_HEADS = 24
B_WIDTH = B_HEADS * HEAD_DIM
B_OUT_WIDTH = B_HEADS_PER_GROUP * HEAD_DIM
B_WINDOWS = (128, 512, 2048)
B_DILATIONS = (1, 4, 16)
BLOCK = 128
NUM_BUCKETS = 32
MAX_DISTANCE = 2048
N_GROUPS = 4
EXPERTS_PER_GROUP = 4
N_EXPERTS = 16
TOP_K = 2
D_EXPERT = 512
EPS = 1e-6
NEG = -1e30
SCALE = HEAD_DIM ** -0.5
LOG2E = 1.4426950408889634

LANES = 128
TOK_TILE = 512
VMEM_LIMIT = 56 * 1024 * 1024


def _params(*sem):
    return pltpu.CompilerParams(dimension_semantics=sem, vmem_limit_bytes=VMEM_LIMIT)


def _rms(x, g):
    return x * lax.rsqrt(jnp.mean(x * x, axis=-1, keepdims=True) + EPS) * g


def _split3(x):
    hi = x.astype(BF16).astype(F32)
    r = x - hi
    mid = r.astype(BF16).astype(F32)
    return hi, mid, r - mid


def _dot(a, b):
    return jnp.dot(a, b, preferred_element_type=F32)


def _dot_nt(a, b):
    return lax.dot_general(a, b, (((1,), (1,)), ((), ())), preferred_element_type=F32)


def _dot_tn(a, b):
    return lax.dot_general(a, b, (((0,), (0,)), ((), ())), preferred_element_type=F32)


def _fox_in_kernel(x_ref, g_ref, wqk_ref, wvt_ref, wf_ref, bf_ref, qg_ref, kg_ref,
                   q_ref, k_ref, vt_ref, carry_ref):
    tm = x_ref.shape[1]

    @pl.when(pl.program_id(1) == 0)
    def _():
        carry_ref[...] = jnp.zeros_like(carry_ref)

    xb = _rms(x_ref[0], g_ref[...]).astype(BF16)
    pqk = _dot(xb, wqk_ref[...])
    vt_ref[0, 0] = _dot_nt(wvt_ref[...], xb).astype(BF16)

    lane = lax.broadcasted_iota(jnp.int32, (1, LANES), 1)
    z = _dot(xb, wf_ref[...]) + bf_ref[...]
    lf = jnp.minimum(z, 0.0) - jnp.log1p(jnp.exp(-jnp.abs(z)))
    lf = jnp.where(lane < A_HEADS, lf, 0.0)
    row = lax.broadcasted_iota(jnp.int32, (tm, tm), 0)
    col = lax.broadcasted_iota(jnp.int32, (tm, tm), 1)
    tri = (col <= row).astype(BF16)
    hi, mid, lo = _split3(lf)
    cum = (_dot(tri, hi.astype(BF16)) + _dot(tri, mid.astype(BF16))
           + _dot(tri, lo.astype(BF16))) + carry_ref[...]
    carry_ref[...] = cum[tm - 1:tm, :]
    c = cum * (-LOG2E)

    head_lane = lane < HEAD_DIM
    for h in range(A_HEADS):
        sl = slice((h // 2) * LANES, (h // 2 + 1) * LANES)
        qs = pqk[:, sl]
        ks = pqk[:, A_HEADS * HEAD_DIM + sl.start:A_HEADS * HEAD_DIM + sl.stop]
        if h % 2:
            qs = pltpu.roll(qs, HEAD_DIM, 1)
            ks = pltpu.roll(ks, HEAD_DIM, 1)
        qs = jnp.where(head_lane, qs, 0.0)
        ks = jnp.where(head_lane, ks, 0.0)
        qn = qs * lax.rsqrt(jnp.sum(qs * qs, axis=-1, keepdims=True) * (1.0 / HEAD_DIM) + EPS) * qg_ref[...]
        kn = ks * lax.rsqrt(jnp.sum(ks * ks, axis=-1, keepdims=True) * (1.0 / HEAD_DIM) + EPS) * kg_ref[...]
        qa = jnp.where((lane >= HEAD_DIM) & (lane < HEAD_DIM + 3), 1.0, qn)
        c_hi, c_mid, c_lo = _split3(jnp.broadcast_to(c[:, h:h + 1], (tm, LANES)))
        ka = jnp.where(lane == HEAD_DIM, c_hi,
                       jnp.where(lane == HEAD_DIM + 1, c_mid,
                                 jnp.where(lane == HEAD_DIM + 2, c_lo, kn)))
        q_ref[0, :, h * LANES:(h + 1) * LANES] = qa.astype(BF16)
        k_ref[0, :, h * LANES:(h + 1) * LANES] = ka.astype(BF16)


def _fox_in(x, a_norm, w_in, b_f, q_gain, k_gain):
    B, S, _ = x.shape
    tm = TOK_TILE
    nt = S // tm
    aw = A_HEADS * HEAD_DIM
    wqk = w_in[:, :2 * aw].astype(BF16)
    wvt = w_in[:, 2 * aw:3 * aw].T.astype(BF16)
    wf = jnp.pad(w_in[:, 3 * aw:], ((0, 0), (0, LANES - A_HEADS))).astype(BF16)
    bf = jnp.pad(b_f, (0, LANES - A_HEADS)).reshape(1, LANES)
    qg = jnp.pad(q_gain * (SCALE * LOG2E), (0, LANES - HEAD_DIM)).reshape(1, LANES)
    kg = jnp.pad(k_gain, (0, LANES - HEAD_DIM)).reshape(1, LANES)
    const = lambda b, t: (0, 0)
    return pl.pallas_call(
        _fox_in_kernel,
        grid=(B, nt),
        in_specs=[
            pl.BlockSpec((1, tm, D_MODEL), lambda b, t: (b, t, 0)),
            pl.BlockSpec((1, D_MODEL), const),
            pl.BlockSpec((D_MODEL, 2 * aw), const),
            pl.BlockSpec((aw, D_MODEL), const),
            pl.BlockSpec((D_MODEL, LANES), const),
            pl.BlockSpec((1, LANES), const),
            pl.BlockSpec((1, LANES), const),
            pl.BlockSpec((1, LANES), const),
        ],
        out_specs=[
            pl.BlockSpec((1, tm, A_HEADS * LANES), lambda b, t: (b, t, 0)),
            pl.BlockSpec((1, tm, A_HEADS * LANES), lambda b, t: (b, t, 0)),
            pl.BlockSpec((1, 1, aw, tm), lambda b, t: (b, t, 0, 0)),
        ],
        out_shape=[
            jax.ShapeDtypeStruct((B, S, A_HEADS * LANES), BF16),
            jax.ShapeDtypeStruct((B, S, A_HEADS * LANES), BF16),
            jax.ShapeDtypeStruct((B, nt, aw, tm), BF16),
        ],
        scratch_shapes=[pltpu.VMEM((1, LANES), F32)],
        compiler_params=_params("arbitrary", "arbitrary"),
        name="fox_in",
    )(x, a_norm.reshape(1, D_MODEL), wqk, wvt, wf, bf, qg, kg)


def _fox_attn_kernel(q_ref, k_ref, vt_ref, o_ref, acc_ref, m_ref):
    tq = q_ref.shape[1]
    tk = vt_ref.shape[3]
    qi = pl.program_id(2)
    qa = q_ref[0]
    m_ref[...] = jnp.full_like(m_ref, NEG)
    acc_ref[...] = jnp.zeros_like(acc_ref)
    ones = jnp.ones((16, tk), BF16)

    def step(j, masked):
        kj = k_ref[0, pl.ds(pl.multiple_of(j * tk, tk), tk), :]
        st = _dot_nt(kj, qa)
        if masked:
            kpos = j * tk + lax.broadcasted_iota(jnp.int32, (tk, tq), 0)
            qpos = qi * tq + lax.broadcasted_iota(jnp.int32, (tk, tq), 1)
            st = jnp.where(kpos <= qpos, st, NEG)
        m_prev = m_ref[...]
        m_new = jnp.maximum(m_prev, jnp.max(st, axis=0, keepdims=True))
        p = jnp.exp2((st - m_new).astype(BF16))
        alpha = jnp.exp2(m_prev - m_new)
        vj = jnp.concatenate([vt_ref[0, j], ones], axis=0)
        acc_ref[...] = alpha * acc_ref[...] + _dot(vj, p)
        m_ref[...] = m_new

    def body(j, carry):
        step(j, False)
        return carry

    lax.fori_loop(0, qi * (tq // tk), body, 0)
    for d in range(tq // tk):
        step(qi * (tq // tk) + d, True)
    acc = acc_ref[...]
    o_ref[0] = (acc[:HEAD_DIM] / acc[HEAD_DIM:HEAD_DIM + 1]).astype(BF16)


def _fox_attn(q, k, vt, tq):
    B, S, _ = q.shape
    nt, tk = vt.shape[1], vt.shape[3]
    return pl.pallas_call(
        _fox_attn_kernel,
        grid=(B, A_HEADS, S // tq),
        in_specs=[
            pl.BlockSpec((1, tq, LANES), lambda b, h, i: (b, i, h)),
            pl.BlockSpec((1, S, LANES), lambda b, h, i: (b, 0, h)),
            pl.BlockSpec((1, nt, HEAD_DIM, tk), lambda b, h, i: (b, 0, h, 0)),
        ],
        out_specs=pl.BlockSpec((1, HEAD_DIM, tq), lambda b, h, i: (b, h, i)),
        out_shape=jax.ShapeDtypeStruct((B, A_HEADS * HEAD_DIM, S), BF16),
        scratch_shapes=[pltpu.VMEM((HEAD_DIM + 16, tq), F32), pltpu.VMEM((1, tq), F32)],
        compiler_params=_params("arbitrary", "arbitrary", "arbitrary"),
        name="fox_attn",
    )(q, k, vt)


def _route(xn, wrh_ref, wrl_ref, br_ref):
    xh = xn.astype(BF16)
    xl = (xn - xh.astype(F32)).astype(BF16)
    wh = wrh_ref[...]
    logits = _dot(xh, wh) + _dot(xl, wh) + _dot(xh, wrl_ref[...]) + br_ref[...]
    lane = lax.broadcasted_iota(jnp.int32, (1, LANES), 1)
    lanef = lane.astype(F32)
    far = float(LANES)

    gl = jnp.where(lane < N_GROUPS, logits, NEG)
    gm = jnp.max(gl, axis=-1, keepdims=True)
    g_val = 1.0 / jnp.sum(jnp.exp(gl - gm), axis=-1, keepdims=True)
    g_idx = jnp.min(jnp.where(gl == gm, lanef, far), axis=-1, keepdims=True)

    lo = N_GROUPS + EXPERTS_PER_GROUP * g_idx
    el = jnp.where((lanef >= lo) & (lanef < lo + EXPERTS_PER_GROUP), logits, NEG)
    em1 = jnp.max(el, axis=-1, keepdims=True)
    ez = jnp.sum(jnp.exp(el - em1), axis=-1, keepdims=True)
    i1 = jnp.min(jnp.where(el == em1, lanef, far), axis=-1, keepdims=True)
    el2 = jnp.where(lanef == i1, NEG, el)
    em2 = jnp.max(el2, axis=-1, keepdims=True)
    i2 = jnp.min(jnp.where(el2 == em2, lanef, far), axis=-1, keepdims=True)
    p1 = 1.0 / ez
    p2 = jnp.exp(em2 - em1) / ez
    den = p1 + p2
    gate1 = g_val * (p1 / den)
    gate2 = g_val * (p2 / den)
    return jnp.where(lane == 0, i1 - N_GROUPS,
                     jnp.where(lane == 1, i2 - N_GROUPS,
                               jnp.where(lane == 2, gate1,
                                         jnp.where(lane == 3, gate2, 0.0))))


def _router_weights(w_group, b_group, w_expert, b_expert):
    w = jnp.pad(jnp.concatenate([w_group, w_expert], axis=1),
                ((0, 0), (0, LANES - N_GROUPS - N_EXPERTS)))
    b = jnp.pad(jnp.concatenate([b_group, b_expert]), (0, LANES - N_GROUPS - N_EXPERTS))
    wh = w.astype(BF16)
    wl = (w - wh.astype(F32)).astype(BF16)
    return wh, wl, b.reshape(1, LANES)


def _ffn_in(h, fg_ref, wrh_ref, wrl_ref, br_ref, h_ref, xn_ref, route_ref):
    h_ref[0] = h
    xn = _rms(h, fg_ref[...])
    xn_ref[0] = xn
    route_ref[0] = _route(xn, wrh_ref, wrl_ref, br_ref)


def _fox_out_kernel(ot_ref, x_ref, wo_ref, fg_ref, wrh_ref, wrl_ref, br_ref,
                    h_ref, xn_ref, route_ref):
    h = x_ref[0] + _dot_tn(ot_ref[0], wo_ref[...])
    _ffn_in(h, fg_ref, wrh_ref, wrl_ref, br_ref, h_ref, xn_ref, route_ref)


def _token_out_specs(B, S, tm):
    specs = [
        pl.BlockSpec((1, tm, D_MODEL), lambda b, t: (b, t, 0)),
        pl.BlockSpec((1, tm, D_MODEL), lambda b, t: (b, t, 0)),
        pl.BlockSpec((1, tm, LANES), lambda b, t: (b, t, 0)),
    ]
    shapes = [
        jax.ShapeDtypeStruct((B, S, D_MODEL), F32),
        jax.ShapeDtypeStruct((B, S, D_MODEL), F32),
        jax.ShapeDtypeStruct((B, S, LANES), F32),
    ]
    return specs, shapes


def _fox_out(ot, x, w_out, ffn_g, router):
    B, S, _ = x.shape
    tm = TOK_TILE
    wrh, wrl, br = router
    const = lambda b, t: (0, 0)
    out_specs, out_shape = _token_out_specs(B, S, tm)
    return pl.pallas_call(
        _fox_out_kernel,
        grid=(B, S // tm),
        in_specs=[
            pl.BlockSpec((1, A_HEADS * HEAD_DIM, tm), lambda b, t: (b, 0, t)),
            pl.BlockSpec((1, tm, D_MODEL), lambda b, t: (b, t, 0)),
            pl.BlockSpec((A_HEADS * HEAD_DIM, D_MODEL), const),
            pl.BlockSpec((1, D_MODEL), const),
            pl.BlockSpec((D_MODEL, LANES), const),
            pl.BlockSpec((D_MODEL, LANES), const),
            pl.BlockSpec((1, LANES), const),
        ],
        out_specs=out_specs,
        out_shape=out_shape,
        compiler_params=_params("arbitrary", "arbitrary"),
        name="fox_out",
    )(ot, x, w_out.astype(BF16), ffn_g.reshape(1, D_MODEL), wrh, wrl, br)


MOE_TILE = 256


def _moe_plan(route, n_tok):
    tm = MOE_TILE
    n_pairs = TOP_K * n_tok
    n_tiles = n_pairs // tm + N_EXPERTS
    pair_e = route[:, :TOP_K].astype(jnp.int32).T.reshape(-1)
    order = jnp.argsort(pair_e, stable=True).astype(jnp.int32)
    counts = jnp.sum(pair_e[:, None] == jnp.arange(N_EXPERTS)[None, :], axis=0).astype(jnp.int32)
    tiles_per = (counts + tm - 1) // tm
    tile_end = jnp.cumsum(tiles_per)
    n_used = tile_end[-1]
    tid = jnp.arange(n_tiles, dtype=jnp.int32)
    tid_c = jnp.minimum(tid, n_used - 1)
    tile_e = jnp.minimum(jnp.searchsorted(tile_end, tid_c, side="right"), N_EXPERTS - 1).astype(jnp.int32)
    within = (tid_c - (tile_end - tiles_per)[tile_e]) * tm
    valid = jnp.where(tid < n_used, jnp.clip(counts[tile_e] - within, 0, tm), 0).astype(jnp.int32)
    src = (jnp.cumsum(counts) - counts)[tile_e][:, None] + within[:, None] + jnp.arange(tm, dtype=jnp.int32)[None, :]
    live = jnp.arange(tm, dtype=jnp.int32)[None, :] < valid[:, None]
    idx = jnp.where(live, order[jnp.clip(src, 0, n_pairs - 1)], 0).astype(jnp.int32)
    return tile_e, valid, idx


def _moe_kernel(te_ref, nv_ref, idx_hbm, xn_hbm, wg_ref, wu_ref, wd_ref, out_hbm,
                idx_s, xbuf, ybuf, isem, gsem, ssem, *, n_tok):
    del te_ref
    tm = xbuf.shape[1]
    i = pl.program_id(0)
    n_tiles = pl.num_programs(0)
    slot = lax.rem(i, 2)
    other = 1 - slot

    def idx_copy(t, s):
        return pltpu.make_async_copy(idx_hbm.at[t], idx_s.at[s], isem.at[s])

    def gather_start(s):
        def body(r, c):
            tok = lax.rem(idx_s[s, r], n_tok)
            pltpu.make_async_copy(xn_hbm.at[pl.ds(tok, 1)], xbuf.at[s, pl.ds(r, 1)], gsem.at[s]).start()
            return c
        lax.fori_loop(0, tm, body, 0, unroll=8)

    def gather_wait(s):
        def body(r, c):
            pltpu.make_async_copy(xn_hbm.at[pl.ds(0, 1)], xbuf.at[s, pl.ds(r, 1)], gsem.at[s]).wait()
            return c
        lax.fori_loop(0, tm, body, 0, unroll=8)

    def scatter_start(s, n):
        def body(r, c):
            pltpu.make_async_copy(ybuf.at[s, pl.ds(r, 1)], out_hbm.at[pl.ds(idx_s[s, r], 1)], ssem.at[s]).start()
            return c
        lax.fori_loop(0, n, body, 0)

    def scatter_wait(s, n):
        def body(r, c):
            pltpu.make_async_copy(ybuf.at[s, pl.ds(r, 1)], out_hbm.at[pl.ds(0, 1)], ssem.at[s]).wait()
            return c
        lax.fori_loop(0, n, body, 0)

    @pl.when(i == 0)
    def _():
        idx_copy(0, 0).start()
        idx_copy(0, 0).wait()
        gather_start(0)
        idx_copy(1, 1).start()

    @pl.when(i + 1 < n_tiles)
    def _():
        idx_copy(i + 1, other).wait()
        gather_start(other)

    gather_wait(slot)

    @pl.when(i >= 2)
    def _():
        scatter_wait(slot, nv_ref[jnp.maximum(i - 2, 0)])

    @pl.when(nv_ref[i] > 0)
    def _():
        xb = xbuf[slot].astype(BF16)
        hid = jax.nn.silu(_dot(xb, wg_ref[0])) * _dot(xb, wu_ref[0])
        ybuf[slot] = _dot(hid.astype(BF16), wd_ref[0])

    scatter_start(slot, nv_ref[i])

    @pl.when(i + 2 < n_tiles)
    def _():
        idx_copy(i + 2, slot).start()

    @pl.when(i == n_tiles - 1)
    def _():
        scatter_wait(other, nv_ref[jnp.maximum(i - 1, 0)])
        scatter_wait(slot, nv_ref[i])


def _moe(xn, route, w_gate, w_up, w_down):
    B, S, _ = xn.shape
    n_tok = B * S
    tm = MOE_TILE
    tile_e, valid, idx = _moe_plan(route.reshape(n_tok, LANES), n_tok)
    n_tiles = tile_e.shape[0]
    grid_spec = pltpu.PrefetchScalarGridSpec(
        num_scalar_prefetch=2,
        grid=(n_tiles,),
        in_specs=[
            pl.BlockSpec(memory_space=pl.ANY),
            pl.BlockSpec(memory_space=pl.ANY),
            pl.BlockSpec((1, D_MODEL, D_EXPERT), lambda i, te, nv: (te[i], 0, 0)),
            pl.BlockSpec((1, D_MODEL, D_EXPERT), lambda i, te, nv: (te[i], 0, 0)),
            pl.BlockSpec((1, D_EXPERT, D_MODEL), lambda i, te, nv: (te[i], 0, 0)),
        ],
        out_specs=pl.BlockSpec(memory_space=pl.ANY),
        scratch_shapes=[
            pltpu.SMEM((2, tm), jnp.int32),
            pltpu.VMEM((2, tm, D_MODEL), F32),
            pltpu.VMEM((2, tm, D_MODEL), F32),
            pltpu.SemaphoreType.DMA((2,)),
            pltpu.SemaphoreType.DMA((2,)),
            pltpu.SemaphoreType.DMA((2,)),
        ],
    )
    out = pl.pallas_call(
        functools.partial(_moe_kernel, n_tok=n_tok),
        grid_spec=grid_spec,
        out_shape=jax.ShapeDtypeStruct((TOP_K * n_tok, D_MODEL), F32),
        compiler_params=_params("arbitrary"),
        name="moe_experts",
    )(tile_e, valid, idx, xn.reshape(n_tok, D_MODEL), w_gate.astype(BF16), w_up.astype(BF16),
      w_down.astype(BF16))
    return out.reshape(TOP_K, B, S, D_MODEL)


def _moe_combine(h, y0, y1, route):
    lane = lax.broadcasted_iota(jnp.int32, (1, LANES), 1)
    g1 = jnp.sum(jnp.where(lane == 2, route, 0.0), axis=-1, keepdims=True)
    g2 = jnp.sum(jnp.where(lane == 3, route, 0.0), axis=-1, keepdims=True)
    return h + (g1 * y0 + g2 * y1)


def _head_rms_pairs(p, gain):
    lane = lax.broadcasted_iota(jnp.int32, (1, LANES), 1)
    first = lane < HEAD_DIM
    outs = []
    for j in range(p.shape[1] // LANES):
        s = p[:, j * LANES:(j + 1) * LANES]
        sq = s * s
        s_all = jnp.sum(sq, axis=-1, keepdims=True)
        s_lo = jnp.sum(jnp.where(first, sq, 0.0), axis=-1, keepdims=True)
        r = jnp.where(first, lax.rsqrt(s_lo * (1.0 / HEAD_DIM) + EPS),
                      lax.rsqrt((s_all - s_lo) * (1.0 / HEAD_DIM) + EPS))
        outs.append((s * r * gain).astype(BF16))
    return outs


def _dil_in_kernel(h_ref, y_ref, route_ref, kvg_ref, bg_ref, wk_ref, wv_ref, wq_ref, kgain_ref, qgain_ref,
                   h2_ref, k_ref, v_ref, q_ref):
    h2 = _moe_combine(h_ref[0], y_ref[0, 0], y_ref[1, 0], route_ref[0])
    h2_ref[0] = h2
    xkv = _rms(h2, kvg_ref[...]).astype(BF16)
    for j, s in enumerate(_head_rms_pairs(_dot(xkv, wk_ref[...]), kgain_ref[...])):
        k_ref[0, :, j * LANES:(j + 1) * LANES] = s
    v_ref[0] = _dot(xkv, wv_ref[...]).astype(BF16)
    xq = _rms(h2, bg_ref[...]).astype(BF16)
    for j, s in enumerate(_head_rms_pairs(_dot(xq, wq_ref[...]), qgain_ref[...])):
        q_ref[0, :, j * LANES:(j + 1) * LANES] = s


def _dil_in(h, y, route, kv_norm, b_norm, kv_w, w_q, k_gain, q_gain):
    B, S, _ = h.shape
    tm = TOK_TILE
    const = lambda b, t: (0, 0)
    tok = lambda w: pl.BlockSpec((1, tm, w), lambda b, t: (b, t, 0))
    return pl.pallas_call(
        _dil_in_kernel,
        grid=(B, S // tm),
        in_specs=[
            tok(D_MODEL),
            pl.BlockSpec((TOP_K, 1, tm, D_MODEL), lambda b, t: (0, b, t, 0)),
            tok(LANES),
            pl.BlockSpec((1, D_MODEL), const),
            pl.BlockSpec((1, D_MODEL), const),
            pl.BlockSpec((D_MODEL, B_WIDTH), const),
            pl.BlockSpec((D_MODEL, B_WIDTH), const),
            pl.BlockSpec((D_MODEL, B_WIDTH), const),
            pl.BlockSpec((1, LANES), const),
            pl.BlockSpec((1, LANES), const),
        ],
        out_specs=[tok(D_MODEL), tok(B_WIDTH), tok(B_WIDTH), tok(B_WIDTH)],
        out_shape=[
            jax.ShapeDtypeStruct((B, S, D_MODEL), F32),
            jax.ShapeDtypeStruct((B, S, B_WIDTH), BF16),
            jax.ShapeDtypeStruct((B, S, B_WIDTH), BF16),
            jax.ShapeDtypeStruct((B, S, B_WIDTH), BF16),
        ],
        compiler_params=_params("arbitrary", "arbitrary"),
        name="dil_in",
    )(h, y, route, kv_norm.reshape(1, D_MODEL), b_norm.reshape(1, D_MODEL),
      kv_w[:, :B_WIDTH].astype(BF16), kv_w[:, B_WIDTH:].astype(BF16), w_q.astype(BF16),
      jnp.tile(k_gain, 2).reshape(1, LANES), jnp.tile(q_gain * (SCALE * LOG2E), 2).reshape(1, LANES))


def _t5_bucket(dist):
    max_exact = NUM_BUCKETS // 2
    d_f = jnp.maximum(dist, max_exact).astype(F32)
    large = max_exact + (jnp.log(d_f / max_exact) / math.log(MAX_DISTANCE / max_exact)
                         * (NUM_BUCKETS - max_exact)).astype(jnp.int32)
    large = jnp.minimum(large, NUM_BUCKETS - 1)
    return jnp.where(dist < max_exact, dist, large)


def _branch_bias(rel_bias, g, d):
    a = jnp.arange(BLOCK)[:, None]
    b = jnp.arange(2 * BLOCK)[None, :]
    n = BLOCK + a - b
    band = (n >= 0) & (n <= B_WINDOWS[g] // d)
    bias = rel_bias[_t5_bucket(jnp.maximum(n, 0) * d)][:, :, g * B_HEADS_PER_GROUP:(g + 1) * B_HEADS_PER_GROUP]
    bias = jnp.moveaxis(bias, 2, 0).astype(F32) * LOG2E
    return jnp.where(band[None], bias, NEG)


def _dil_attn_kernel(q_ref, kp_ref, kc_ref, vp_ref, vc_ref, bias_ref, o_ref, lse_ref):
    n = pl.program_id(2)
    lane = lax.broadcasted_iota(jnp.int32, (1, LANES), 1)
    first = lane < HEAD_DIM
    col = lax.broadcasted_iota(jnp.int32, (1, 2 * BLOCK), 1)
    dead = (n == 0) & (col < BLOCK)
    lse_all = jnp.zeros((BLOCK, LANES), F32)
    for j in range(B_HEADS_PER_GROUP // 2):
        sl = slice(j * LANES, (j + 1) * LANES)
        q = q_ref[0, :, sl]
        kk = jnp.concatenate([kp_ref[0, :, sl], kc_ref[0, :, sl]], axis=0)
        vv = jnp.concatenate([vp_ref[0, :, sl], vc_ref[0, :, sl]], axis=0)
        outs = []
        for hh in range(2):
            qm = jnp.where(first if hh == 0 else jnp.logical_not(first), q, jnp.zeros_like(q))
            s = _dot_nt(qm, kk) + jnp.where(dead, NEG, bias_ref[2 * j + hh])
            m = jnp.max(s, axis=-1, keepdims=True)
            p = jnp.exp2(s - m)
            den = jnp.sum(p, axis=-1, keepdims=True)
            outs.append(_dot((p / den).astype(BF16), vv))
            lse_all = jnp.where(lane == 2 * j + hh, m + jnp.log2(den), lse_all)
        o_ref[0, :, sl] = jnp.where(first, outs[0], outs[1]).astype(BF16)
    lse_ref[0] = lse_all


def _dil_attn(q, k, v, bias, g, d):
    B, S, _ = q.shape
    L = S // d
    w = B_OUT_WIDTH
    nblk = B_WIDTH // w
    qv, kv_, vv = (a.reshape(B, L, d * B_WIDTH) for a in (q, k, v))
    cur = lambda b, r, n: (b, n, r * nblk + g)
    prev = lambda b, r, n: (b, jnp.maximum(n - 1, 0), r * nblk + g)
    o, lse = pl.pallas_call(
        _dil_attn_kernel,
        grid=(B, d, L // BLOCK),
        in_specs=[
            pl.BlockSpec((1, BLOCK, w), cur),
            pl.BlockSpec((1, BLOCK, w), prev),
            pl.BlockSpec((1, BLOCK, w), cur),
            pl.BlockSpec((1, BLOCK, w), prev),
            pl.BlockSpec((1, BLOCK, w), cur),
            pl.BlockSpec((B_HEADS_PER_GROUP, BLOCK, 2 * BLOCK), lambda b, r, n: (0, 0, 0)),
        ],
        out_specs=[
            pl.BlockSpec((1, BLOCK, w), lambda b, r, n: (b, n, r)),
            pl.BlockSpec((1, BLOCK, LANES), lambda b, r, n: (b, n, r)),
        ],
        out_shape=[
            jax.ShapeDtypeStruct((B, L, d * w), BF16),
            jax.ShapeDtypeStruct((B, L, d * LANES), F32),
        ],
        compiler_params=_params("arbitrary", "arbitrary", "arbitrary"),
        name=f"dil_attn_{g}",
    )(qv, kv_, kv_, vv, vv, bias)
    return o.reshape(B, S, w), lse.reshape(B, S, LANES)


def _dil_out_kernel(o0_ref, o1_ref, o2_ref, l0_ref, l1_ref, l2_ref, h_ref, wo_ref, fg_ref,
                    wrh_ref, wrl_ref, br_ref, h_out_ref, xn_ref, route_ref):
    l0, l1, l2 = l0_ref[0], l1_ref[0], l2_ref[0]
    m = jnp.maximum(jnp.maximum(l0, l1), l2)
    e0, e1, e2 = jnp.exp2(l0 - m), jnp.exp2(l1 - m), jnp.exp2(l2 - m)
    den = e0 + e1 + e2
    row = lax.broadcasted_iota(jnp.int32, (LANES, B_OUT_WIDTH), 0)
    col = lax.broadcasted_iota(jnp.int32, (LANES, B_OUT_WIDTH), 1)
    spread = (jnp.right_shift(col, 6) == row).astype(BF16)

    def widen(a):
        hi = a.astype(BF16)
        lo = (a - hi.astype(F32)).astype(BF16)
        return _dot(hi, spread) + _dot(lo, spread)

    merged = (widen(e0 / den) * o0_ref[0].astype(F32) + widen(e1 / den) * o1_ref[0].astype(F32)
              + widen(e2 / den) * o2_ref[0].astype(F32))
    h = h_ref[0] + _dot(merged.astype(BF16), wo_ref[...])
    _ffn_in(h, fg_ref, wrh_ref, wrl_ref, br_ref, h_out_ref, xn_ref, route_ref)


def _dil_out(os_, lses, h, w_out, ffn_g, router):
    B, S, _ = h.shape
    tm = TOK_TILE
    wrh, wrl, br = router
    const = lambda b, t: (0, 0)
    tok = lambda w: pl.BlockSpec((1, tm, w), lambda b, t: (b, t, 0))
    out_specs, out_shape = _token_out_specs(B, S, tm)
    return pl.pallas_call(
        _dil_out_kernel,
        grid=(B, S // tm),
        in_specs=[tok(B_OUT_WIDTH)] * 3 + [tok(LANES)] * 3 + [
            tok(D_MODEL),
            pl.BlockSpec((B_OUT_WIDTH, D_MODEL), const),
            pl.BlockSpec((1, D_MODEL), const),
            pl.BlockSpec((D_MODEL, LANES), const),
            pl.BlockSpec((D_MODEL, LANES), const),
            pl.BlockSpec((1, LANES), const),
        ],
        out_specs=out_specs,
        out_shape=out_shape,
        compiler_params=_params("arbitrary", "arbitrary"),
        name="dil_out",
    )(*os_, *lses, h, w_out.astype(BF16), ffn_g.reshape(1, D_MODEL), wrh, wrl, br)


def _final_kernel(h_ref, y_ref, route_ref, o_ref):
    o_ref[0] = _moe_combine(h_ref[0], y_ref[0, 0], y_ref[1, 0], route_ref[0])


def _final(h, y, route):
    B, S, _ = h.shape
    tm = TOK_TILE
    tok = lambda w: pl.BlockSpec((1, tm, w), lambda b, t: (b, t, 0))
    return pl.pallas_call(
        _final_kernel,
        grid=(B, S // tm),
        in_specs=[tok(D_MODEL), pl.BlockSpec((TOP_K, 1, tm, D_MODEL), lambda b, t: (0, b, t, 0)), tok(LANES)],
        out_specs=tok(D_MODEL),
        out_shape=jax.ShapeDtypeStruct((B, S, D_MODEL), F32),
        compiler_params=_params("arbitrary", "arbitrary"),
        name="moe_final",
    )(h, y, route)


def kernel(x, a_norm, a_w_in, a_b_f, a_q_gain, a_k_gain, a_w_out, kv_norm, kv_w, kv_k_gain, rel_bias, b_norm, b_w_q, b_q_gain, b_w_out, ffn_norm, moe_w_group, moe_b_group, moe_w_expert, moe_b_expert, moe_w_gate, moe_w_up, moe_w_down):
    routers = [_router_weights(moe_w_group[l], moe_b_group[l], moe_w_expert[l], moe_b_expert[l])
               for l in range(2)]
    q, k, vt = _fox_in(x, a_norm[0], a_w_in[0], a_b_f[0], a_q_gain[0], a_k_gain[0])
    ot = _fox_attn(q, k, vt, TOK_TILE)
    h1, xn1, route1 = _fox_out(ot, x, a_w_out[0], ffn_norm[0], routers[0])
    y1 = _moe(xn1, route1, moe_w_gate[0], moe_w_up[0], moe_w_down[0])
    h2, k2, v2, q2 = _dil_in(h1, y1, route1, kv_norm, b_norm[0], kv_w, b_w_q[0], kv_k_gain, b_q_gain[0])
    outs, lses = [], []
    for g, d in enumerate(B_DILATIONS):
        o, lse = _dil_attn(q2, k2, v2, _branch_bias(rel_bias, g, d), g, d)
        outs.append(o)
        lses.append(lse)
    h3, xn3, route3 = _dil_out(outs, lses, h2, b_w_out[0], ffn_norm[1], routers[1])
    y3 = _moe(xn3, route3, moe_w_gate[1], moe_w_up[1], moe_w_down[1])
    return _final(h3, y3, route3)
```

```python
import functools
import math

import jax
import jax.numpy as jnp
from jax import lax
from jax.experimental import pallas as pl
from jax.experimental.pallas import tpu as pltpu

F32 = jnp.float32
BF16 = jnp.bfloat16

D_MODEL = 1024
HEAD_DIM = 64
A_HEADS = 16
B_GROUPS = 3
B_HEADS_PER_GROUP = 8
B_HEADS = 24
B_WIDTH = B_HEADS * HEAD_DIM
B_OUT_WIDTH = B_HEADS_PER_GROUP * HEAD_DIM
B_WINDOWS = (128, 512, 2048)
B_DILATIONS = (1, 4, 16)
BLOCK = 128
NUM_BUCKETS = 32
MAX_DISTANCE = 2048
N_GROUPS = 4
EXPERTS_PER_GROUP = 4
N_EXPERTS = 16
TOP_K = 2
D_EXPERT = 512
EPS = 1e-6
NEG = -1e30
SCALE = HEAD_DIM ** -0.5
LOG2E = 1.4426950408889634

LANES = 128
TOK_TILE = 512
VMEM_LIMIT = 56 * 1024 * 1024


def _params(*sem):
    return pltpu.CompilerParams(dimension_semantics=sem, vmem_limit_bytes=VMEM_LIMIT)


def _rms(x, g):
    return x * lax.rsqrt(jnp.mean(x * x, axis=-1, keepdims=True) + EPS) * g


def _split3(x):
    hi = x.astype(BF16).astype(F32)
    r = x - hi
    mid = r.astype(BF16).astype(F32)
    return hi, mid, r - mid


SLAB = D_MODEL // LANES


def _rows_from_slabs(ref):
    return jnp.concatenate([ref[:, c, :] for c in range(SLAB)], axis=1)


def _rows_to_slabs(ref, x):
    for c in range(SLAB):
        ref[:, c, :] = x[:, c * LANES:(c + 1) * LANES]


def _dot(a, b):
    return jnp.dot(a, b, preferred_element_type=F32)


def _dot_nt(a, b):
    return lax.dot_general(a, b, (((1,), (1,)), ((), ())), preferred_element_type=F32)


def _dot_tn(a, b):
    return lax.dot_general(a, b, (((0,), (0,)), ((), ())), preferred_element_type=F32)


def _fox_in_kernel(x_ref, g_ref, wqk_ref, wvt_ref, wf_ref, bf_ref, qg_ref, kg_ref,
                   q_ref, k_ref, vt_ref, carry_ref):
    tm = x_ref.shape[1]

    @pl.when(pl.program_id(1) == 0)
    def _():
        carry_ref[...] = jnp.zeros_like(carry_ref)

    xb = _rms(x_ref[0], g_ref[...]).astype(BF16)
    pqk = _dot(xb, wqk_ref[...])
    vt_ref[0, 0] = _dot_nt(wvt_ref[...], xb).astype(BF16)

    lane = lax.broadcasted_iota(jnp.int32, (1, LANES), 1)
    z = _dot(xb, wf_ref[...]) + bf_ref[...]
    lf = jnp.minimum(z, 0.0) - jnp.log1p(jnp.exp(-jnp.abs(z)))
    lf = jnp.where(lane < A_HEADS, lf, 0.0)
    row = lax.broadcasted_iota(jnp.int32, (tm, tm), 0)
    col = lax.broadcasted_iota(jnp.int32, (tm, tm), 1)
    tri = (col <= row).astype(BF16)
    hi, mid, lo = _split3(lf)
    cum = (_dot(tri, hi.astype(BF16)) + _dot(tri, mid.astype(BF16))
           + _dot(tri, lo.astype(BF16))) + carry_ref[...]
    carry_ref[...] = cum[tm - 1:tm, :]
    c = cum * (-LOG2E)

    head_lane = lane < HEAD_DIM
    for h in range(A_HEADS):
        sl = slice((h // 2) * LANES, (h // 2 + 1) * LANES)
        qs = pqk[:, sl]
        ks = pqk[:, A_HEADS * HEAD_DIM + sl.start:A_HEADS * HEAD_DIM + sl.stop]
        if h % 2:
            qs = pltpu.roll(qs, HEAD_DIM, 1)
            ks = pltpu.roll(ks, HEAD_DIM, 1)
        qs = jnp.where(head_lane, qs, 0.0)
        ks = jnp.where(head_lane, ks, 0.0)
        qn = qs * lax.rsqrt(jnp.sum(qs * qs, axis=-1, keepdims=True) * (1.0 / HEAD_DIM) + EPS) * qg_ref[...]
        kn = ks * lax.rsqrt(jnp.sum(ks * ks, axis=-1, keepdims=True) * (1.0 / HEAD_DIM) + EPS) * kg_ref[...]
        qa = jnp.where((lane >= HEAD_DIM) & (lane < HEAD_DIM + 3), 1.0, qn)
        c_hi, c_mid, c_lo = _split3(jnp.broadcast_to(c[:, h:h + 1], (tm, LANES)))
        ka = jnp.where(lane == HEAD_DIM, c_hi,
                       jnp.where(lane == HEAD_DIM + 1, c_mid,
                                 jnp.where(lane == HEAD_DIM + 2, c_lo, kn)))
        q_ref[0, :, h * LANES:(h + 1) * LANES] = qa.astype(BF16)
        k_ref[0, :, h * LANES:(h + 1) * LANES] = ka.astype(BF16)


def _fox_in(x, a_norm, w_in, b_f, q_gain, k_gain):
    B, S, _ = x.shape
    tm = TOK_TILE
    nt = S // tm
    aw = A_HEADS * HEAD_DIM
    wqk = w_in[:, :2 * aw].astype(BF16)
    wvt = w_in[:, 2 * aw:3 * aw].T.astype(BF16)
    wf = jnp.pad(w_in[:, 3 * aw:], ((0, 0), (0, LANES - A_HEADS))).astype(BF16)
    bf = jnp.pad(b_f, (0, LANES - A_HEADS)).reshape(1, LANES)
    qg = jnp.pad(q_gain * (SCALE * LOG2E), (0, LANES - HEAD_DIM)).reshape(1, LANES)
    kg = jnp.pad(k_gain, (0, LANES - HEAD_DIM)).reshape(1, LANES)
    const = lambda b, t: (0, 0)
    return pl.pallas_call(
        _fox_in_kernel,
        grid=(B, nt),
        in_specs=[
            pl.BlockSpec((1, tm, D_MODEL), lambda b, t: (b, t, 0)),
            pl.BlockSpec((1, D_MODEL), const),
            pl.BlockSpec((D_MODEL, 2 * aw), const),
            pl.BlockSpec((aw, D_MODEL), const),
            pl.BlockSpec((D_MODEL, LANES), const),
            pl.BlockSpec((1, LANES), const),
            pl.BlockSpec((1, LANES), const),
            pl.BlockSpec((1, LANES), const),
        ],
        out_specs=[
            pl.BlockSpec((1, tm, A_HEADS * LANES), lambda b, t: (b, t, 0)),
            pl.BlockSpec((1, tm, A_HEADS * LANES), lambda b, t: (b, t, 0)),
            pl.BlockSpec((1, 1, aw, tm), lambda b, t: (b, t, 0, 0)),
        ],
        out_shape=[
            jax.ShapeDtypeStruct((B, S, A_HEADS * LANES), BF16),
            jax.ShapeDtypeStruct((B, S, A_HEADS * LANES), BF16),
            jax.ShapeDtypeStruct((B, nt, aw, tm), BF16),
        ],
        scratch_shapes=[pltpu.VMEM((1, LANES), F32)],
        compiler_params=_params("arbitrary", "arbitrary"),
        name="fox_in",
    )(x, a_norm.reshape(1, D_MODEL), wqk, wvt, wf, bf, qg, kg)


FOX_TQ = 2048
FOX_QCHUNK = 256
FOX_UNROLL = 4
FOX_AHEAD = 4


def _fox_attn_kernel(q_ref, k_ref, vt_ref, o_ref, *scratch):
    tq = q_ref.shape[1]
    tk = vt_ref.shape[3]
    qc = FOX_QCHUNK
    nc = tq // qc
    ratio = tq // tk
    qi = pl.program_id(2)
    acc_refs, m_refs = scratch[:nc], scratch[nc:]
    for c in range(nc):
        m_refs[c][...] = jnp.full_like(m_refs[c], NEG)
        acc_refs[c][...] = jnp.zeros_like(acc_refs[c])
    ones = jnp.ones((16, tk), BF16)

    def scores(kj, c):
        return _dot_nt(kj, q_ref[0, c * qc:(c + 1) * qc, :])

    def update(st, vj, c, mask):
        if mask is not None:
            st = jnp.where(mask, st, NEG)
        m_prev = m_refs[c][...]
        m_new = jnp.maximum(m_prev, jnp.max(st, axis=0, keepdims=True))
        p = jnp.exp2((st - m_new).astype(BF16))
        alpha = jnp.exp2(m_prev - m_new)
        acc_refs[c][...] = alpha * acc_refs[c][...] + _dot(vj, p)
        m_refs[c][...] = m_new

    def load(j):
        kj = k_ref[0, pl.ds(pl.multiple_of(j * tk, tk), tk), :]
        vj = jnp.concatenate([vt_ref[0, j], ones], axis=0)
        return kj, vj

    def run(work):
        st = [None] * len(work)
        for i in range(min(FOX_AHEAD, len(work))):
            st[i] = scores(work[i][0], work[i][2])
        for i, (kj, vj, c, mask) in enumerate(work):
            if i + FOX_AHEAD < len(work):
                st[i + FOX_AHEAD] = scores(work[i + FOX_AHEAD][0], work[i + FOX_AHEAD][2])
            update(st[i], vj, c, mask)
            st[i] = None

    def body(jj, carry):
        work = []
        for u in range(FOX_UNROLL):
            kj, vj = load(jj * FOX_UNROLL + u)
            work += [(kj, vj, c, None) for c in range(nc)]
        run(work)
        return carry

    lax.fori_loop(0, qi * (ratio // FOX_UNROLL), body, 0)
    work = []
    for d in range(ratio):
        kj, vj = load(qi * ratio + d)
        for c in range(nc):
            if d * tk > (c + 1) * qc - 1:
                continue
            if (d + 1) * tk - 1 <= c * qc:
                work.append((kj, vj, c, None))
            else:
                kpos = d * tk + lax.broadcasted_iota(jnp.int32, (tk, qc), 0)
                qpos = c * qc + lax.broadcasted_iota(jnp.int32, (tk, qc), 1)
                work.append((kj, vj, c, kpos <= qpos))
    run(work)
    for c in range(nc):
        acc = acc_refs[c][...]
        o_ref[0, :, c * qc:(c + 1) * qc] = (acc[:HEAD_DIM] / acc[HEAD_DIM:HEAD_DIM + 1]).astype(BF16)


def _fox_attn(q, k, vt):
    B, S, _ = q.shape
    nt, tk = vt.shape[1], vt.shape[3]
    tq = FOX_TQ
    assert tq % (tk * FOX_UNROLL) == 0 and tq % FOX_QCHUNK == 0
    return pl.pallas_call(
        _fox_attn_kernel,
        grid=(B, A_HEADS, S // tq),
        in_specs=[
            pl.BlockSpec((1, tq, LANES), lambda b, h, i: (b, i, h)),
            pl.BlockSpec((1, S, LANES), lambda b, h, i: (b, 0, h)),
            pl.BlockSpec((1, nt, HEAD_DIM, tk), lambda b, h, i: (b, 0, h, 0)),
        ],
        out_specs=pl.BlockSpec((1, HEAD_DIM, tq), lambda b, h, i: (b, h, i)),
        out_shape=jax.ShapeDtypeStruct((B, A_HEADS * HEAD_DIM, S), BF16),
        scratch_shapes=([pltpu.VMEM((HEAD_DIM + 16, FOX_QCHUNK), F32)] * (tq // FOX_QCHUNK)
                        + [pltpu.VMEM((1, FOX_QCHUNK), F32)] * (tq // FOX_QCHUNK)),
        compiler_params=_params("arbitrary", "arbitrary", "arbitrary"),
        name="fox_attn",
    )(q, k, vt)


def _route(xn, wrh_ref, wrl_ref, br_ref):
    xh = xn.astype(BF16)
    xl = (xn - xh.astype(F32)).astype(BF16)
    wh = wrh_ref[...]
    logits = _dot(xh, wh) + _dot(xl, wh) + _dot(xh, wrl_ref[...]) + br_ref[...]
    lane = lax.broadcasted_iota(jnp.int32, (1, LANES), 1)
    lanef = lane.astype(F32)
    far = float(LANES)

    gl = jnp.where(lane < N_GROUPS, logits, NEG)
    gm = jnp.max(gl, axis=-1, keepdims=True)
    g_val = 1.0 / jnp.sum(jnp.exp(gl - gm), axis=-1, keepdims=True)
    g_idx = jnp.min(jnp.where(gl == gm, lanef, far), axis=-1, keepdims=True)

    lo = N_GROUPS + EXPERTS_PER_GROUP * g_idx
    el = jnp.where((lanef >= lo) & (lanef < lo + EXPERTS_PER_GROUP), logits, NEG)
    em1 = jnp.max(el, axis=-1, keepdims=True)
    ez = jnp.sum(jnp.exp(el - em1), axis=-1, keepdims=True)
    i1 = jnp.min(jnp.where(el == em1, lanef, far), axis=-1, keepdims=True)
    el2 = jnp.where(lanef == i1, NEG, el)
    em2 = jnp.max(el2, axis=-1, keepdims=True)
    i2 = jnp.min(jnp.where(el2 == em2, lanef, far), axis=-1, keepdims=True)
    p1 = 1.0 / ez
    p2 = jnp.exp(em2 - em1) / ez
    den = p1 + p2
    gate1 = g_val * (p1 / den)
    gate2 = g_val * (p2 / den)
    return jnp.where(lane == 0, i1 - N_GROUPS,
                     jnp.where(lane == 1, i2 - N_GROUPS,
                               jnp.where(lane == 2, gate1,
                                         jnp.where(lane == 3, gate2, 0.0))))


def _router_weights(w_group, b_group, w_expert, b_expert):
    w = jnp.pad(jnp.concatenate([w_group, w_expert], axis=1),
                ((0, 0), (0, LANES - N_GROUPS - N_EXPERTS)))
    b = jnp.pad(jnp.concatenate([b_group, b_expert]), (0, LANES - N_GROUPS - N_EXPERTS))
    wh = w.astype(BF16)
    wl = (w - wh.astype(F32)).astype(BF16)
    return wh, wl, b.reshape(1, LANES)


def _ffn_in(h, fg_ref, wrh_ref, wrl_ref, br_ref, h_ref, xn_ref, route_ref):
    h_ref[0] = h
    xn = _rms(h, fg_ref[...])
    _rows_to_slabs(xn_ref.at[0], xn)
    route_ref[0] = _route(xn, wrh_ref, wrl_ref, br_ref)


def _fox_out_kernel(ot_ref, x_ref, wo_ref, fg_ref, wrh_ref, wrl_ref, br_ref,
                    h_ref, xn_ref, route_ref):
    h = x_ref[0] + _dot_tn(ot_ref[0], wo_ref[...])
    _ffn_in(h, fg_ref, wrh_ref, wrl_ref, br_ref, h_ref, xn_ref, route_ref)


def _token_out_specs(B, S, tm):
    specs = [
        pl.BlockSpec((1, tm, D_MODEL), lambda b, t: (b, t, 0)),
        pl.BlockSpec((1, tm, SLAB, LANES), lambda b, t: (b, t, 0, 0)),
        pl.BlockSpec((1, tm, LANES), lambda b, t: (b, t, 0)),
    ]
    shapes = [
        jax.ShapeDtypeStruct((B, S, D_MODEL), F32),
        jax.ShapeDtypeStruct((B, S, SLAB, LANES), F32),
        jax.ShapeDtypeStruct((B, S, LANES), F32),
    ]
    return specs, shapes


def _fox_out(ot, x, w_out, ffn_g, router):
    B, S, _ = x.shape
    tm = TOK_TILE
    wrh, wrl, br = router
    const = lambda b, t: (0, 0)
    out_specs, out_shape = _token_out_specs(B, S, tm)
    return pl.pallas_call(
        _fox_out_kernel,
        grid=(B, S // tm),
        in_specs=[
            pl.BlockSpec((1, A_HEADS * HEAD_DIM, tm), lambda b, t: (b, 0, t)),
            pl.BlockSpec((1, tm, D_MODEL), lambda b, t: (b, t, 0)),
            pl.BlockSpec((A_HEADS * HEAD_DIM, D_MODEL), const),
            pl.BlockSpec((1, D_MODEL), const),
            pl.BlockSpec((D_MODEL, LANES), const),
            pl.BlockSpec((D_MODEL, LANES), const),
            pl.BlockSpec((1, LANES), const),
        ],
        out_specs=out_specs,
        out_shape=out_shape,
        compiler_params=_params("arbitrary", "arbitrary"),
        name="fox_out",
    )(ot, x, w_out.astype(BF16), ffn_g.reshape(1, D_MODEL), wrh, wrl, br)


MOE_TILE = 256


def _moe_plan(route, n_tok):
    tm = MOE_TILE
    n_pairs = TOP_K * n_tok
    n_tiles = n_pairs // tm + N_EXPERTS
    pair_e = route[:, :TOP_K].astype(jnp.int32).T.reshape(-1)
    order = jnp.argsort(pair_e, stable=True).astype(jnp.int32)
    counts = jnp.sum(pair_e[:, None] == jnp.arange(N_EXPERTS)[None, :], axis=0).astype(jnp.int32)
    tiles_per = (counts + tm - 1) // tm
    tile_end = jnp.cumsum(tiles_per)
    n_used = tile_end[-1]
    tid = jnp.arange(n_tiles, dtype=jnp.int32)
    tid_c = jnp.minimum(tid, n_used - 1)
    tile_e = jnp.minimum(jnp.searchsorted(tile_end, tid_c, side="right"), N_EXPERTS - 1).astype(jnp.int32)
    within = (tid_c - (tile_end - tiles_per)[tile_e]) * tm
    valid = jnp.where(tid < n_used, jnp.clip(counts[tile_e] - within, 0, tm), 0).astype(jnp.int32)
    src = (jnp.cumsum(counts) - counts)[tile_e][:, None] + within[:, None] + jnp.arange(tm, dtype=jnp.int32)[None, :]
    live = jnp.arange(tm, dtype=jnp.int32)[None, :] < valid[:, None]
    idx = jnp.where(live, order[jnp.clip(src, 0, n_pairs - 1)], 0).astype(jnp.int32)
    return tile_e, valid, idx


def _moe_kernel(te_ref, nv_ref, idx_hbm, xn_hbm, wg_ref, wu_ref, wd_ref, out_hbm,
                idx_s, xbuf, ybuf, isem, gsem, ssem, *, n_tok):
    del te_ref
    tm = xbuf.shape[1]
    i = pl.program_id(0)
    n_tiles = pl.num_programs(0)
    slot = lax.rem(i, 2)
    other = 1 - slot

    def idx_copy(t, s):
        return pltpu.make_async_copy(idx_hbm.at[t], idx_s.at[s], isem.at[s])

    def gather_start(s):
        def body(r, c):
            tok = lax.rem(idx_s[s, r], n_tok)
            pltpu.make_async_copy(xn_hbm.at[tok], xbuf.at[s, r], gsem.at[s]).start()
            return c
        lax.fori_loop(0, tm, body, 0, unroll=8)

    def gather_wait(s):
        pltpu.make_async_copy(xn_hbm.at[pl.ds(0, tm)], xbuf.at[s], gsem.at[s]).wait()

    def scatter_start(s, n):
        def body(r, c):
            pltpu.make_async_copy(ybuf.at[s, r], out_hbm.at[idx_s[s, r]], ssem.at[s]).start()
            return c
        lax.fori_loop(0, n, body, 0)

    def scatter_wait(s, n):
        @pl.when(n > 0)
        def _():
            pltpu.make_async_copy(ybuf.at[s, pl.ds(0, n)], out_hbm.at[pl.ds(0, n)], ssem.at[s]).wait()

    @pl.when(i == 0)
    def _():
        idx_copy(0, 0).start()
        idx_copy(0, 0).wait()
        gather_start(0)
        idx_copy(1, 1).start()

    @pl.when(i + 1 < n_tiles)
    def _():
        idx_copy(i + 1, other).wait()
        gather_start(other)

    gather_wait(slot)

    @pl.when(i >= 2)
    def _():
        scatter_wait(slot, nv_ref[jnp.maximum(i - 2, 0)])

    @pl.when(nv_ref[i] > 0)
    def _():
        xb = _rows_from_slabs(xbuf.at[slot]).astype(BF16)
        hid = jax.nn.silu(_dot(xb, wg_ref[0])) * _dot(xb, wu_ref[0])
        _rows_to_slabs(ybuf.at[slot], _dot(hid.astype(BF16), wd_ref[0]))

    scatter_start(slot, nv_ref[i])

    @pl.when(i + 2 < n_tiles)
    def _():
        idx_copy(i + 2, slot).start()

    @pl.when(i == n_tiles - 1)
    def _():
        scatter_wait(other, nv_ref[jnp.maximum(i - 1, 0)])
        scatter_wait(slot, nv_ref[i])


def _moe(xn, route, w_gate, w_up, w_down):
    B, S = xn.shape[:2]
    n_tok = B * S
    tm = MOE_TILE
    tile_e, valid, idx = _moe_plan(route.reshape(n_tok, LANES), n_tok)
    n_tiles = tile_e.shape[0]
    grid_spec = pltpu.PrefetchScalarGridSpec(
        num_scalar_prefetch=2,
        grid=(n_tiles,),
        in_specs=[
            pl.BlockSpec(memory_space=pl.ANY),
            pl.BlockSpec(memory_space=pl.ANY),
            pl.BlockSpec((1, D_MODEL, D_EXPERT), lambda i, te, nv: (te[i], 0, 0)),
            pl.BlockSpec((1, D_MODEL, D_EXPERT), lambda i, te, nv: (te[i], 0, 0)),
            pl.BlockSpec((1, D_EXPERT, D_MODEL), lambda i, te, nv: (te[i], 0, 0)),
        ],
        out_specs=pl.BlockSpec(memory_space=pl.ANY),
        scratch_shapes=[
            pltpu.SMEM((2, tm), jnp.int32),
            pltpu.VMEM((2, tm, SLAB, LANES), F32),
            pltpu.VMEM((2, tm, SLAB, LANES), F32),
            pltpu.SemaphoreType.DMA((2,)),
            pltpu.SemaphoreType.DMA((2,)),
            pltpu.SemaphoreType.DMA((2,)),
        ],
    )
    out = pl.pallas_call(
        functools.partial(_moe_kernel, n_tok=n_tok),
        grid_spec=grid_spec,
        out_shape=jax.ShapeDtypeStruct((TOP_K * n_tok, SLAB, LANES), F32),
        compiler_params=_params("arbitrary"),
        name="moe_experts",
    )(tile_e, valid, idx, xn.reshape(n_tok, SLAB, LANES), w_gate.astype(BF16), w_up.astype(BF16),
      w_down.astype(BF16))
    return out.reshape(TOP_K, B, S, SLAB, LANES)


def _moe_combine(h, y_ref, route):
    lane = lax.broadcasted_iota(jnp.int32, (1, LANES), 1)
    g1 = jnp.sum(jnp.where(lane == 2, route, 0.0), axis=-1, keepdims=True)
    g2 = jnp.sum(jnp.where(lane == 3, route, 0.0), axis=-1, keepdims=True)
    return h + (g1 * _rows_from_slabs(y_ref.at[0, 0]) + g2 * _rows_from_slabs(y_ref.at[1, 0]))


def _head_rms_pairs(p, gain):
    lane = lax.broadcasted_iota(jnp.int32, (1, LANES), 1)
    first = lane < HEAD_DIM
    outs = []
    for j in range(p.shape[1] // LANES):
        s = p[:, j * LANES:(j + 1) * LANES]
        sq = s * s
        s_all = jnp.sum(sq, axis=-1, keepdims=True)
        s_lo = jnp.sum(jnp.where(first, sq, 0.0), axis=-1, keepdims=True)
        r = jnp.where(first, lax.rsqrt(s_lo * (1.0 / HEAD_DIM) + EPS),
                      lax.rsqrt((s_all - s_lo) * (1.0 / HEAD_DIM) + EPS))
        outs.append((s * r * gain).astype(BF16))
    return outs


def _dil_in_kernel(h_ref, y_ref, route_ref, kvg_ref, bg_ref, wk_ref, wv_ref, wq_ref, kgain_ref, qgain_ref,
                   h2_ref, k_ref, v_ref, q_ref):
    h2 = _moe_combine(h_ref[0], y_ref, route_ref[0])
    h2_ref[0] = h2
    xkv = _rms(h2, kvg_ref[...]).astype(BF16)
    for j, s in enumerate(_head_rms_pairs(_dot(xkv, wk_ref[...]), kgain_ref[...])):
        k_ref[0, :, j * LANES:(j + 1) * LANES] = s
    v_ref[0] = _dot(xkv, wv_ref[...]).astype(BF16)
    xq = _rms(h2, bg_ref[...]).astype(BF16)
    for j, s in enumerate(_head_rms_pairs(_dot(xq, wq_ref[...]), qgain_ref[...])):
        q_ref[0, :, j * LANES:(j + 1) * LANES] = s


def _dil_in(h, y, route, kv_norm, b_norm, kv_w, w_q, k_gain, q_gain):
    B, S, _ = h.shape
    tm = TOK_TILE
    const = lambda b, t: (0, 0)
    tok = lambda w: pl.BlockSpec((1, tm, w), lambda b, t: (b, t, 0))
    return pl.pallas_call(
        _dil_in_kernel,
        grid=(B, S // tm),
        in_specs=[
            tok(D_MODEL),
            pl.BlockSpec((TOP_K, 1, tm, SLAB, LANES), lambda b, t: (0, b, t, 0, 0)),
            tok(LANES),
            pl.BlockSpec((1, D_MODEL), const),
            pl.BlockSpec((1, D_MODEL), const),
            pl.BlockSpec((D_MODEL, B_WIDTH), const),
            pl.BlockSpec((D_MODEL, B_WIDTH), const),
            pl.BlockSpec((D_MODEL, B_WIDTH), const),
            pl.BlockSpec((1, LANES), const),
            pl.BlockSpec((1, LANES), const),
        ],
        out_specs=[tok(D_MODEL), tok(B_WIDTH), tok(B_WIDTH), tok(B_WIDTH)],
        out_shape=[
            jax.ShapeDtypeStruct((B, S, D_MODEL), F32),
            jax.ShapeDtypeStruct((B, S, B_WIDTH), BF16),
            jax.ShapeDtypeStruct((B, S, B_WIDTH), BF16),
            jax.ShapeDtypeStruct((B, S, B_WIDTH), BF16),
        ],
        compiler_params=_params("arbitrary", "arbitrary"),
        name="dil_in",
    )(h, y, route, kv_norm.reshape(1, D_MODEL), b_norm.reshape(1, D_MODEL),
      kv_w[:, :B_WIDTH].astype(BF16), kv_w[:, B_WIDTH:].astype(BF16), w_q.astype(BF16),
      jnp.tile(k_gain, 2).reshape(1, LANES), jnp.tile(q_gain * (SCALE * LOG2E), 2).reshape(1, LANES))


def _t5_bucket(dist):
    max_exact = NUM_BUCKETS // 2
    d_f = jnp.maximum(dist, max_exact).astype(F32)
    large = max_exact + (jnp.log(d_f / max_exact) / math.log(MAX_DISTANCE / max_exact)
                         * (NUM_BUCKETS - max_exact)).astype(jnp.int32)
    large = jnp.minimum(large, NUM_BUCKETS - 1)
    return jnp.where(dist < max_exact, dist, large)


def _branch_bias(rel_bias, g, d):
    a = jnp.arange(BLOCK)[:, None]
    b = jnp.arange(2 * BLOCK)[None, :]
    n = BLOCK + a - b
    band = (n >= 0) & (n <= B_WINDOWS[g] // d)
    onehot = (_t5_bucket(jnp.maximum(n, 0) * d)[..., None] == jnp.arange(NUM_BUCKETS)).astype(F32)
    table = rel_bias[:, g * B_HEADS_PER_GROUP:(g + 1) * B_HEADS_PER_GROUP].astype(F32)
    bias = jnp.einsum("abk,kh->hab", onehot, table, precision=lax.Precision.HIGHEST) * LOG2E
    return jnp.where(band[None], bias, NEG)


def _dil_attn_kernel(q_ref, kp_ref, kc_ref, vp_ref, vc_ref, bias_ref, o_ref, lse_ref):
    n = pl.program_id(2)
    lane = lax.broadcasted_iota(jnp.int32, (1, LANES), 1)
    first = lane < HEAD_DIM
    col = lax.broadcasted_iota(jnp.int32, (1, 2 * BLOCK), 1)
    dead = (n == 0) & (col < BLOCK)
    lse_all = jnp.zeros((BLOCK, LANES), F32)
    for j in range(B_HEADS_PER_GROUP // 2):
        sl = slice(j * LANES, (j + 1) * LANES)
        q = q_ref[0, :, sl]
        kk = jnp.concatenate([kp_ref[0, :, sl], kc_ref[0, :, sl]], axis=0)
        vv = jnp.concatenate([vp_ref[0, :, sl], vc_ref[0, :, sl]], axis=0)
        outs = []
        for hh in range(2):
            qm = jnp.where(first if hh == 0 else jnp.logical_not(first), q, jnp.zeros_like(q))
            s = _dot_nt(qm, kk) + jnp.where(dead, NEG, bias_ref[2 * j + hh])
            m = jnp.max(s, axis=-1, keepdims=True)
            p = jnp.exp2(s - m)
            den = jnp.sum(p, axis=-1, keepdims=True)
            outs.append(_dot((p / den).astype(BF16), vv))
            lse_all = jnp.where(lane == 2 * j + hh, m + jnp.log2(den), lse_all)
        o_ref[0, :, sl] = jnp.where(first, outs[0], outs[1]).astype(BF16)
    lse_ref[0] = lse_all


def _dil_attn(q, k, v, bias, g, d):
    B, S, _ = q.shape
    L = S // d
    w = B_OUT_WIDTH
    nblk = B_WIDTH // w
    qv, kv_, vv = (a.reshape(B, L, d * B_WIDTH) for a in (q, k, v))
    cur = lambda b, r, n: (b, n, r * nblk + g)
    prev = lambda b, r, n: (b, jnp.maximum(n - 1, 0), r * nblk + g)
    o, lse = pl.pallas_call(
        _dil_attn_kernel,
        grid=(B, d, L // BLOCK),
        in_specs=[
            pl.BlockSpec((1, BLOCK, w), cur),
            pl.BlockSpec((1, BLOCK, w), prev),
            pl.BlockSpec((1, BLOCK, w), cur),
            pl.BlockSpec((1, BLOCK, w), prev),
            pl.BlockSpec((1, BLOCK, w), cur),
            pl.BlockSpec((B_HEADS_PER_GROUP, BLOCK, 2 * BLOCK), lambda b, r, n: (0, 0, 0)),
        ],
        out_specs=[
            pl.BlockSpec((1, BLOCK, w), lambda b, r, n: (b, n, r)),
            pl.BlockSpec((1, BLOCK, LANES), lambda b, r, n: (b, n, r)),
        ],
        out_shape=[
            jax.ShapeDtypeStruct((B, L, d * w), BF16),
            jax.ShapeDtypeStruct((B, L, d * LANES), F32),
        ],
        compiler_params=_params("arbitrary", "arbitrary", "arbitrary"),
        name=f"dil_attn_{g}",
    )(qv, kv_, kv_, vv, vv, bias)
    return o.reshape(B, S, w), lse.reshape(B, S, LANES)


def _dil_out_kernel(o0_ref, o1_ref, o2_ref, l0_ref, l1_ref, l2_ref, h_ref, wo_ref, fg_ref,
                    wrh_ref, wrl_ref, br_ref, h_out_ref, xn_ref, route_ref):
    l0, l1, l2 = l0_ref[0], l1_ref[0], l2_ref[0]
    m = jnp.maximum(jnp.maximum(l0, l1), l2)
    e0, e1, e2 = jnp.exp2(l0 - m), jnp.exp2(l1 - m), jnp.exp2(l2 - m)
    den = e0 + e1 + e2
    row = lax.broadcasted_iota(jnp.int32, (LANES, B_OUT_WIDTH), 0)
    col = lax.broadcasted_iota(jnp.int32, (LANES, B_OUT_WIDTH), 1)
    spread = (jnp.right_shift(col, 6) == row).astype(BF16)

    def widen(a):
        hi = a.astype(BF16)
        lo = (a - hi.astype(F32)).astype(BF16)
        return _dot(hi, spread) + _dot(lo, spread)

    merged = (widen(e0 / den) * o0_ref[0].astype(F32) + widen(e1 / den) * o1_ref[0].astype(F32)
              + widen(e2 / den) * o2_ref[0].astype(F32))
    h = h_ref[0] + _dot(merged.astype(BF16), wo_ref[...])
    _ffn_in(h, fg_ref, wrh_ref, wrl_ref, br_ref, h_out_ref, xn_ref, route_ref)


def _dil_out(os_, lses, h, w_out, ffn_g, router):
    B, S, _ = h.shape
    tm = TOK_TILE
    wrh, wrl, br = router
    const = lambda b, t: (0, 0)
    tok = lambda w: pl.BlockSpec((1, tm, w), lambda b, t: (b, t, 0))
    out_specs, out_shape = _token_out_specs(B, S, tm)
    return pl.pallas_call(
        _dil_out_kernel,
        grid=(B, S // tm),
        in_specs=[tok(B_OUT_WIDTH)] * 3 + [tok(LANES)] * 3 + [
            tok(D_MODEL),
            pl.BlockSpec((B_OUT_WIDTH, D_MODEL), const),
            pl.BlockSpec((1, D_MODEL), const),
            pl.BlockSpec((D_MODEL, LANES), const),
            pl.BlockSpec((D_MODEL, LANES), const),
            pl.BlockSpec((1, LANES), const),
        ],
        out_specs=out_specs,
        out_shape=out_shape,
        compiler_params=_params("arbitrary", "arbitrary"),
        name="dil_out",
    )(*os_, *lses, h, w_out.astype(BF16), ffn_g.reshape(1, D_MODEL), wrh, wrl, br)


def _final_kernel(h_ref, y_ref, route_ref, o_ref):
    o_ref[0] = _moe_combine(h_ref[0], y_ref, route_ref[0])


def _final(h, y, route):
    B, S, _ = h.shape
    tm = TOK_TILE
    tok = lambda w: pl.BlockSpec((1, tm, w), lambda b, t: (b, t, 0))
    return pl.pallas_call(
        _final_kernel,
        grid=(B, S // tm),
        in_specs=[tok(D_MODEL), pl.BlockSpec((TOP_K, 1, tm, SLAB, LANES), lambda b, t: (0, b, t, 0, 0)), tok(LANES)],
        out_specs=tok(D_MODEL),
        out_shape=jax.ShapeDtypeStruct((B, S, D_MODEL), F32),
        compiler_params=_params("arbitrary", "arbitrary"),
        name="moe_final",
    )(h, y, route)


def kernel(x, a_norm, a_w_in, a_b_f, a_q_gain, a_k_gain, a_w_out, kv_norm, kv_w, kv_k_gain, rel_bias, b_norm, b_w_q, b_q_gain, b_w_out, ffn_norm, moe_w_group, moe_b_group, moe_w_expert, moe_b_expert, moe_w_gate, moe_w_up, moe_w_down):
    routers = [_router_weights(moe_w_group[l], moe_b_group[l], moe_w_expert[l], moe_b_expert[l])
               for l in range(2)]
    q, k, vt = _fox_in(x, a_norm[0], a_w_in[0], a_b_f[0], a_q_gain[0], a_k_gain[0])
    ot = _fox_attn(q, k, vt)
    h1, xn1, route1 = _fox_out(ot, x, a_w_out[0], ffn_norm[0], routers[0])
    y1 = _moe(xn1, route1, moe_w_gate[0], moe_w_up[0], moe_w_down[0])
    h2, k2, v2, q2 = _dil_in(h1, y1, route1, kv_norm, b_norm[0], kv_w, b_w_q[0], kv_k_gain, b_q_gain[0])
    outs, lses = [], []
    for g, d in enumerate(B_DILATIONS):
        o, lse = _dil_attn(q2, k2, v2, _branch_bias(rel_bias, g, d), g, d)
        outs.append(o)
        lses.append(lse)
    h3, xn3, route3 = _dil_out(outs, lses, h2, b_w_out[0], ffn_norm[1], routers[1])
    y3 = _moe(xn3, route3, moe_w_gate[1], moe_w_up[1], moe_w_down[1])
    return _final(h3, y3, route3)
```

```python
import functools
import math

import jax
import jax.numpy as jnp
from jax import lax
from jax.experimental import pallas as pl
from jax.experimental.pallas import tpu as pltpu

F32 = jnp.float32
BF16 = jnp.bfloat16

D_MODEL = 1024
HEAD_DIM = 64
A_HEADS = 16
B_GROUPS = 3
B_HEADS_PER_GROUP = 8
B_HEADS = 24
B_WIDTH = B_HEADS * HEAD_DIM
B_OUT_WIDTH = B_HEADS_PER_GROUP * HEAD_DIM
B_WINDOWS = (128, 512, 2048)
B_DILATIONS = (1, 4, 16)
BLOCK = 128
NUM_BUCKETS = 32
MAX_DISTANCE = 2048
N_GROUPS = 4
EXPERTS_PER_GROUP = 4
N_EXPERTS = 16
TOP_K = 2
D_EXPERT = 512
EPS = 1e-6
NEG = -1e30
SCALE = HEAD_DIM ** -0.5
LOG2E = 1.4426950408889634

LANES = 128
TOK_TILE = 512
VMEM_LIMIT = 56 * 1024 * 1024


def _params(*sem):
    return pltpu.CompilerParams(dimension_semantics=sem, vmem_limit_bytes=VMEM_LIMIT)


def _rms(x, g):
    return x * lax.rsqrt(jnp.mean(x * x, axis=-1, keepdims=True) + EPS) * g


def _split3(x):
    hi = x.astype(BF16).astype(F32)
    r = x - hi
    mid = r.astype(BF16).astype(F32)
    return hi, mid, r - mid


SLAB = D_MODEL // LANES


def _rows_from_slabs(ref):
    return jnp.concatenate([ref[:, c, :] for c in range(SLAB)], axis=1)


def _rows_to_slabs(ref, x):
    for c in range(SLAB):
        ref[:, c, :] = x[:, c * LANES:(c + 1) * LANES]


def _dot(a, b):
    return jnp.dot(a, b, preferred_element_type=F32)


def _dot_nt(a, b):
    return lax.dot_general(a, b, (((1,), (1,)), ((), ())), preferred_element_type=F32)


def _dot_tn(a, b):
    return lax.dot_general(a, b, (((0,), (0,)), ((), ())), preferred_element_type=F32)


def _fox_in_kernel(x_ref, g_ref, wqk_ref, wvt_ref, wf_ref, bf_ref, qg_ref, kg_ref,
                   q_ref, k_ref, vt_ref, carry_ref):
    tm = x_ref.shape[1]

    @pl.when(pl.program_id(1) == 0)
    def _():
        carry_ref[...] = jnp.zeros_like(carry_ref)

    xb = _rms(x_ref[0], g_ref[...]).astype(BF16)
    pqk = _dot(xb, wqk_ref[...])
    vt_ref[0, 0] = _dot_nt(wvt_ref[...], xb).astype(BF16)

    lane = lax.broadcasted_iota(jnp.int32, (1, LANES), 1)
    z = _dot(xb, wf_ref[...]) + bf_ref[...]
    lf = jnp.minimum(z, 0.0) - jnp.log1p(jnp.exp(-jnp.abs(z)))
    lf = jnp.where(lane < A_HEADS, lf, 0.0)
    row = lax.broadcasted_iota(jnp.int32, (tm, tm), 0)
    col = lax.broadcasted_iota(jnp.int32, (tm, tm), 1)
    tri = (col <= row).astype(BF16)
    hi, mid, lo = _split3(lf)
    cum = (_dot(tri, hi.astype(BF16)) + _dot(tri, mid.astype(BF16))
           + _dot(tri, lo.astype(BF16))) + carry_ref[...]
    carry_ref[...] = cum[tm - 1:tm, :]
    c = cum * (-LOG2E)

    head_lane = lane < HEAD_DIM
    for h in range(A_HEADS):
        sl = slice((h // 2) * LANES, (h // 2 + 1) * LANES)
        qs = pqk[:, sl]
        ks = pqk[:, A_HEADS * HEAD_DIM + sl.start:A_HEADS * HEAD_DIM + sl.stop]
        if h % 2:
            qs = pltpu.roll(qs, HEAD_DIM, 1)
            ks = pltpu.roll(ks, HEAD_DIM, 1)
        qs = jnp.where(head_lane, qs, 0.0)
        ks = jnp.where(head_lane, ks, 0.0)
        qn = qs * lax.rsqrt(jnp.sum(qs * qs, axis=-1, keepdims=True) * (1.0 / HEAD_DIM) + EPS) * qg_ref[...]
        kn = ks * lax.rsqrt(jnp.sum(ks * ks, axis=-1, keepdims=True) * (1.0 / HEAD_DIM) + EPS) * kg_ref[...]
        qa = jnp.where((lane >= HEAD_DIM) & (lane < HEAD_DIM + 3), 1.0, qn)
        c_hi, c_mid, c_lo = _split3(jnp.broadcast_to(c[:, h:h + 1], (tm, LANES)))
        ka = jnp.where(lane == HEAD_DIM, c_hi,
                       jnp.where(lane == HEAD_DIM + 1, c_mid,
                                 jnp.where(lane == HEAD_DIM + 2, c_lo, kn)))
        q_ref[0, :, h * LANES:(h + 1) * LANES] = qa.astype(BF16)
        k_ref[0, :, h * LANES:(h + 1) * LANES] = ka.astype(BF16)


def _fox_in(x, a_norm, w_in, b_f, q_gain, k_gain):
    B, S, _ = x.shape
    tm = TOK_TILE
    nt = S // tm
    aw = A_HEADS * HEAD_DIM
    wqk = w_in[:, :2 * aw].astype(BF16)
    wvt = w_in[:, 2 * aw:3 * aw].T.astype(BF16)
    wf = jnp.pad(w_in[:, 3 * aw:], ((0, 0), (0, LANES - A_HEADS))).astype(BF16)
    bf = jnp.pad(b_f, (0, LANES - A_HEADS)).reshape(1, LANES)
    qg = jnp.pad(q_gain * (SCALE * LOG2E), (0, LANES - HEAD_DIM)).reshape(1, LANES)
    kg = jnp.pad(k_gain, (0, LANES - HEAD_DIM)).reshape(1, LANES)
    const = lambda b, t: (0, 0)
    return pl.pallas_call(
        _fox_in_kernel,
        grid=(B, nt),
        in_specs=[
            pl.BlockSpec((1, tm, D_MODEL), lambda b, t: (b, t, 0)),
            pl.BlockSpec((1, D_MODEL), const),
            pl.BlockSpec((D_MODEL, 2 * aw), const),
            pl.BlockSpec((aw, D_MODEL), const),
            pl.BlockSpec((D_MODEL, LANES), const),
            pl.BlockSpec((1, LANES), const),
            pl.BlockSpec((1, LANES), const),
            pl.BlockSpec((1, LANES), const),
        ],
        out_specs=[
            pl.BlockSpec((1, tm, A_HEADS * LANES), lambda b, t: (b, t, 0)),
            pl.BlockSpec((1, tm, A_HEADS * LANES), lambda b, t: (b, t, 0)),
            pl.BlockSpec((1, 1, aw, tm), lambda b, t: (b, t, 0, 0)),
        ],
        out_shape=[
            jax.ShapeDtypeStruct((B, S, A_HEADS * LANES), BF16),
            jax.ShapeDtypeStruct((B, S, A_HEADS * LANES), BF16),
            jax.ShapeDtypeStruct((B, nt, aw, tm), BF16),
        ],
        scratch_shapes=[pltpu.VMEM((1, LANES), F32)],
        compiler_params=_params("arbitrary", "arbitrary"),
        name="fox_in",
    )(x, a_norm.reshape(1, D_MODEL), wqk, wvt, wf, bf, qg, kg)


FOX_TQ = 2048
FOX_QCHUNK = 256
FOX_UNROLL = 4
FOX_AHEAD = 4


def _fox_attn_kernel(q_ref, k_ref, vt_ref, o_ref, *scratch):
    tq = q_ref.shape[1]
    tk = vt_ref.shape[3]
    qc = FOX_QCHUNK
    nc = tq // qc
    ratio = tq // tk
    qi = pl.program_id(2)
    acc_refs, m_refs = scratch[:nc], scratch[nc:]
    for c in range(nc):
        m_refs[c][...] = jnp.full_like(m_refs[c], NEG)
        acc_refs[c][...] = jnp.zeros_like(acc_refs[c])
    ones = jnp.ones((16, tk), BF16)

    def scores(kj, c):
        return _dot_nt(kj, q_ref[0, c * qc:(c + 1) * qc, :])

    def update(st, vj, c, mask):
        if mask is not None:
            st = jnp.where(mask, st, NEG)
        m_prev = m_refs[c][...]
        m_new = jnp.maximum(m_prev, jnp.max(st, axis=0, keepdims=True))
        p = jnp.exp2((st - m_new).astype(BF16))
        alpha = jnp.exp2(m_prev - m_new)
        acc_refs[c][...] = alpha * acc_refs[c][...] + _dot(vj, p)
        m_refs[c][...] = m_new

    def load(j):
        kj = k_ref[0, pl.ds(pl.multiple_of(j * tk, tk), tk), :]
        vj = jnp.concatenate([vt_ref[0, j], ones], axis=0)
        return kj, vj

    def run(work):
        st = [None] * len(work)
        for i in range(min(FOX_AHEAD, len(work))):
            st[i] = scores(work[i][0], work[i][2])
        for i, (kj, vj, c, mask) in enumerate(work):
            if i + FOX_AHEAD < len(work):
                st[i + FOX_AHEAD] = scores(work[i + FOX_AHEAD][0], work[i + FOX_AHEAD][2])
            update(st[i], vj, c, mask)
            st[i] = None

    def body(jj, carry):
        work = []
        for u in range(FOX_UNROLL):
            kj, vj = load(jj * FOX_UNROLL + u)
            work += [(kj, vj, c, None) for c in range(nc)]
        run(work)
        return carry

    lax.fori_loop(0, qi * (ratio // FOX_UNROLL), body, 0)
    work = []
    for d in range(ratio):
        kj, vj = load(qi * ratio + d)
        for c in range(nc):
            if d * tk > (c + 1) * qc - 1:
                continue
            if (d + 1) * tk - 1 <= c * qc:
                work.append((kj, vj, c, None))
            else:
                kpos = d * tk + lax.broadcasted_iota(jnp.int32, (tk, qc), 0)
                qpos = c * qc + lax.broadcasted_iota(jnp.int32, (tk, qc), 1)
                work.append((kj, vj, c, kpos <= qpos))
    run(work)
    for c in range(nc):
        acc = acc_refs[c][...]
        o_ref[0, :, c * qc:(c + 1) * qc] = (acc[:HEAD_DIM] / acc[HEAD_DIM:HEAD_DIM + 1]).astype(BF16)


def _fox_attn(q, k, vt):
    B, S, _ = q.shape
    nt, tk = vt.shape[1], vt.shape[3]
    tq = FOX_TQ
    assert tq % (tk * FOX_UNROLL) == 0 and tq % FOX_QCHUNK == 0
    return pl.pallas_call(
        _fox_attn_kernel,
        grid=(B, A_HEADS, S // tq),
        in_specs=[
            pl.BlockSpec((1, tq, LANES), lambda b, h, i: (b, i, h)),
            pl.BlockSpec((1, S, LANES), lambda b, h, i: (b, 0, h)),
            pl.BlockSpec((1, nt, HEAD_DIM, tk), lambda b, h, i: (b, 0, h, 0)),
        ],
        out_specs=pl.BlockSpec((1, HEAD_DIM, tq), lambda b, h, i: (b, h, i)),
        out_shape=jax.ShapeDtypeStruct((B, A_HEADS * HEAD_DIM, S), BF16),
        scratch_shapes=([pltpu.VMEM((HEAD_DIM + 16, FOX_QCHUNK), F32)] * (tq // FOX_QCHUNK)
                        + [pltpu.VMEM((1, FOX_QCHUNK), F32)] * (tq // FOX_QCHUNK)),
        compiler_params=_params("arbitrary", "arbitrary", "arbitrary"),
        name="fox_attn",
    )(q, k, vt)


def _route(xn, wrh_ref, wrl_ref, br_ref):
    xh = xn.astype(BF16)
    xl = (xn - xh.astype(F32)).astype(BF16)
    wh = wrh_ref[...]
    logits = _dot(xh, wh) + _dot(xl, wh) + _dot(xh, wrl_ref[...]) + br_ref[...]
    lane = lax.broadcasted_iota(jnp.int32, (1, LANES), 1)
    lanef = lane.astype(F32)
    far = float(LANES)

    gl = jnp.where(lane < N_GROUPS, logits, NEG)
    gm = jnp.max(gl, axis=-1, keepdims=True)
    g_val = 1.0 / jnp.sum(jnp.exp(gl - gm), axis=-1, keepdims=True)
    g_idx = jnp.min(jnp.where(gl == gm, lanef, far), axis=-1, keepdims=True)

    lo = N_GROUPS + EXPERTS_PER_GROUP * g_idx
    el = jnp.where((lanef >= lo) & (lanef < lo + EXPERTS_PER_GROUP), logits, NEG)
    em1 = jnp.max(el, axis=-1, keepdims=True)
    ez = jnp.sum(jnp.exp(el - em1), axis=-1, keepdims=True)
    i1 = jnp.min(jnp.where(el == em1, lanef, far), axis=-1, keepdims=True)
    el2 = jnp.where(lanef == i1, NEG, el)
    em2 = jnp.max(el2, axis=-1, keepdims=True)
    i2 = jnp.min(jnp.where(el2 == em2, lanef, far), axis=-1, keepdims=True)
    p1 = 1.0 / ez
    p2 = jnp.exp(em2 - em1) / ez
    den = p1 + p2
    gate1 = g_val * (p1 / den)
    gate2 = g_val * (p2 / den)
    return jnp.where(lane == 0, i1 - N_GROUPS,
                     jnp.where(lane == 1, i2 - N_GROUPS,
                               jnp.where(lane == 2, gate1,
                                         jnp.where(lane == 3, gate2, 0.0))))


def _router_weights(w_group, b_group, w_expert, b_expert):
    w = jnp.pad(jnp.concatenate([w_group, w_expert], axis=1),
                ((0, 0), (0, LANES - N_GROUPS - N_EXPERTS)))
    b = jnp.pad(jnp.concatenate([b_group, b_expert]), (0, LANES - N_GROUPS - N_EXPERTS))
    wh = w.astype(BF16)
    wl = (w - wh.astype(F32)).astype(BF16)
    return wh, wl, b.reshape(1, LANES)


def _ffn_in(h, fg_ref, wrh_ref, wrl_ref, br_ref, h_ref, xn_ref, route_ref):
    h_ref[0] = h
    xn = _rms(h, fg_ref[...])
    _rows_to_slabs(xn_ref.at[0], xn)
    route_ref[0] = _route(xn, wrh_ref, wrl_ref, br_ref)


def _fox_out_kernel(ot_ref, x_ref, wo_ref, fg_ref, wrh_ref, wrl_ref, br_ref,
                    h_ref, xn_ref, route_ref):
    h = x_ref[0] + _dot_tn(ot_ref[0], wo_ref[...])
    _ffn_in(h, fg_ref, wrh_ref, wrl_ref, br_ref, h_ref, xn_ref, route_ref)


def _token_out_specs(B, S, tm):
    specs = [
        pl.BlockSpec((1, tm, D_MODEL), lambda b, t: (b, t, 0)),
        pl.BlockSpec((1, tm, SLAB, LANES), lambda b, t: (b, t, 0, 0)),
        pl.BlockSpec((1, tm, LANES), lambda b, t: (b, t, 0)),
    ]
    shapes = [
        jax.ShapeDtypeStruct((B, S, D_MODEL), F32),
        jax.ShapeDtypeStruct((B, S, SLAB, LANES), F32),
        jax.ShapeDtypeStruct((B, S, LANES), F32),
    ]
    return specs, shapes


def _fox_out(ot, x, w_out, ffn_g, router):
    B, S, _ = x.shape
    tm = TOK_TILE
    wrh, wrl, br = router
    const = lambda b, t: (0, 0)
    out_specs, out_shape = _token_out_specs(B, S, tm)
    return pl.pallas_call(
        _fox_out_kernel,
        grid=(B, S // tm),
        in_specs=[
            pl.BlockSpec((1, A_HEADS * HEAD_DIM, tm), lambda b, t: (b, 0, t)),
            pl.BlockSpec((1, tm, D_MODEL), lambda b, t: (b, t, 0)),
            pl.BlockSpec((A_HEADS * HEAD_DIM, D_MODEL), const),
            pl.BlockSpec((1, D_MODEL), const),
            pl.BlockSpec((D_MODEL, LANES), const),
            pl.BlockSpec((D_MODEL, LANES), const),
            pl.BlockSpec((1, LANES), const),
        ],
        out_specs=out_specs,
        out_shape=out_shape,
        compiler_params=_params("arbitrary", "arbitrary"),
        name="fox_out",
    )(ot, x, w_out.astype(BF16), ffn_g.reshape(1, D_MODEL), wrh, wrl, br)


MOE_TILE = 256


def _moe_plan(route, n_tok):
    tm = MOE_TILE
    n_pairs = TOP_K * n_tok
    n_tiles = n_pairs // tm + N_EXPERTS
    pair_e = route[:, :TOP_K].astype(jnp.int32).T.reshape(-1)
    order = jnp.argsort(pair_e, stable=True).astype(jnp.int32)
    counts = jnp.sum(pair_e[:, None] == jnp.arange(N_EXPERTS)[None, :], axis=0).astype(jnp.int32)
    tiles_per = (counts + tm - 1) // tm
    tile_end = jnp.cumsum(tiles_per)
    n_used = tile_end[-1]
    tid = jnp.arange(n_tiles, dtype=jnp.int32)
    tid_c = jnp.minimum(tid, n_used - 1)
    tile_e = jnp.minimum(jnp.searchsorted(tile_end, tid_c, side="right"), N_EXPERTS - 1).astype(jnp.int32)
    within = (tid_c - (tile_end - tiles_per)[tile_e]) * tm
    valid = jnp.where(tid < n_used, jnp.clip(counts[tile_e] - within, 0, tm), 0).astype(jnp.int32)
    src = (jnp.cumsum(counts) - counts)[tile_e][:, None] + within[:, None] + jnp.arange(tm, dtype=jnp.int32)[None, :]
    live = jnp.arange(tm, dtype=jnp.int32)[None, :] < valid[:, None]
    pair = order[jnp.clip(src, 0, n_pairs - 1)]
    dump = n_pairs + (tid % 2)[:, None] * tm + jnp.arange(tm, dtype=jnp.int32)[None, :]
    tok = jnp.where(live, pair % n_tok, 0).astype(jnp.int32)
    dst = jnp.where(live, pair, dump).astype(jnp.int32)
    return tile_e, valid, tok, dst


def _moe_kernel(te_ref, nv_ref, tok_hbm, dst_hbm, xn_hbm, wg_ref, wu_ref, wd_ref, out_hbm,
                tok_s, dst_s, xbuf, ybuf, isem, gsem, ssem):
    del te_ref
    groups = xbuf.shape[1]
    tm = groups * 8
    i = pl.program_id(0)
    n_tiles = pl.num_programs(0)
    slot = lax.rem(i, 2)
    other = 1 - slot

    def idx_copies(t, s):
        rows = pl.ds(s * tm, tm)
        return (pltpu.make_async_copy(tok_hbm.at[t], tok_s.at[rows], isem.at[0, s]),
                pltpu.make_async_copy(dst_hbm.at[t], dst_s.at[rows], isem.at[1, s]))

    def gather_start(s):
        def body(g, c):
            base = s * tm + g * 8
            for j in range(8):
                pltpu.make_async_copy(xn_hbm.at[tok_s[base + j]], xbuf.at[s, g, :, j, :], gsem.at[s]).start()
            return c
        lax.fori_loop(0, groups, body, 0)

    def scatter_start(s):
        def body(g, c):
            base = s * tm + g * 8
            for j in range(8):
                pltpu.make_async_copy(ybuf.at[s, g, :, j, :], out_hbm.at[dst_s[base + j]], ssem.at[s]).start()
            return c
        lax.fori_loop(0, groups, body, 0)

    def wait_rows(buf, sem, s):
        pltpu.make_async_copy(buf.at[s], buf.at[s], sem.at[s]).wait()

    @pl.when(i == 0)
    def _():
        for cp in idx_copies(0, 0):
            cp.start()
        for cp in idx_copies(0, 0):
            cp.wait()
        gather_start(0)
        for cp in idx_copies(1, 1):
            cp.start()

    @pl.when(i + 1 < n_tiles)
    def _():
        for cp in idx_copies(i + 1, other):
            cp.wait()
        gather_start(other)

    wait_rows(xbuf, gsem, slot)

    @pl.when(i >= 2)
    def _():
        wait_rows(ybuf, ssem, slot)

    @pl.when(nv_ref[i] > 0)
    def _():
        xb = jnp.concatenate([xbuf[slot, :, c].reshape(tm, LANES) for c in range(SLAB)], axis=1).astype(BF16)
        hid = jax.nn.silu(_dot(xb, wg_ref[0])) * _dot(xb, wu_ref[0])
        y = _dot(hid.astype(BF16), wd_ref[0])
        for c in range(SLAB):
            ybuf[slot, :, c] = y[:, c * LANES:(c + 1) * LANES].reshape(groups, 8, LANES)

    scatter_start(slot)

    @pl.when(i + 2 < n_tiles)
    def _():
        for cp in idx_copies(i + 2, slot):
            cp.start()

    @pl.when(i == n_tiles - 1)
    def _():
        wait_rows(ybuf, ssem, other)
        wait_rows(ybuf, ssem, slot)


def _moe(xn, route, w_gate, w_up, w_down):
    B, S = xn.shape[:2]
    n_tok = B * S
    tm = MOE_TILE
    tile_e, valid, tok, dst = _moe_plan(route.reshape(n_tok, LANES), n_tok)
    n_tiles = tile_e.shape[0]
    grid_spec = pltpu.PrefetchScalarGridSpec(
        num_scalar_prefetch=2,
        grid=(n_tiles,),
        in_specs=[
            pl.BlockSpec(memory_space=pl.ANY),
            pl.BlockSpec(memory_space=pl.ANY),
            pl.BlockSpec(memory_space=pl.ANY),
            pl.BlockSpec((1, D_MODEL, D_EXPERT), lambda i, te, nv: (te[i], 0, 0)),
            pl.BlockSpec((1, D_MODEL, D_EXPERT), lambda i, te, nv: (te[i], 0, 0)),
            pl.BlockSpec((1, D_EXPERT, D_MODEL), lambda i, te, nv: (te[i], 0, 0)),
        ],
        out_specs=pl.BlockSpec(memory_space=pl.ANY),
        scratch_shapes=[
            pltpu.SMEM((2 * tm,), jnp.int32),
            pltpu.SMEM((2 * tm,), jnp.int32),
            pltpu.VMEM((2, tm // 8, SLAB, 8, LANES), F32),
            pltpu.VMEM((2, tm // 8, SLAB, 8, LANES), F32),
            pltpu.SemaphoreType.DMA((2, 2)),
            pltpu.SemaphoreType.DMA((2,)),
            pltpu.SemaphoreType.DMA((2,)),
        ],
    )
    return pl.pallas_call(
        _moe_kernel,
        grid_spec=grid_spec,
        out_shape=jax.ShapeDtypeStruct((TOP_K * n_tok + 2 * tm, SLAB, LANES), F32),
        compiler_params=_params("arbitrary"),
        name="moe_experts",
    )(tile_e, valid, tok, dst, xn.reshape(n_tok, SLAB, LANES), w_gate.astype(BF16), w_up.astype(BF16),
      w_down.astype(BF16))


def _moe_combine(h, y0_ref, y1_ref, route):
    lane = lax.broadcasted_iota(jnp.int32, (1, LANES), 1)
    g1 = jnp.sum(jnp.where(lane == 2, route, 0.0), axis=-1, keepdims=True)
    g2 = jnp.sum(jnp.where(lane == 3, route, 0.0), axis=-1, keepdims=True)
    return h + (g1 * _rows_from_slabs(y0_ref) + g2 * _rows_from_slabs(y1_ref))


def _moe_out_specs(B, S, tm):
    per_b = S // tm
    return [pl.BlockSpec((tm, SLAB, LANES), lambda b, t, k=k: (k * B * per_b + b * per_b + t, 0, 0))
            for k in range(TOP_K)]


def _head_rms_pairs(p, gain):
    lane = lax.broadcasted_iota(jnp.int32, (1, LANES), 1)
    first = lane < HEAD_DIM
    outs = []
    for j in range(p.shape[1] // LANES):
        s = p[:, j * LANES:(j + 1) * LANES]
        sq = s * s
        s_all = jnp.sum(sq, axis=-1, keepdims=True)
        s_lo = jnp.sum(jnp.where(first, sq, 0.0), axis=-1, keepdims=True)
        r = jnp.where(first, lax.rsqrt(s_lo * (1.0 / HEAD_DIM) + EPS),
                      lax.rsqrt((s_all - s_lo) * (1.0 / HEAD_DIM) + EPS))
        outs.append((s * r * gain).astype(BF16))
    return outs


def _dil_in_kernel(h_ref, y0_ref, y1_ref, route_ref, kvg_ref, bg_ref, wk_ref, wv_ref, wq_ref, kgain_ref,
                   qgain_ref, h2_ref, k_ref, v_ref, q_ref):
    h2 = _moe_combine(h_ref[0], y0_ref, y1_ref, route_ref[0])
    h2_ref[0] = h2
    xkv = _rms(h2, kvg_ref[...]).astype(BF16)
    for j, s in enumerate(_head_rms_pairs(_dot(xkv, wk_ref[...]), kgain_ref[...])):
        k_ref[0, :, j * LANES:(j + 1) * LANES] = s
    v_ref[0] = _dot(xkv, wv_ref[...]).astype(BF16)
    xq = _rms(h2, bg_ref[...]).astype(BF16)
    for j, s in enumerate(_head_rms_pairs(_dot(xq, wq_ref[...]), qgain_ref[...])):
        q_ref[0, :, j * LANES:(j + 1) * LANES] = s


def _dil_in(h, y, route, kv_norm, b_norm, kv_w, w_q, k_gain, q_gain):
    B, S, _ = h.shape
    tm = TOK_TILE
    const = lambda b, t: (0, 0)
    tok = lambda w: pl.BlockSpec((1, tm, w), lambda b, t: (b, t, 0))
    return pl.pallas_call(
        _dil_in_kernel,
        grid=(B, S // tm),
        in_specs=[
            tok(D_MODEL),
            *_moe_out_specs(B, S, tm),
            tok(LANES),
            pl.BlockSpec((1, D_MODEL), const),
            pl.BlockSpec((1, D_MODEL), const),
            pl.BlockSpec((D_MODEL, B_WIDTH), const),
            pl.BlockSpec((D_MODEL, B_WIDTH), const),
            pl.BlockSpec((D_MODEL, B_WIDTH), const),
            pl.BlockSpec((1, LANES), const),
            pl.BlockSpec((1, LANES), const),
        ],
        out_specs=[tok(D_MODEL), tok(B_WIDTH), tok(B_WIDTH), tok(B_WIDTH)],
        out_shape=[
            jax.ShapeDtypeStruct((B, S, D_MODEL), F32),
            jax.ShapeDtypeStruct((B, S, B_WIDTH), BF16),
            jax.ShapeDtypeStruct((B, S, B_WIDTH), BF16),
            jax.ShapeDtypeStruct((B, S, B_WIDTH), BF16),
        ],
        compiler_params=_params("arbitrary", "arbitrary"),
        name="dil_in",
    )(h, y, y, route, kv_norm.reshape(1, D_MODEL), b_norm.reshape(1, D_MODEL),
      kv_w[:, :B_WIDTH].astype(BF16), kv_w[:, B_WIDTH:].astype(BF16), w_q.astype(BF16),
      jnp.tile(k_gain, 2).reshape(1, LANES), jnp.tile(q_gain * (SCALE * LOG2E), 2).reshape(1, LANES))


def _t5_bucket(dist):
    max_exact = NUM_BUCKETS // 2
    d_f = jnp.maximum(dist, max_exact).astype(F32)
    large = max_exact + (jnp.log(d_f / max_exact) / math.log(MAX_DISTANCE / max_exact)
                         * (NUM_BUCKETS - max_exact)).astype(jnp.int32)
    large = jnp.minimum(large, NUM_BUCKETS - 1)
    return jnp.where(dist < max_exact, dist, large)


def _branch_bias(rel_bias, g, d):
    a = jnp.arange(BLOCK)[:, None]
    b = jnp.arange(2 * BLOCK)[None, :]
    n = BLOCK + a - b
    band = (n >= 0) & (n <= B_WINDOWS[g] // d)
    onehot = (_t5_bucket(jnp.maximum(n, 0) * d)[..., None] == jnp.arange(NUM_BUCKETS)).astype(F32)
    table = rel_bias[:, g * B_HEADS_PER_GROUP:(g + 1) * B_HEADS_PER_GROUP].astype(F32)
    bias = jnp.einsum("abk,kh->hab", onehot, table, precision=lax.Precision.HIGHEST) * LOG2E
    return jnp.where(band[None], bias, NEG)


def _dil_attn_kernel(q_ref, kp_ref, kc_ref, vp_ref, vc_ref, bias_ref, o_ref, lse_ref):
    n = pl.program_id(2)
    lane = lax.broadcasted_iota(jnp.int32, (1, LANES), 1)
    first = lane < HEAD_DIM
    col = lax.broadcasted_iota(jnp.int32, (1, 2 * BLOCK), 1)
    dead = (n == 0) & (col < BLOCK)
    lse_all = jnp.zeros((BLOCK, LANES), F32)
    for j in range(B_HEADS_PER_GROUP // 2):
        sl = slice(j * LANES, (j + 1) * LANES)
        q = q_ref[0, :, sl]
        kk = jnp.concatenate([kp_ref[0, :, sl], kc_ref[0, :, sl]], axis=0)
        vv = jnp.concatenate([vp_ref[0, :, sl], vc_ref[0, :, sl]], axis=0)
        outs = []
        for hh in range(2):
            qm = jnp.where(first if hh == 0 else jnp.logical_not(first), q, jnp.zeros_like(q))
            s = _dot_nt(qm, kk) + jnp.where(dead, NEG, bias_ref[2 * j + hh])
            m = jnp.max(s, axis=-1, keepdims=True)
            p = jnp.exp2(s - m)
            den = jnp.sum(p, axis=-1, keepdims=True)
            outs.append(_dot((p / den).astype(BF16), vv))
            lse_all = jnp.where(lane == 2 * j + hh, m + jnp.log2(den), lse_all)
        o_ref[0, :, sl] = jnp.where(first, outs[0], outs[1]).astype(BF16)
    lse_ref[0] = lse_all


def _dil_attn(q, k, v, bias, g, d):
    B, S, _ = q.shape
    L = S // d
    w = B_OUT_WIDTH
    nblk = B_WIDTH // w
    qv, kv_, vv = (a.reshape(B, L, d * B_WIDTH) for a in (q, k, v))
    cur = lambda b, r, n: (b, n, r * nblk + g)
    prev = lambda b, r, n: (b, jnp.maximum(n - 1, 0), r * nblk + g)
    o, lse = pl.pallas_call(
        _dil_attn_kernel,
        grid=(B, d, L // BLOCK),
        in_specs=[
            pl.BlockSpec((1, BLOCK, w), cur),
            pl.BlockSpec((1, BLOCK, w), prev),
            pl.BlockSpec((1, BLOCK, w), cur),
            pl.BlockSpec((1, BLOCK, w), prev),
            pl.BlockSpec((1, BLOCK, w), cur),
            pl.BlockSpec((B_HEADS_PER_GROUP, BLOCK, 2 * BLOCK), lambda b, r, n: (0, 0, 0)),
        ],
        out_specs=[
            pl.BlockSpec((1, BLOCK, w), lambda b, r, n: (b, n, r)),
            pl.BlockSpec((1, BLOCK, LANES), lambda b, r, n: (b, n, r)),
        ],
        out_shape=[
            jax.ShapeDtypeStruct((B, L, d * w), BF16),
            jax.ShapeDtypeStruct((B, L, d * LANES), F32),
        ],
        compiler_params=_params("arbitrary", "arbitrary", "arbitrary"),
        name=f"dil_attn_{g}",
    )(qv, kv_, kv_, vv, vv, bias)
    return o.reshape(B, S, w), lse.reshape(B, S, LANES)


def _dil_out_kernel(o0_ref, o1_ref, o2_ref, l0_ref, l1_ref, l2_ref, h_ref, wo_ref, fg_ref,
                    wrh_ref, wrl_ref, br_ref, h_out_ref, xn_ref, route_ref):
    l0, l1, l2 = l0_ref[0], l1_ref[0], l2_ref[0]
    m = jnp.maximum(jnp.maximum(l0, l1), l2)
    e0, e1, e2 = jnp.exp2(l0 - m), jnp.exp2(l1 - m), jnp.exp2(l2 - m)
    den = e0 + e1 + e2
    row = lax.broadcasted_iota(jnp.int32, (LANES, B_OUT_WIDTH), 0)
    col = lax.broadcasted_iota(jnp.int32, (LANES, B_OUT_WIDTH), 1)
    spread = (jnp.right_shift(col, 6) == row).astype(BF16)

    def widen(a):
        hi = a.astype(BF16)
        lo = (a - hi.astype(F32)).astype(BF16)
        return _dot(hi, spread) + _dot(lo, spread)

    merged = (widen(e0 / den) * o0_ref[0].astype(F32) + widen(e1 / den) * o1_ref[0].astype(F32)
              + widen(e2 / den) * o2_ref[0].astype(F32))
    h = h_ref[0] + _dot(merged.astype(BF16), wo_ref[...])
    _ffn_in(h, fg_ref, wrh_ref, wrl_ref, br_ref, h_out_ref, xn_ref, route_ref)


def _dil_out(os_, lses, h, w_out, ffn_g, router):
    B, S, _ = h.shape
    tm = TOK_TILE
    wrh, wrl, br = router
    const = lambda b, t: (0, 0)
    tok = lambda w: pl.BlockSpec((1, tm, w), lambda b, t: (b, t, 0))
    out_specs, out_shape = _token_out_specs(B, S, tm)
    return pl.pallas_call(
        _dil_out_kernel,
        grid=(B, S // tm),
        in_specs=[tok(B_OUT_WIDTH)] * 3 + [tok(LANES)] * 3 + [
            tok(D_MODEL),
            pl.BlockSpec((B_OUT_WIDTH, D_MODEL), const),
            pl.BlockSpec((1, D_MODEL), const),
            pl.BlockSpec((D_MODEL, LANES), const),
            pl.BlockSpec((D_MODEL, LANES), const),
            pl.BlockSpec((1, LANES), const),
        ],
        out_specs=out_specs,
        out_shape=out_shape,
        compiler_params=_params("arbitrary", "arbitrary"),
        name="dil_out",
    )(*os_, *lses, h, w_out.astype(BF16), ffn_g.reshape(1, D_MODEL), wrh, wrl, br)


def _final_kernel(h_ref, y0_ref, y1_ref, route_ref, o_ref):
    o_ref[0] = _moe_combine(h_ref[0], y0_ref, y1_ref, route_ref[0])


def _final(h, y, route):
    B, S, _ = h.shape
    tm = TOK_TILE
    tok = lambda w: pl.BlockSpec((1, tm, w), lambda b, t: (b, t, 0))
    return pl.pallas_call(
        _final_kernel,
        grid=(B, S // tm),
        in_specs=[tok(D_MODEL), *_moe_out_specs(B, S, tm), tok(LANES)],
        out_specs=tok(D_MODEL),
        out_shape=jax.ShapeDtypeStruct((B, S, D_MODEL), F32),
        compiler_params=_params("arbitrary", "arbitrary"),
        name="moe_final",
    )(h, y, y, route)


def kernel(x, a_norm, a_w_in, a_b_f, a_q_gain, a_k_gain, a_w_out, kv_norm, kv_w, kv_k_gain, rel_bias, b_norm, b_w_q, b_q_gain, b_w_out, ffn_norm, moe_w_group, moe_b_group, moe_w_expert, moe_b_expert, moe_w_gate, moe_w_up, moe_w_down):
    routers = [_router_weights(moe_w_group[l], moe_b_group[l], moe_w_expert[l], moe_b_expert[l])
               for l in range(2)]
    q, k, vt = _fox_in(x, a_norm[0], a_w_in[0], a_b_f[0], a_q_gain[0], a_k_gain[0])
    ot = _fox_attn(q, k, vt)
    h1, xn1, route1 = _fox_out(ot, x, a_w_out[0], ffn_norm[0], routers[0])
    y1 = _moe(xn1, route1, moe_w_gate[0], moe_w_up[0], moe_w_down[0])
    h2, k2, v2, q2 = _dil_in(h1, y1, route1, kv_norm, b_norm[0], kv_w, b_w_q[0], kv_k_gain, b_q_gain[0])
    outs, lses = [], []
    for g, d in enumerate(B_DILATIONS):
        o, lse = _dil_attn(q2, k2, v2, _branch_bias(rel_bias, g, d), g, d)
        outs.append(o)
        lses.append(lse)
    h3, xn3, route3 = _dil_out(outs, lses, h2, b_w_out[0], ffn_norm[1], routers[1])
    y3 = _moe(xn3, route3, moe_w_gate[1], moe_w_up[1], moe_w_down[1])
    return _final(h3, y3, route3)
```

```python
import functools
import math

import jax
import jax.numpy as jnp
from jax import lax
from jax.experimental import pallas as pl
from jax.experimental.pallas import tpu as pltpu

F32 = jnp.float32
BF16 = jnp.bfloat16

D_MODEL = 1024
HEAD_DIM = 64
A_HEADS = 16
B_GROUPS = 3
B_HEADS_PER_GROUP = 8
B_HEADS = 24
B_WIDTH = B_HEADS * HEAD_DIM
B_OUT_WIDTH = B_HEADS_PER_GROUP * HEAD_DIM
B_WINDOWS = (128, 512, 2048)
B_DILATIONS = (1, 4, 16)
BLOCK = 128
NUM_BUCKETS = 32
MAX_DISTANCE = 2048
N_GROUPS = 4
EXPERTS_PER_GROUP = 4
N_EXPERTS = 16
TOP_K = 2
D_EXPERT = 512
EPS = 1e-6
NEG = -1e30
SCALE = HEAD_DIM ** -0.5
LOG2E = 1.4426950408889634

LANES = 128
TOK_TILE = 512
VMEM_LIMIT = 56 * 1024 * 1024


def _params(*sem):
    return pltpu.CompilerParams(dimension_semantics=sem, vmem_limit_bytes=VMEM_LIMIT)


def _rms(x, g):
    return x * lax.rsqrt(jnp.mean(x * x, axis=-1, keepdims=True) + EPS) * g


def _split3(x):
    hi = x.astype(BF16).astype(F32)
    r = x - hi
    mid = r.astype(BF16).astype(F32)
    return hi, mid, r - mid


SLAB = D_MODEL // LANES


def _rows_from_slabs(ref):
    return jnp.concatenate([ref[:, c, :] for c in range(SLAB)], axis=1)


def _rows_to_slabs(ref, x):
    for c in range(SLAB):
        ref[:, c, :] = x[:, c * LANES:(c + 1) * LANES]


def _dot(a, b):
    return jnp.dot(a, b, preferred_element_type=F32)


def _dot_nt(a, b):
    return lax.dot_general(a, b, (((1,), (1,)), ((), ())), preferred_element_type=F32)


def _dot_tn(a, b):
    return lax.dot_general(a, b, (((0,), (0,)), ((), ())), preferred_element_type=F32)


def _fox_in_kernel(x_ref, g_ref, wqk_ref, wvt_ref, wf_ref, bf_ref, qg_ref, kg_ref,
                   q_ref, k_ref, vt_ref, carry_ref):
    tm = x_ref.shape[1]

    @pl.when(pl.program_id(1) == 0)
    def _():
        carry_ref[...] = jnp.zeros_like(carry_ref)

    xb = _rms(x_ref[0], g_ref[...]).astype(BF16)
    pqk = _dot(xb, wqk_ref[...])
    vt_ref[0, 0] = _dot_nt(wvt_ref[...], xb).astype(BF16)

    lane = lax.broadcasted_iota(jnp.int32, (1, LANES), 1)
    z = _dot(xb, wf_ref[...]) + bf_ref[...]
    lf = jnp.minimum(z, 0.0) - jnp.log1p(jnp.exp(-jnp.abs(z)))
    lf = jnp.where(lane < A_HEADS, lf, 0.0)
    row = lax.broadcasted_iota(jnp.int32, (tm, tm), 0)
    col = lax.broadcasted_iota(jnp.int32, (tm, tm), 1)
    tri = (col <= row).astype(BF16)
    hi, mid, lo = _split3(lf)
    cum = (_dot(tri, hi.astype(BF16)) + _dot(tri, mid.astype(BF16))
           + _dot(tri, lo.astype(BF16))) + carry_ref[...]
    carry_ref[...] = cum[tm - 1:tm, :]
    c = cum * (-LOG2E)

    head_lane = lane < HEAD_DIM
    for h in range(A_HEADS):
        sl = slice((h // 2) * LANES, (h // 2 + 1) * LANES)
        qs = pqk[:, sl]
        ks = pqk[:, A_HEADS * HEAD_DIM + sl.start:A_HEADS * HEAD_DIM + sl.stop]
        if h % 2:
            qs = pltpu.roll(qs, HEAD_DIM, 1)
            ks = pltpu.roll(ks, HEAD_DIM, 1)
        qs = jnp.where(head_lane, qs, 0.0)
        ks = jnp.where(head_lane, ks, 0.0)
        qn = qs * lax.rsqrt(jnp.sum(qs * qs, axis=-1, keepdims=True) * (1.0 / HEAD_DIM) + EPS) * qg_ref[...]
        kn = ks * lax.rsqrt(jnp.sum(ks * ks, axis=-1, keepdims=True) * (1.0 / HEAD_DIM) + EPS) * kg_ref[...]
        qa = jnp.where((lane >= HEAD_DIM) & (lane < HEAD_DIM + 3), 1.0, qn)
        c_hi, c_mid, c_lo = _split3(jnp.broadcast_to(c[:, h:h + 1], (tm, LANES)))
        ka = jnp.where(lane == HEAD_DIM, c_hi,
                       jnp.where(lane == HEAD_DIM + 1, c_mid,
                                 jnp.where(lane == HEAD_DIM + 2, c_lo, kn)))
        q_ref[0, :, h * LANES:(h + 1) * LANES] = qa.astype(BF16)
        k_ref[0, :, h * LANES:(h + 1) * LANES] = ka.astype(BF16)


def _fox_in(x, a_norm, w_in, b_f, q_gain, k_gain):
    B, S, _ = x.shape
    tm = TOK_TILE
    nt = S // tm
    aw = A_HEADS * HEAD_DIM
    wqk = w_in[:, :2 * aw].astype(BF16)
    wvt = w_in[:, 2 * aw:3 * aw].T.astype(BF16)
    wf = jnp.pad(w_in[:, 3 * aw:], ((0, 0), (0, LANES - A_HEADS))).astype(BF16)
    bf = jnp.pad(b_f, (0, LANES - A_HEADS)).reshape(1, LANES)
    qg = jnp.pad(q_gain * (SCALE * LOG2E), (0, LANES - HEAD_DIM)).reshape(1, LANES)
    kg = jnp.pad(k_gain, (0, LANES - HEAD_DIM)).reshape(1, LANES)
    const = lambda b, t: (0, 0)
    return pl.pallas_call(
        _fox_in_kernel,
        grid=(B, nt),
        in_specs=[
            pl.BlockSpec((1, tm, D_MODEL), lambda b, t: (b, t, 0)),
            pl.BlockSpec((1, D_MODEL), const),
            pl.BlockSpec((D_MODEL, 2 * aw), const),
            pl.BlockSpec((aw, D_MODEL), const),
            pl.BlockSpec((D_MODEL, LANES), const),
            pl.BlockSpec((1, LANES), const),
            pl.BlockSpec((1, LANES), const),
            pl.BlockSpec((1, LANES), const),
        ],
        out_specs=[
            pl.BlockSpec((1, tm, A_HEADS * LANES), lambda b, t: (b, t, 0)),
            pl.BlockSpec((1, tm, A_HEADS * LANES), lambda b, t: (b, t, 0)),
            pl.BlockSpec((1, 1, aw, tm), lambda b, t: (b, t, 0, 0)),
        ],
        out_shape=[
            jax.ShapeDtypeStruct((B, S, A_HEADS * LANES), BF16),
            jax.ShapeDtypeStruct((B, S, A_HEADS * LANES), BF16),
            jax.ShapeDtypeStruct((B, nt, aw, tm), BF16),
        ],
        scratch_shapes=[pltpu.VMEM((1, LANES), F32)],
        compiler_params=_params("arbitrary", "arbitrary"),
        name="fox_in",
    )(x, a_norm.reshape(1, D_MODEL), wqk, wvt, wf, bf, qg, kg)


FOX_TQ = 2048
FOX_QCHUNK = 256
FOX_UNROLL = 4
FOX_AHEAD = 4


def _fox_attn_kernel(q_ref, k_ref, vt_ref, o_ref, *scratch):
    tq = q_ref.shape[1]
    tk = vt_ref.shape[3]
    qc = FOX_QCHUNK
    nc = tq // qc
    ratio = tq // tk
    qi = pl.program_id(2)
    acc_refs, m_refs = scratch[:nc], scratch[nc:]
    for c in range(nc):
        m_refs[c][...] = jnp.full_like(m_refs[c], NEG)
        acc_refs[c][...] = jnp.zeros_like(acc_refs[c])
    ones = jnp.ones((16, tk), BF16)

    def scores(kj, c):
        return _dot_nt(kj, q_ref[0, c * qc:(c + 1) * qc, :])

    def update(st, vj, c, mask):
        if mask is not None:
            st = jnp.where(mask, st, NEG)
        m_prev = m_refs[c][...]
        m_new = jnp.maximum(m_prev, jnp.max(st, axis=0, keepdims=True))
        p = jnp.exp2((st - m_new).astype(BF16))
        alpha = jnp.exp2(m_prev - m_new)
        acc_refs[c][...] = alpha * acc_refs[c][...] + _dot(vj, p)
        m_refs[c][...] = m_new

    def load(j):
        kj = k_ref[0, pl.ds(pl.multiple_of(j * tk, tk), tk), :]
        vj = jnp.concatenate([vt_ref[0, j], ones], axis=0)
        return kj, vj

    def run(work):
        st = [None] * len(work)
        for i in range(min(FOX_AHEAD, len(work))):
            st[i] = scores(work[i][0], work[i][2])
        for i, (kj, vj, c, mask) in enumerate(work):
            if i + FOX_AHEAD < len(work):
                st[i + FOX_AHEAD] = scores(work[i + FOX_AHEAD][0], work[i + FOX_AHEAD][2])
            update(st[i], vj, c, mask)
            st[i] = None

    def body(jj, carry):
        work = []
        for u in range(FOX_UNROLL):
            kj, vj = load(jj * FOX_UNROLL + u)
            work += [(kj, vj, c, None) for c in range(nc)]
        run(work)
        return carry

    lax.fori_loop(0, qi * (ratio // FOX_UNROLL), body, 0)
    work = []
    for d in range(ratio):
        kj, vj = load(qi * ratio + d)
        for c in range(nc):
            if d * tk > (c + 1) * qc - 1:
                continue
            if (d + 1) * tk - 1 <= c * qc:
                work.append((kj, vj, c, None))
            else:
                kpos = d * tk + lax.broadcasted_iota(jnp.int32, (tk, qc), 0)
                qpos = c * qc + lax.broadcasted_iota(jnp.int32, (tk, qc), 1)
                work.append((kj, vj, c, kpos <= qpos))
    run(work)
    for c in range(nc):
        acc = acc_refs[c][...]
        o_ref[0, :, c * qc:(c + 1) * qc] = (acc[:HEAD_DIM] / acc[HEAD_DIM:HEAD_DIM + 1]).astype(BF16)


def _fox_attn(q, k, vt):
    B, S, _ = q.shape
    nt, tk = vt.shape[1], vt.shape[3]
    tq = FOX_TQ
    assert tq % (tk * FOX_UNROLL) == 0 and tq % FOX_QCHUNK == 0
    return pl.pallas_call(
        _fox_attn_kernel,
        grid=(B, A_HEADS, S // tq),
        in_specs=[
            pl.BlockSpec((1, tq, LANES), lambda b, h, i: (b, i, h)),
            pl.BlockSpec((1, S, LANES), lambda b, h, i: (b, 0, h)),
            pl.BlockSpec((1, nt, HEAD_DIM, tk), lambda b, h, i: (b, 0, h, 0)),
        ],
        out_specs=pl.BlockSpec((1, HEAD_DIM, tq), lambda b, h, i: (b, h, i)),
        out_shape=jax.ShapeDtypeStruct((B, A_HEADS * HEAD_DIM, S), BF16),
        scratch_shapes=([pltpu.VMEM((HEAD_DIM + 16, FOX_QCHUNK), F32)] * (tq // FOX_QCHUNK)
                        + [pltpu.VMEM((1, FOX_QCHUNK), F32)] * (tq // FOX_QCHUNK)),
        compiler_params=_params("arbitrary", "arbitrary", "arbitrary"),
        name="fox_attn",
    )(q, k, vt)


def _route(xn, wrh_ref, wrl_ref, br_ref):
    xh = xn.astype(BF16)
    xl = (xn - xh.astype(F32)).astype(BF16)
    wh = wrh_ref[...]
    logits = _dot(xh, wh) + _dot(xl, wh) + _dot(xh, wrl_ref[...]) + br_ref[...]
    lane = lax.broadcasted_iota(jnp.int32, (1, LANES), 1)
    lanef = lane.astype(F32)
    far = float(LANES)

    gl = jnp.where(lane < N_GROUPS, logits, NEG)
    gm = jnp.max(gl, axis=-1, keepdims=True)
    g_val = 1.0 / jnp.sum(jnp.exp(gl - gm), axis=-1, keepdims=True)
    g_idx = jnp.min(jnp.where(gl == gm, lanef, far), axis=-1, keepdims=True)

    lo = N_GROUPS + EXPERTS_PER_GROUP * g_idx
    el = jnp.where((lanef >= lo) & (lanef < lo + EXPERTS_PER_GROUP), logits, NEG)
    em1 = jnp.max(el, axis=-1, keepdims=True)
    ez = jnp.sum(jnp.exp(el - em1), axis=-1, keepdims=True)
    i1 = jnp.min(jnp.where(el == em1, lanef, far), axis=-1, keepdims=True)
    el2 = jnp.where(lanef == i1, NEG, el)
    em2 = jnp.max(el2, axis=-1, keepdims=True)
    i2 = jnp.min(jnp.where(el2 == em2, lanef, far), axis=-1, keepdims=True)
    p1 = 1.0 / ez
    p2 = jnp.exp(em2 - em1) / ez
    den = p1 + p2
    gate1 = g_val * (p1 / den)
    gate2 = g_val * (p2 / den)
    return jnp.where(lane == 0, i1 - N_GROUPS,
                     jnp.where(lane == 1, i2 - N_GROUPS,
                               jnp.where(lane == 2, gate1,
                                         jnp.where(lane == 3, gate2, 0.0))))


def _router_weights(w_group, b_group, w_expert, b_expert):
    w = jnp.pad(jnp.concatenate([w_group, w_expert], axis=1),
                ((0, 0), (0, LANES - N_GROUPS - N_EXPERTS)))
    b = jnp.pad(jnp.concatenate([b_group, b_expert]), (0, LANES - N_GROUPS - N_EXPERTS))
    wh = w.astype(BF16)
    wl = (w - wh.astype(F32)).astype(BF16)
    return wh, wl, b.reshape(1, LANES)


def _ffn_in(h, fg_ref, wrh_ref, wrl_ref, br_ref, h_ref, xn_ref, route_ref):
    h_ref[0] = h
    xn = _rms(h, fg_ref[...])
    _rows_to_slabs(xn_ref.at[0], xn)
    route_ref[0] = _route(xn, wrh_ref, wrl_ref, br_ref)


def _fox_out_kernel(ot_ref, x_ref, wo_ref, fg_ref, wrh_ref, wrl_ref, br_ref,
                    h_ref, xn_ref, route_ref):
    h = x_ref[0] + _dot_tn(ot_ref[0], wo_ref[...])
    _ffn_in(h, fg_ref, wrh_ref, wrl_ref, br_ref, h_ref, xn_ref, route_ref)


def _token_out_specs(B, S, tm):
    specs = [
        pl.BlockSpec((1, tm, D_MODEL), lambda b, t: (b, t, 0)),
        pl.BlockSpec((1, tm, SLAB, LANES), lambda b, t: (b, t, 0, 0)),
        pl.BlockSpec((1, tm, LANES), lambda b, t: (b, t, 0)),
    ]
    shapes = [
        jax.ShapeDtypeStruct((B, S, D_MODEL), F32),
        jax.ShapeDtypeStruct((B, S, SLAB, LANES), F32),
        jax.ShapeDtypeStruct((B, S, LANES), F32),
    ]
    return specs, shapes


def _fox_out(ot, x, w_out, ffn_g, router):
    B, S, _ = x.shape
    tm = TOK_TILE
    wrh, wrl, br = router
    const = lambda b, t: (0, 0)
    out_specs, out_shape = _token_out_specs(B, S, tm)
    return pl.pallas_call(
        _fox_out_kernel,
        grid=(B, S // tm),
        in_specs=[
            pl.BlockSpec((1, A_HEADS * HEAD_DIM, tm), lambda b, t: (b, 0, t)),
            pl.BlockSpec((1, tm, D_MODEL), lambda b, t: (b, t, 0)),
            pl.BlockSpec((A_HEADS * HEAD_DIM, D_MODEL), const),
            pl.BlockSpec((1, D_MODEL), const),
            pl.BlockSpec((D_MODEL, LANES), const),
            pl.BlockSpec((D_MODEL, LANES), const),
            pl.BlockSpec((1, LANES), const),
        ],
        out_specs=out_specs,
        out_shape=out_shape,
        compiler_params=_params("arbitrary", "arbitrary"),
        name="fox_out",
    )(ot, x, w_out.astype(BF16), ffn_g.reshape(1, D_MODEL), wrh, wrl, br)


MOE_TILE = 256


def _moe_plan(route, n_tok):
    tm = MOE_TILE
    n_pairs = TOP_K * n_tok
    n_tiles = n_pairs // tm + N_EXPERTS
    pair_e = route[:, :TOP_K].astype(jnp.int32).T.reshape(-1)
    order = jnp.argsort(pair_e, stable=True).astype(jnp.int32)
    counts = jnp.sum(pair_e[:, None] == jnp.arange(N_EXPERTS)[None, :], axis=0).astype(jnp.int32)
    tiles_per = (counts + tm - 1) // tm
    tile_end = jnp.cumsum(tiles_per)
    n_used = tile_end[-1]
    tid = jnp.arange(n_tiles, dtype=jnp.int32)
    tid_c = jnp.minimum(tid, n_used - 1)
    tile_e = jnp.minimum(jnp.sum(tile_end[None, :] <= tid_c[:, None], axis=1), N_EXPERTS - 1).astype(jnp.int32)
    within = (tid_c - (tile_end - tiles_per)[tile_e]) * tm
    valid = jnp.where(tid < n_used, jnp.clip(counts[tile_e] - within, 0, tm), 0).astype(jnp.int32)
    src = (jnp.cumsum(counts) - counts)[tile_e][:, None] + within[:, None] + jnp.arange(tm, dtype=jnp.int32)[None, :]
    live = jnp.arange(tm, dtype=jnp.int32)[None, :] < valid[:, None]
    pair = order[jnp.clip(src, 0, n_pairs - 1)]
    dump = n_pairs + (tid % 2)[:, None] * tm + jnp.arange(tm, dtype=jnp.int32)[None, :]
    tok = jnp.where(live, pair % n_tok, 0).astype(jnp.int32)
    dst = jnp.where(live, pair, dump).astype(jnp.int32)
    return tile_e, valid, tok, dst


def _moe_kernel(te_ref, nv_ref, tok_hbm, dst_hbm, xn_hbm, wg_ref, wu_ref, wd_ref, out_hbm,
                tok_s, dst_s, xbuf, ybuf, isem, gsem, ssem):
    del te_ref
    groups = xbuf.shape[1]
    tm = groups * 8
    i = pl.program_id(0)
    n_tiles = pl.num_programs(0)
    slot = lax.rem(i, 2)
    other = 1 - slot

    def idx_copies(t, s):
        rows = pl.ds(s * tm, tm)
        return (pltpu.make_async_copy(tok_hbm.at[t], tok_s.at[rows], isem.at[0, s]),
                pltpu.make_async_copy(dst_hbm.at[t], dst_s.at[rows], isem.at[1, s]))

    def gather_start(s):
        def body(g, c):
            base = s * tm + g * 8
            for j in range(8):
                pltpu.make_async_copy(xn_hbm.at[tok_s[base + j]], xbuf.at[s, g, :, j, :], gsem.at[s]).start()
            return c
        lax.fori_loop(0, groups, body, 0)

    def scatter_start(s):
        def body(g, c):
            base = s * tm + g * 8
            for j in range(8):
                pltpu.make_async_copy(ybuf.at[s, g, :, j, :], out_hbm.at[dst_s[base + j]], ssem.at[s]).start()
            return c
        lax.fori_loop(0, groups, body, 0)

    def wait_rows(buf, sem, s):
        pltpu.make_async_copy(buf.at[s], buf.at[s], sem.at[s]).wait()

    @pl.when(i == 0)
    def _():
        for cp in idx_copies(0, 0):
            cp.start()
        for cp in idx_copies(0, 0):
            cp.wait()
        gather_start(0)
        for cp in idx_copies(1, 1):
            cp.start()

    @pl.when(i + 1 < n_tiles)
    def _():
        for cp in idx_copies(i + 1, other):
            cp.wait()
        gather_start(other)

    wait_rows(xbuf, gsem, slot)

    @pl.when(i >= 2)
    def _():
        wait_rows(ybuf, ssem, slot)

    @pl.when(nv_ref[i] > 0)
    def _():
        xb = jnp.concatenate([xbuf[slot, :, c].reshape(tm, LANES) for c in range(SLAB)], axis=1).astype(BF16)
        hid = jax.nn.silu(_dot(xb, wg_ref[0])) * _dot(xb, wu_ref[0])
        y = _dot(hid.astype(BF16), wd_ref[0])
        for c in range(SLAB):
            ybuf[slot, :, c] = y[:, c * LANES:(c + 1) * LANES].reshape(groups, 8, LANES)

    scatter_start(slot)

    @pl.when(i + 2 < n_tiles)
    def _():
        for cp in idx_copies(i + 2, slot):
            cp.start()

    @pl.when(i == n_tiles - 1)
    def _():
        wait_rows(ybuf, ssem, other)
        wait_rows(ybuf, ssem, slot)


def _moe(xn, route, w_gate, w_up, w_down):
    B, S = xn.shape[:2]
    n_tok = B * S
    tm = MOE_TILE
    tile_e, valid, tok, dst = _moe_plan(route.reshape(n_tok, LANES), n_tok)
    n_tiles = tile_e.shape[0]
    grid_spec = pltpu.PrefetchScalarGridSpec(
        num_scalar_prefetch=2,
        grid=(n_tiles,),
        in_specs=[
            pl.BlockSpec(memory_space=pl.ANY),
            pl.BlockSpec(memory_space=pl.ANY),
            pl.BlockSpec(memory_space=pl.ANY),
            pl.BlockSpec((1, D_MODEL, D_EXPERT), lambda i, te, nv: (te[i], 0, 0)),
            pl.BlockSpec((1, D_MODEL, D_EXPERT), lambda i, te, nv: (te[i], 0, 0)),
            pl.BlockSpec((1, D_EXPERT, D_MODEL), lambda i, te, nv: (te[i], 0, 0)),
        ],
        out_specs=pl.BlockSpec(memory_space=pl.ANY),
        scratch_shapes=[
            pltpu.SMEM((2 * tm,), jnp.int32),
            pltpu.SMEM((2 * tm,), jnp.int32),
            pltpu.VMEM((2, tm // 8, SLAB, 8, LANES), F32),
            pltpu.VMEM((2, tm // 8, SLAB, 8, LANES), F32),
            pltpu.SemaphoreType.DMA((2, 2)),
            pltpu.SemaphoreType.DMA((2,)),
            pltpu.SemaphoreType.DMA((2,)),
        ],
    )
    return pl.pallas_call(
        _moe_kernel,
        grid_spec=grid_spec,
        out_shape=jax.ShapeDtypeStruct((TOP_K * n_tok + 2 * tm, SLAB, LANES), F32),
        compiler_params=_params("arbitrary"),
        name="moe_experts",
    )(tile_e, valid, tok, dst, xn.reshape(n_tok, SLAB, LANES), w_gate.astype(BF16), w_up.astype(BF16),
      w_down.astype(BF16))


def _moe_combine(h, y0_ref, y1_ref, route):
    lane = lax.broadcasted_iota(jnp.int32, (1, LANES), 1)
    g1 = jnp.sum(jnp.where(lane == 2, route, 0.0), axis=-1, keepdims=True)
    g2 = jnp.sum(jnp.where(lane == 3, route, 0.0), axis=-1, keepdims=True)
    return h + (g1 * _rows_from_slabs(y0_ref) + g2 * _rows_from_slabs(y1_ref))


def _moe_out_specs(B, S, tm):
    per_b = S // tm
    return [pl.BlockSpec((tm, SLAB, LANES), lambda b, t, k=k: (k * B * per_b + b * per_b + t, 0, 0))
            for k in range(TOP_K)]


def _head_rms_pairs(p, gain):
    lane = lax.broadcasted_iota(jnp.int32, (1, LANES), 1)
    first = lane < HEAD_DIM
    outs = []
    for j in range(p.shape[1] // LANES):
        s = p[:, j * LANES:(j + 1) * LANES]
        sq = s * s
        s_all = jnp.sum(sq, axis=-1, keepdims=True)
        s_lo = jnp.sum(jnp.where(first, sq, 0.0), axis=-1, keepdims=True)
        r = jnp.where(first, lax.rsqrt(s_lo * (1.0 / HEAD_DIM) + EPS),
                      lax.rsqrt((s_all - s_lo) * (1.0 / HEAD_DIM) + EPS))
        outs.append((s * r * gain).astype(BF16))
    return outs


def _residue_perm(tm, d):
    i = jnp.arange(tm)
    src = (i % (tm // d)) * d + i // (tm // d)
    return (src[:, None] == jnp.arange(tm)[None, :]).astype(BF16)


def _dil_in_kernel(h_ref, y0_ref, y1_ref, route_ref, kvg_ref, bg_ref, wk_ref, wv_ref, wq_ref, kgain_ref,
                   qgain_ref, p1_ref, p2_ref, h2_ref, *qkv_refs):
    tm = h_ref.shape[1]
    h2 = _moe_combine(h_ref[0], y0_ref, y1_ref, route_ref[0])
    h2_ref[0] = h2
    xkv = _rms(h2, kvg_ref[...]).astype(BF16)
    ks = _head_rms_pairs(_dot(xkv, wk_ref[...]), kgain_ref[...])
    v = _dot(xkv, wv_ref[...]).astype(BF16)
    qs = _head_rms_pairs(_dot(_rms(h2, bg_ref[...]).astype(BF16), wq_ref[...]), qgain_ref[...])
    per = B_OUT_WIDTH // LANES
    for g, d in enumerate(B_DILATIONS):
        cat = jnp.concatenate(qs[g * per:(g + 1) * per] + ks[g * per:(g + 1) * per]
                              + [v[:, g * B_OUT_WIDTH:(g + 1) * B_OUT_WIDTH]], axis=1)
        if d > 1:
            cat = _dot((p1_ref if g == 1 else p2_ref)[...], cat).astype(BF16)
        rows = tm // d
        for t in range(3):
            ref = qkv_refs[3 * g + t]
            for r in range(d):
                ref[0, r] = cat[r * rows:(r + 1) * rows, t * B_OUT_WIDTH:(t + 1) * B_OUT_WIDTH]


def _dil_in(h, y, route, kv_norm, b_norm, kv_w, w_q, k_gain, q_gain):
    B, S, _ = h.shape
    tm = TOK_TILE
    const = lambda b, t: (0, 0)
    tok = lambda w: pl.BlockSpec((1, tm, w), lambda b, t: (b, t, 0))
    qkv_specs, qkv_shapes = [], []
    for d in B_DILATIONS:
        qkv_specs += [pl.BlockSpec((1, d, tm // d, B_OUT_WIDTH), lambda b, t: (b, 0, t, 0))] * 3
        qkv_shapes += [jax.ShapeDtypeStruct((B, d, S // d, B_OUT_WIDTH), BF16)] * 3
    outs = pl.pallas_call(
        _dil_in_kernel,
        grid=(B, S // tm),
        in_specs=[
            tok(D_MODEL),
            *_moe_out_specs(B, S, tm),
            tok(LANES),
            pl.BlockSpec((1, D_MODEL), const),
            pl.BlockSpec((1, D_MODEL), const),
            pl.BlockSpec((D_MODEL, B_WIDTH), const),
            pl.BlockSpec((D_MODEL, B_WIDTH), const),
            pl.BlockSpec((D_MODEL, B_WIDTH), const),
            pl.BlockSpec((1, LANES), const),
            pl.BlockSpec((1, LANES), const),
            pl.BlockSpec((tm, tm), const),
            pl.BlockSpec((tm, tm), const),
        ],
        out_specs=[tok(D_MODEL)] + qkv_specs,
        out_shape=[jax.ShapeDtypeStruct((B, S, D_MODEL), F32)] + qkv_shapes,
        compiler_params=_params("arbitrary", "arbitrary"),
        name="dil_in",
    )(h, y, y, route, kv_norm.reshape(1, D_MODEL), b_norm.reshape(1, D_MODEL),
      kv_w[:, :B_WIDTH].astype(BF16), kv_w[:, B_WIDTH:].astype(BF16), w_q.astype(BF16),
      jnp.tile(k_gain, 2).reshape(1, LANES), jnp.tile(q_gain * (SCALE * LOG2E), 2).reshape(1, LANES),
      _residue_perm(tm, B_DILATIONS[1]), _residue_perm(tm, B_DILATIONS[2]))
    return outs[0], [outs[1 + 3 * g:4 + 3 * g] for g in range(B_GROUPS)]


def _t5_bucket(dist):
    max_exact = NUM_BUCKETS // 2
    d_f = jnp.maximum(dist, max_exact).astype(F32)
    large = max_exact + (jnp.log(d_f / max_exact) / math.log(MAX_DISTANCE / max_exact)
                         * (NUM_BUCKETS - max_exact)).astype(jnp.int32)
    large = jnp.minimum(large, NUM_BUCKETS - 1)
    return jnp.where(dist < max_exact, dist, large)


def _branch_bias(rel_bias, g, d):
    a = jnp.arange(BLOCK)[:, None]
    b = jnp.arange(2 * BLOCK)[None, :]
    n = BLOCK + a - b
    band = (n >= 0) & (n <= B_WINDOWS[g] // d)
    onehot = (_t5_bucket(jnp.maximum(n, 0) * d)[..., None] == jnp.arange(NUM_BUCKETS)).astype(F32)
    table = rel_bias[:, g * B_HEADS_PER_GROUP:(g + 1) * B_HEADS_PER_GROUP].astype(F32)
    bias = jnp.einsum("abk,kh->hab", onehot, table, precision=lax.Precision.HIGHEST) * LOG2E
    return jnp.where(band[None], bias, NEG)


DIL_SUB = 4
DIL_AHEAD = 3


def _dil_attn_kernel(q_ref, kp_ref, kc_ref, vp_ref, vc_ref, bias_ref, o_ref, lse_ref):
    n = pl.program_id(2)
    lane = lax.broadcasted_iota(jnp.int32, (1, LANES), 1)
    first = lane < HEAD_DIM
    col = lax.broadcasted_iota(jnp.int32, (1, 2 * BLOCK), 1)
    dead = (n == 0) & (col < BLOCK)
    pairs = B_HEADS_PER_GROUP // 2

    def band(cur_ref, prev_ref, i, j):
        sl = slice(j * LANES, (j + 1) * LANES)
        if i == 0:
            return jnp.concatenate([prev_ref[0, 0, :, sl], cur_ref[0, 0, :BLOCK, sl]], axis=0)
        return cur_ref[0, 0, (i - 1) * BLOCK:(i + 1) * BLOCK, sl]

    def scores(i, j, hh):
        q = pltpu.bitcast(q_ref[0, 0, i * BLOCK:(i + 1) * BLOCK, j * LANES:(j + 1) * LANES], jnp.int32)
        qm = pltpu.bitcast(jnp.where(first if hh == 0 else jnp.logical_not(first), q, 0), BF16)
        return _dot_nt(qm, band(kc_ref, kp_ref, i, j))

    items = [(i, j, hh) for i in range(q_ref.shape[2] // BLOCK) for j in range(pairs) for hh in range(2)]
    st, sb, mx, pr, dn, outs = {}, {}, {}, {}, {}, {}
    state = {"m_blk": None, "den_blk": None}

    def stage_max(t):
        i, j, hh = t
        bias = bias_ref[2 * j + hh]
        sb[t] = st.pop(t) + (jnp.where(dead, NEG, bias) if i == 0 else bias)
        mx[t] = jnp.max(sb[t], axis=-1, keepdims=True)

    def stage_exp(t):
        pr[t] = jnp.exp2(sb.pop(t) - mx[t])
        dn[t] = jnp.sum(pr[t], axis=-1, keepdims=True)

    def stage_out(t):
        i, j, hh = t
        m, den = mx.pop(t), dn.pop(t)
        outs[hh] = _dot(pr.pop(t).astype(BF16), band(vc_ref, vp_ref, i, j)) * (1.0 / den)
        if j == 0 and hh == 0:
            state["m_blk"] = jnp.zeros((BLOCK, LANES), F32)
            state["den_blk"] = jnp.ones((BLOCK, LANES), F32)
        state["m_blk"] = jnp.where(lane == 2 * j + hh, m, state["m_blk"])
        state["den_blk"] = jnp.where(lane == 2 * j + hh, den, state["den_blk"])
        if hh == 1:
            o_ref[0, 0, i * BLOCK:(i + 1) * BLOCK, j * LANES:(j + 1) * LANES] = (
                jnp.where(first, outs[0], outs[1]).astype(BF16))
            if j == pairs - 1:
                lse = jnp.where(lane < B_HEADS_PER_GROUP, state["m_blk"] + jnp.log2(state["den_blk"]), 0.0)
                hi, mid, lo = _split3(lse)
                lse_ref[0, 0, i * BLOCK:(i + 1) * BLOCK, :] = (
                    hi + pltpu.roll(mid, 8, 1) + pltpu.roll(lo, 16, 1)).astype(BF16)

    n_items = len(items)
    for step in range(n_items + DIL_AHEAD + 2):
        if step < n_items:
            st[items[step]] = scores(*items[step])
        if 0 <= step - DIL_AHEAD < n_items:
            stage_max(items[step - DIL_AHEAD])
        if 0 <= step - DIL_AHEAD - 1 < n_items:
            stage_exp(items[step - DIL_AHEAD - 1])
        if 0 <= step - DIL_AHEAD - 2 < n_items:
            stage_out(items[step - DIL_AHEAD - 2])


def _dil_attn(q, k, v, bias, g):
    B, d, L, w = q.shape
    sub = min(DIL_SUB, L // BLOCK)
    rows = sub * BLOCK
    cur = pl.BlockSpec((1, 1, rows, w), lambda b, r, n: (b, r, n, 0))
    prev = pl.BlockSpec((1, 1, BLOCK, w), lambda b, r, n: (b, r, jnp.maximum(n * sub - 1, 0), 0))
    return pl.pallas_call(
        _dil_attn_kernel,
        grid=(B, d, L // rows),
        in_specs=[cur, prev, cur, prev, cur,
                  pl.BlockSpec((B_HEADS_PER_GROUP, BLOCK, 2 * BLOCK), lambda b, r, n: (0, 0, 0))],
        out_specs=[cur, pl.BlockSpec((1, 1, rows, LANES), lambda b, r, n: (b, r, n, 0))],
        out_shape=[
            jax.ShapeDtypeStruct((B, d, L, w), BF16),
            jax.ShapeDtypeStruct((B, d, L, LANES), BF16),
        ],
        compiler_params=_params("arbitrary", "arbitrary", "arbitrary"),
        name=f"dil_attn_{g}",
    )(q, k, k, v, v, bias)


def _dil_out_kernel(o0_ref, o1_ref, o2_ref, l0_ref, l1_ref, l2_ref, h_ref, wo_ref, fg_ref,
                    wrh_ref, wrl_ref, br_ref, p1t_ref, p2t_ref, h_out_ref, xn_ref, route_ref):
    tm = h_ref.shape[1]

    def natural(ref, pt_ref):
        x = ref[0].reshape(tm, ref.shape[3])
        return x.astype(F32) if pt_ref is None else _dot(pt_ref[...], x)

    def lse_of(ref, pt_ref):
        x = natural(ref, pt_ref)
        return x + pltpu.roll(x, LANES - 8, 1) + pltpu.roll(x, LANES - 16, 1)

    o0, o1, o2 = natural(o0_ref, None), natural(o1_ref, p1t_ref), natural(o2_ref, p2t_ref)
    l0, l1, l2 = lse_of(l0_ref, None), lse_of(l1_ref, p1t_ref), lse_of(l2_ref, p2t_ref)
    m = jnp.maximum(jnp.maximum(l0, l1), l2)
    e0, e1, e2 = jnp.exp2(l0 - m), jnp.exp2(l1 - m), jnp.exp2(l2 - m)
    den = e0 + e1 + e2
    row = lax.broadcasted_iota(jnp.int32, (LANES, B_OUT_WIDTH), 0)
    col = lax.broadcasted_iota(jnp.int32, (LANES, B_OUT_WIDTH), 1)
    spread = (jnp.right_shift(col, 6) == row).astype(BF16)

    def widen(a):
        hi = a.astype(BF16)
        lo = (a - hi.astype(F32)).astype(BF16)
        return _dot(hi, spread) + _dot(lo, spread)

    merged = widen(e0 / den) * o0 + widen(e1 / den) * o1 + widen(e2 / den) * o2
    h = h_ref[0] + _dot(merged.astype(BF16), wo_ref[...])
    _ffn_in(h, fg_ref, wrh_ref, wrl_ref, br_ref, h_out_ref, xn_ref, route_ref)


def _dil_out(os_, lses, h, w_out, ffn_g, router):
    B, S, _ = h.shape
    tm = TOK_TILE
    wrh, wrl, br = router
    const = lambda b, t: (0, 0)
    tok = lambda w: pl.BlockSpec((1, tm, w), lambda b, t: (b, t, 0))
    res = lambda w: [pl.BlockSpec((1, d, tm // d, w), lambda b, t: (b, 0, t, 0)) for d in B_DILATIONS]
    out_specs, out_shape = _token_out_specs(B, S, tm)
    return pl.pallas_call(
        _dil_out_kernel,
        grid=(B, S // tm),
        in_specs=res(B_OUT_WIDTH) + res(LANES) + [
            tok(D_MODEL),
            pl.BlockSpec((B_OUT_WIDTH, D_MODEL), const),
            pl.BlockSpec((1, D_MODEL), const),
            pl.BlockSpec((D_MODEL, LANES), const),
            pl.BlockSpec((D_MODEL, LANES), const),
            pl.BlockSpec((1, LANES), const),
            pl.BlockSpec((tm, tm), const),
            pl.BlockSpec((tm, tm), const),
        ],
        out_specs=out_specs,
        out_shape=out_shape,
        compiler_params=_params("arbitrary", "arbitrary"),
        name="dil_out",
    )(*os_, *lses, h, w_out.astype(BF16), ffn_g.reshape(1, D_MODEL), wrh, wrl, br,
      _residue_perm(tm, B_DILATIONS[1]).T, _residue_perm(tm, B_DILATIONS[2]).T)


def _final_kernel(h_ref, y0_ref, y1_ref, route_ref, o_ref):
    o_ref[0] = _moe_combine(h_ref[0], y0_ref, y1_ref, route_ref[0])


def _final(h, y, route):
    B, S, _ = h.shape
    tm = TOK_TILE
    tok = lambda w: pl.BlockSpec((1, tm, w), lambda b, t: (b, t, 0))
    return pl.pallas_call(
        _final_kernel,
        grid=(B, S // tm),
        in_specs=[tok(D_MODEL), *_moe_out_specs(B, S, tm), tok(LANES)],
        out_specs=tok(D_MODEL),
        out_shape=jax.ShapeDtypeStruct((B, S, D_MODEL), F32),
        compiler_params=_params("arbitrary", "arbitrary"),
        name="moe_final",
    )(h, y, y, route)


def kernel(x, a_norm, a_w_in, a_b_f, a_q_gain, a_k_gain, a_w_out, kv_norm, kv_w, kv_k_gain, rel_bias, b_norm, b_w_q, b_q_gain, b_w_out, ffn_norm, moe_w_group, moe_b_group, moe_w_expert, moe_b_expert, moe_w_gate, moe_w_up, moe_w_down):
    routers = [_router_weights(moe_w_group[l], moe_b_group[l], moe_w_expert[l], moe_b_expert[l])
               for l in range(2)]
    q, k, vt = _fox_in(x, a_norm[0], a_w_in[0], a_b_f[0], a_q_gain[0], a_k_gain[0])
    ot = _fox_attn(q, k, vt)
    h1, xn1, route1 = _fox_out(ot, x, a_w_out[0], ffn_norm[0], routers[0])
    y1 = _moe(xn1, route1, moe_w_gate[0], moe_w_up[0], moe_w_down[0])
    h2, qkv = _dil_in(h1, y1, route1, kv_norm, b_norm[0], kv_w, b_w_q[0], kv_k_gain, b_q_gain[0])
    outs, lses = [], []
    for g, d in enumerate(B_DILATIONS):
        o, lse = _dil_attn(*qkv[g], _branch_bias(rel_bias, g, d), g)
        outs.append(o)
        lses.append(lse)
    h3, xn3, route3 = _dil_out(outs, lses, h2, b_w_out[0], ffn_norm[1], routers[1])
    y3 = _moe(xn3, route3, moe_w_gate[1], moe_w_up[1], moe_w_down[1])
    return _final(h3, y3, route3)
```

```python
import functools
import math

import jax
import jax.numpy as jnp
from jax import lax
from jax.experimental import pallas as pl
from jax.experimental.pallas import tpu as pltpu

F32 = jnp.float32
BF16 = jnp.bfloat16

D_MODEL = 1024
HEAD_DIM = 64
A_HEADS = 16
B_GROUPS = 3
B_HEADS_PER_GROUP = 8
B_HEADS = 24
B_WIDTH = B_HEADS * HEAD_DIM
B_OUT_WIDTH = B_HEADS_PER_GROUP * HEAD_DIM
B_WINDOWS = (128, 512, 2048)
B_DILATIONS = (1, 4, 16)
BLOCK = 128
NUM_BUCKETS = 32
MAX_DISTANCE = 2048
N_GROUPS = 4
EXPERTS_PER_GROUP = 4
N_EXPERTS = 16
TOP_K = 2
D_EXPERT = 512
EPS = 1e-6
NEG = -1e30
SCALE = HEAD_DIM ** -0.5
LOG2E = 1.4426950408889634

LANES = 128
TOK_TILE = 512
VMEM_LIMIT = 56 * 1024 * 1024


def _params(*sem):
    return pltpu.CompilerParams(dimension_semantics=sem, vmem_limit_bytes=VMEM_LIMIT)


def _rms(x, g):
    return x * lax.rsqrt(jnp.mean(x * x, axis=-1, keepdims=True) + EPS) * g


def _split3(x):
    hi = x.astype(BF16).astype(F32)
    r = x - hi
    mid = r.astype(BF16).astype(F32)
    return hi, mid, r - mid


SLAB = D_MODEL // LANES


def _rows_from_slabs(ref):
    return jnp.concatenate([ref[:, c, :] for c in range(SLAB)], axis=1)


def _rows_to_slabs(ref, x):
    for c in range(SLAB):
        ref[:, c, :] = x[:, c * LANES:(c + 1) * LANES]


def _dot(a, b):
    return jnp.dot(a, b, preferred_element_type=F32)


def _dot_nt(a, b):
    return lax.dot_general(a, b, (((1,), (1,)), ((), ())), preferred_element_type=F32)


def _dot_tn(a, b):
    return lax.dot_general(a, b, (((0,), (0,)), ((), ())), preferred_element_type=F32)


HSUM_COLS = 256


def _head_sum_matrix():
    i = jnp.arange(HSUM_COLS) // HEAD_DIM
    return (i[:, None] == i[None, :]).astype(BF16)


def _head_rms(p, hsum_ref):
    outs = []
    for j in range(p.shape[1] // HSUM_COLS):
        s = p[:, j * HSUM_COLS:(j + 1) * HSUM_COLS]
        ss = _dot((s * s).astype(BF16), hsum_ref[...])
        outs.append(s * lax.rsqrt(ss * (1.0 / HEAD_DIM) + EPS))
    return outs[0] if len(outs) == 1 else jnp.concatenate(outs, axis=1)


FOX_IN_COLS = 256
FOX_IN_AHEAD = 2


def _fox_in_kernel(x_ref, g_ref, wqk_ref, wvt_ref, wf_ref, bf_ref, qg_ref, kg_ref, sel_ref, hsum_ref,
                   q_ref, k_ref, vt_ref, carry_ref):
    tm = x_ref.shape[1]
    aw = A_HEADS * HEAD_DIM

    @pl.when(pl.program_id(1) == 0)
    def _():
        carry_ref[...] = jnp.zeros_like(carry_ref)

    xb = _rms(x_ref[0], g_ref[...]).astype(BF16)
    lane = lax.broadcasted_iota(jnp.int32, (1, LANES), 1)
    z = _dot(xb, wf_ref[...]) + bf_ref[...]
    lf = jnp.minimum(z, 0.0) - jnp.log1p(jnp.exp(-jnp.abs(z)))
    lf = jnp.where(lane < A_HEADS, lf, 0.0)
    row = lax.broadcasted_iota(jnp.int32, (tm, tm), 0)
    col = lax.broadcasted_iota(jnp.int32, (tm, tm), 1)
    tri = (col <= row).astype(BF16)
    hi, mid, lo = _split3(lf)
    cum = (_dot(tri, hi.astype(BF16)) + _dot(tri, mid.astype(BF16))
           + _dot(tri, lo.astype(BF16))) + carry_ref[...]
    carry_ref[...] = cum[tm - 1:tm, :]
    c_hi, c_mid, c_lo = _split3(cum * (-LOG2E))
    cc = (c_hi + pltpu.roll(c_mid, A_HEADS, 1) + pltpu.roll(c_lo, 2 * A_HEADS, 1)).astype(BF16)

    head_lane = lane < HEAD_DIM
    ones_vec = jnp.where((lane >= HEAD_DIM) & (lane < HEAD_DIM + 3), 1.0, 0.0)
    per = FOX_IN_COLS // HEAD_DIM
    n_tiles = 2 * aw // FOX_IN_COLS

    def project(n):
        return _dot(xb, wqk_ref[:, n * FOX_IN_COLS:(n + 1) * FOX_IN_COLS])

    def finish(n, p):
        is_k = n * FOX_IN_COLS >= aw
        h0 = (n * FOX_IN_COLS - (aw if is_k else 0)) // HEAD_DIM
        if is_k:
            aug = _dot(cc, sel_ref[:, h0 * LANES:(h0 + per) * LANES])
        pn = _head_rms(p, hsum_ref)
        for i in range(per):
            s = pn[:, (i // 2) * LANES:(i // 2 + 1) * LANES]
            if i % 2:
                s = pltpu.roll(s, HEAD_DIM, 1)
            dst = slice((h0 + i) * LANES, (h0 + i + 1) * LANES)
            if is_k:
                k_ref[0, :, dst] = jnp.where(head_lane, s * kg_ref[...], aug[:, i * LANES:(i + 1) * LANES]).astype(BF16)
            else:
                q_ref[0, :, dst] = jnp.where(head_lane, s * qg_ref[...], ones_vec).astype(BF16)

    pending = {n: project(n) for n in range(min(FOX_IN_AHEAD, n_tiles))}
    for n in range(n_tiles):
        if n + FOX_IN_AHEAD < n_tiles:
            pending[n + FOX_IN_AHEAD] = project(n + FOX_IN_AHEAD)
        elif n + FOX_IN_AHEAD == n_tiles:
            vt_ref[0, 0] = _dot_nt(wvt_ref[...], xb).astype(BF16)
        finish(n, pending.pop(n))


def _fox_in(x, a_norm, w_in, b_f, q_gain, k_gain):
    B, S, _ = x.shape
    tm = TOK_TILE
    nt = S // tm
    aw = A_HEADS * HEAD_DIM
    wqk = w_in[:, :2 * aw].astype(BF16)
    wvt = w_in[:, 2 * aw:3 * aw].T.astype(BF16)
    wf = jnp.pad(w_in[:, 3 * aw:], ((0, 0), (0, LANES - A_HEADS))).astype(BF16)
    bf = jnp.pad(b_f, (0, LANES - A_HEADS)).reshape(1, LANES)
    qg = jnp.pad(q_gain * (SCALE * LOG2E), (0, LANES - HEAD_DIM)).reshape(1, LANES)
    kg = jnp.pad(k_gain, (0, LANES - HEAD_DIM)).reshape(1, LANES)
    src = jnp.arange(LANES)
    dst = (src % A_HEADS) * LANES + HEAD_DIM + src // A_HEADS
    sel = ((dst[:, None] == jnp.arange(A_HEADS * LANES)[None, :]) & (src[:, None] < 3 * A_HEADS)).astype(BF16)
    const = lambda b, t: (0, 0)
    return pl.pallas_call(
        _fox_in_kernel,
        grid=(B, nt),
        in_specs=[
            pl.BlockSpec((1, tm, D_MODEL), lambda b, t: (b, t, 0)),
            pl.BlockSpec((1, D_MODEL), const),
            pl.BlockSpec((D_MODEL, 2 * aw), const),
            pl.BlockSpec((aw, D_MODEL), const),
            pl.BlockSpec((D_MODEL, LANES), const),
            pl.BlockSpec((1, LANES), const),
            pl.BlockSpec((1, LANES), const),
            pl.BlockSpec((1, LANES), const),
            pl.BlockSpec((LANES, A_HEADS * LANES), const),
            pl.BlockSpec((HSUM_COLS, HSUM_COLS), const),
        ],
        out_specs=[
            pl.BlockSpec((1, tm, A_HEADS * LANES), lambda b, t: (b, t, 0)),
            pl.BlockSpec((1, tm, A_HEADS * LANES), lambda b, t: (b, t, 0)),
            pl.BlockSpec((1, 1, aw, tm), lambda b, t: (b, t, 0, 0)),
        ],
        out_shape=[
            jax.ShapeDtypeStruct((B, S, A_HEADS * LANES), BF16),
            jax.ShapeDtypeStruct((B, S, A_HEADS * LANES), BF16),
            jax.ShapeDtypeStruct((B, nt, aw, tm), BF16),
        ],
        scratch_shapes=[pltpu.VMEM((1, LANES), F32)],
        compiler_params=_params("arbitrary", "arbitrary"),
        name="fox_in",
    )(x, a_norm.reshape(1, D_MODEL), wqk, wvt, wf, bf, qg, kg, sel, _head_sum_matrix())


FOX_TQ = 2048
FOX_QCHUNK = 256
FOX_UNROLL = 4
FOX_AHEAD = 4


def _fox_attn_kernel(q_ref, k_ref, vt_ref, o_ref, *scratch):
    tq = q_ref.shape[1]
    tk = vt_ref.shape[3]
    qc = FOX_QCHUNK
    nc = tq // qc
    ratio = tq // tk
    qi = pl.program_id(2)
    acc_refs, m_refs = scratch[:nc], scratch[nc:]
    for c in range(nc):
        m_refs[c][...] = jnp.full_like(m_refs[c], NEG)
        acc_refs[c][...] = jnp.zeros_like(acc_refs[c])
    ones = jnp.ones((16, tk), BF16)

    def scores(kj, c):
        return _dot_nt(kj, q_ref[0, c * qc:(c + 1) * qc, :])

    def update(st, vj, c, mask):
        if mask is not None:
            st = jnp.where(mask, st, NEG)
        m_prev = m_refs[c][...]
        m_new = jnp.maximum(m_prev, jnp.max(st, axis=0, keepdims=True))
        p = jnp.exp2((st - m_new).astype(BF16))
        alpha = jnp.exp2(m_prev - m_new)
        acc_refs[c][...] = alpha * acc_refs[c][...] + _dot(vj, p)
        m_refs[c][...] = m_new

    def load(j):
        kj = k_ref[0, pl.ds(pl.multiple_of(j * tk, tk), tk), :]
        vj = jnp.concatenate([vt_ref[0, j], ones], axis=0)
        return kj, vj

    def run(work):
        st = [None] * len(work)
        for i in range(min(FOX_AHEAD, len(work))):
            st[i] = scores(work[i][0], work[i][2])
        for i, (kj, vj, c, mask) in enumerate(work):
            if i + FOX_AHEAD < len(work):
                st[i + FOX_AHEAD] = scores(work[i + FOX_AHEAD][0], work[i + FOX_AHEAD][2])
            update(st[i], vj, c, mask)
            st[i] = None

    def body(jj, carry):
        work = []
        for u in range(FOX_UNROLL):
            kj, vj = load(jj * FOX_UNROLL + u)
            work += [(kj, vj, c, None) for c in range(nc)]
        run(work)
        return carry

    lax.fori_loop(0, qi * (ratio // FOX_UNROLL), body, 0)
    work = []
    for d in range(ratio):
        kj, vj = load(qi * ratio + d)
        for c in range(nc):
            if d * tk > (c + 1) * qc - 1:
                continue
            if (d + 1) * tk - 1 <= c * qc:
                work.append((kj, vj, c, None))
            else:
                kpos = d * tk + lax.broadcasted_iota(jnp.int32, (tk, qc), 0)
                qpos = c * qc + lax.broadcasted_iota(jnp.int32, (tk, qc), 1)
                work.append((kj, vj, c, kpos <= qpos))
    run(work)
    for c in range(nc):
        acc = acc_refs[c][...]
        o_ref[0, :, c * qc:(c + 1) * qc] = (acc[:HEAD_DIM] / acc[HEAD_DIM:HEAD_DIM + 1]).astype(BF16)


def _fox_attn(q, k, vt):
    B, S, _ = q.shape
    nt, tk = vt.shape[1], vt.shape[3]
    tq = FOX_TQ
    assert tq % (tk * FOX_UNROLL) == 0 and tq % FOX_QCHUNK == 0
    return pl.pallas_call(
        _fox_attn_kernel,
        grid=(B, A_HEADS, S // tq),
        in_specs=[
            pl.BlockSpec((1, tq, LANES), lambda b, h, i: (b, i, h)),
            pl.BlockSpec((1, S, LANES), lambda b, h, i: (b, 0, h)),
            pl.BlockSpec((1, nt, HEAD_DIM, tk), lambda b, h, i: (b, 0, h, 0)),
        ],
        out_specs=pl.BlockSpec((1, HEAD_DIM, tq), lambda b, h, i: (b, h, i)),
        out_shape=jax.ShapeDtypeStruct((B, A_HEADS * HEAD_DIM, S), BF16),
        scratch_shapes=([pltpu.VMEM((HEAD_DIM + 16, FOX_QCHUNK), F32)] * (tq // FOX_QCHUNK)
                        + [pltpu.VMEM((1, FOX_QCHUNK), F32)] * (tq // FOX_QCHUNK)),
        compiler_params=_params("arbitrary", "arbitrary", "arbitrary"),
        name="fox_attn",
    )(q, k, vt)


def _route(xn, wrh_ref, wrl_ref, br_ref):
    xh = xn.astype(BF16)
    xl = (xn - xh.astype(F32)).astype(BF16)
    wh = wrh_ref[...]
    logits = _dot(xh, wh) + _dot(xl, wh) + _dot(xh, wrl_ref[...]) + br_ref[...]
    lane = lax.broadcasted_iota(jnp.int32, (1, LANES), 1)
    lanef = lane.astype(F32)
    far = float(LANES)

    gl = jnp.where(lane < N_GROUPS, logits, NEG)
    gm = jnp.max(gl, axis=-1, keepdims=True)
    g_val = 1.0 / jnp.sum(jnp.exp(gl - gm), axis=-1, keepdims=True)
    g_idx = jnp.min(jnp.where(gl == gm, lanef, far), axis=-1, keepdims=True)

    lo = N_GROUPS + EXPERTS_PER_GROUP * g_idx
    el = jnp.where((lanef >= lo) & (lanef < lo + EXPERTS_PER_GROUP), logits, NEG)
    em1 = jnp.max(el, axis=-1, keepdims=True)
    ez = jnp.sum(jnp.exp(el - em1), axis=-1, keepdims=True)
    i1 = jnp.min(jnp.where(el == em1, lanef, far), axis=-1, keepdims=True)
    el2 = jnp.where(lanef == i1, NEG, el)
    em2 = jnp.max(el2, axis=-1, keepdims=True)
    i2 = jnp.min(jnp.where(el2 == em2, lanef, far), axis=-1, keepdims=True)
    p1 = 1.0 / ez
    p2 = jnp.exp(em2 - em1) / ez
    den = p1 + p2
    gate1 = g_val * (p1 / den)
    gate2 = g_val * (p2 / den)
    return jnp.where(lane == 0, i1 - N_GROUPS,
                     jnp.where(lane == 1, i2 - N_GROUPS,
                               jnp.where(lane == 2, gate1,
                                         jnp.where(lane == 3, gate2, 0.0))))


def _router_weights(w_group, b_group, w_expert, b_expert):
    w = jnp.pad(jnp.concatenate([w_group, w_expert], axis=1),
                ((0, 0), (0, LANES - N_GROUPS - N_EXPERTS)))
    b = jnp.pad(jnp.concatenate([b_group, b_expert]), (0, LANES - N_GROUPS - N_EXPERTS))
    wh = w.astype(BF16)
    wl = (w - wh.astype(F32)).astype(BF16)
    return wh, wl, b.reshape(1, LANES)


def _ffn_in(h, fg_ref, wrh_ref, wrl_ref, br_ref, h_ref, xn_ref, route_ref):
    h_ref[0] = h
    xn = _rms(h, fg_ref[...])
    _rows_to_slabs(xn_ref.at[0], xn)
    route_ref[0] = _route(xn, wrh_ref, wrl_ref, br_ref)


def _fox_out_kernel(ot_ref, x_ref, wo_ref, fg_ref, wrh_ref, wrl_ref, br_ref,
                    h_ref, xn_ref, route_ref):
    h = x_ref[0] + _dot_tn(ot_ref[0], wo_ref[...])
    _ffn_in(h, fg_ref, wrh_ref, wrl_ref, br_ref, h_ref, xn_ref, route_ref)


def _token_out_specs(B, S, tm):
    specs = [
        pl.BlockSpec((1, tm, D_MODEL), lambda b, t: (b, t, 0)),
        pl.BlockSpec((1, tm, SLAB, LANES), lambda b, t: (b, t, 0, 0)),
        pl.BlockSpec((1, tm, LANES), lambda b, t: (b, t, 0)),
    ]
    shapes = [
        jax.ShapeDtypeStruct((B, S, D_MODEL), F32),
        jax.ShapeDtypeStruct((B, S, SLAB, LANES), F32),
        jax.ShapeDtypeStruct((B, S, LANES), F32),
    ]
    return specs, shapes


def _fox_out(ot, x, w_out, ffn_g, router):
    B, S, _ = x.shape
    tm = TOK_TILE
    wrh, wrl, br = router
    const = lambda b, t: (0, 0)
    out_specs, out_shape = _token_out_specs(B, S, tm)
    return pl.pallas_call(
        _fox_out_kernel,
        grid=(B, S // tm),
        in_specs=[
            pl.BlockSpec((1, A_HEADS * HEAD_DIM, tm), lambda b, t: (b, 0, t)),
            pl.BlockSpec((1, tm, D_MODEL), lambda b, t: (b, t, 0)),
            pl.BlockSpec((A_HEADS * HEAD_DIM, D_MODEL), const),
            pl.BlockSpec((1, D_MODEL), const),
            pl.BlockSpec((D_MODEL, LANES), const),
            pl.BlockSpec((D_MODEL, LANES), const),
            pl.BlockSpec((1, LANES), const),
        ],
        out_specs=out_specs,
        out_shape=out_shape,
        compiler_params=_params("arbitrary", "arbitrary"),
        name="fox_out",
    )(ot, x, w_out.astype(BF16), ffn_g.reshape(1, D_MODEL), wrh, wrl, br)


MOE_TILE = 256


def _moe_plan(route, n_tok):
    tm = MOE_TILE
    n_pairs = TOP_K * n_tok
    n_tiles = n_pairs // tm + N_EXPERTS
    pair_e = route[:, :TOP_K].astype(jnp.int32).T.reshape(-1)
    order = jnp.argsort(pair_e, stable=True).astype(jnp.int32)
    counts = jnp.sum(pair_e[:, None] == jnp.arange(N_EXPERTS)[None, :], axis=0).astype(jnp.int32)
    tiles_per = (counts + tm - 1) // tm
    tile_end = jnp.cumsum(tiles_per)
    n_used = tile_end[-1]
    tid = jnp.arange(n_tiles, dtype=jnp.int32)
    tid_c = jnp.minimum(tid, n_used - 1)
    tile_e = jnp.minimum(jnp.sum(tile_end[None, :] <= tid_c[:, None], axis=1), N_EXPERTS - 1).astype(jnp.int32)
    within = (tid_c - (tile_end - tiles_per)[tile_e]) * tm
    valid = jnp.where(tid < n_used, jnp.clip(counts[tile_e] - within, 0, tm), 0).astype(jnp.int32)
    src = (jnp.cumsum(counts) - counts)[tile_e][:, None] + within[:, None] + jnp.arange(tm, dtype=jnp.int32)[None, :]
    live = jnp.arange(tm, dtype=jnp.int32)[None, :] < valid[:, None]
    pair = order[jnp.clip(src, 0, n_pairs - 1)]
    dump = n_pairs + (tid % 2)[:, None] * tm + jnp.arange(tm, dtype=jnp.int32)[None, :]
    tok = jnp.where(live, pair % n_tok, 0).astype(jnp.int32)
    dst = jnp.where(live, pair, dump).astype(jnp.int32)
    return tile_e, valid, tok, dst


def _moe_kernel(te_ref, nv_ref, tok_hbm, dst_hbm, xn_hbm, wg_ref, wu_ref, wd_ref, out_hbm,
                tok_s, dst_s, xbuf, ybuf, isem, gsem, ssem):
    del te_ref
    groups = xbuf.shape[1]
    tm = groups * 8
    i = pl.program_id(0)
    n_tiles = pl.num_programs(0)
    slot = lax.rem(i, 2)
    other = 1 - slot

    def idx_copies(t, s):
        rows = pl.ds(s * tm, tm)
        return (pltpu.make_async_copy(tok_hbm.at[t], tok_s.at[rows], isem.at[0, s]),
                pltpu.make_async_copy(dst_hbm.at[t], dst_s.at[rows], isem.at[1, s]))

    def gather_start(s):
        def body(g, c):
            base = s * tm + g * 8
            for j in range(8):
                pltpu.make_async_copy(xn_hbm.at[tok_s[base + j]], xbuf.at[s, g, :, j, :],
                                      gsem.at[s]).start(priority=j % 2)
            return c
        lax.fori_loop(0, groups, body, 0)

    def scatter_start(s):
        def body(g, c):
            base = s * tm + g * 8
            for j in range(8):
                pltpu.make_async_copy(ybuf.at[s, g, :, j, :], out_hbm.at[dst_s[base + j]],
                                      ssem.at[s]).start(priority=j % 2)
            return c
        lax.fori_loop(0, groups, body, 0)

    def wait_rows(buf, sem, s):
        pltpu.make_async_copy(buf.at[s], buf.at[s], sem.at[s]).wait()

    @pl.when(i == 0)
    def _():
        for cp in idx_copies(0, 0):
            cp.start()
        for cp in idx_copies(0, 0):
            cp.wait()
        gather_start(0)
        for cp in idx_copies(1, 1):
            cp.start()

    @pl.when(i + 1 < n_tiles)
    def _():
        for cp in idx_copies(i + 1, other):
            cp.wait()
        gather_start(other)

    wait_rows(xbuf, gsem, slot)

    @pl.when(i >= 2)
    def _():
        wait_rows(ybuf, ssem, slot)

    @pl.when(nv_ref[i] > 0)
    def _():
        xb = jnp.concatenate([xbuf[slot, :, c].reshape(tm, LANES) for c in range(SLAB)], axis=1).astype(BF16)
        hid = jax.nn.silu(_dot(xb, wg_ref[0])) * _dot(xb, wu_ref[0])
        y = _dot(hid.astype(BF16), wd_ref[0])
        for c in range(SLAB):
            ybuf[slot, :, c] = y[:, c * LANES:(c + 1) * LANES].reshape(groups, 8, LANES)

    scatter_start(slot)

    @pl.when(i + 2 < n_tiles)
    def _():
        for cp in idx_copies(i + 2, slot):
            cp.start()

    @pl.when(i == n_tiles - 1)
    def _():
        wait_rows(ybuf, ssem, other)
        wait_rows(ybuf, ssem, slot)


def _moe(xn, route, w_gate, w_up, w_down):
    B, S = xn.shape[:2]
    n_tok = B * S
    tm = MOE_TILE
    tile_e, valid, tok, dst = _moe_plan(route.reshape(n_tok, LANES), n_tok)
    n_tiles = tile_e.shape[0]
    grid_spec = pltpu.PrefetchScalarGridSpec(
        num_scalar_prefetch=2,
        grid=(n_tiles,),
        in_specs=[
            pl.BlockSpec(memory_space=pl.ANY),
            pl.BlockSpec(memory_space=pl.ANY),
            pl.BlockSpec(memory_space=pl.ANY),
            pl.BlockSpec((1, D_MODEL, D_EXPERT), lambda i, te, nv: (te[i], 0, 0)),
            pl.BlockSpec((1, D_MODEL, D_EXPERT), lambda i, te, nv: (te[i], 0, 0)),
            pl.BlockSpec((1, D_EXPERT, D_MODEL), lambda i, te, nv: (te[i], 0, 0)),
        ],
        out_specs=pl.BlockSpec(memory_space=pl.ANY),
        scratch_shapes=[
            pltpu.SMEM((2 * tm,), jnp.int32),
            pltpu.SMEM((2 * tm,), jnp.int32),
            pltpu.VMEM((2, tm // 8, SLAB, 8, LANES), F32),
            pltpu.VMEM((2, tm // 8, SLAB, 8, LANES), F32),
            pltpu.SemaphoreType.DMA((2, 2)),
            pltpu.SemaphoreType.DMA((2,)),
            pltpu.SemaphoreType.DMA((2,)),
        ],
    )
    return pl.pallas_call(
        _moe_kernel,
        grid_spec=grid_spec,
        out_shape=jax.ShapeDtypeStruct((TOP_K * n_tok + 2 * tm, SLAB, LANES), F32),
        compiler_params=_params("arbitrary"),
        name="moe_experts",
    )(tile_e, valid, tok, dst, xn.reshape(n_tok, SLAB, LANES), w_gate.astype(BF16), w_up.astype(BF16),
      w_down.astype(BF16))


def _moe_combine(h, y0_ref, y1_ref, route):
    lane = lax.broadcasted_iota(jnp.int32, (1, LANES), 1)
    g1 = jnp.sum(jnp.where(lane == 2, route, 0.0), axis=-1, keepdims=True)
    g2 = jnp.sum(jnp.where(lane == 3, route, 0.0), axis=-1, keepdims=True)
    return h + (g1 * _rows_from_slabs(y0_ref) + g2 * _rows_from_slabs(y1_ref))


def _moe_out_specs(B, S, tm):
    per_b = S // tm
    return [pl.BlockSpec((tm, SLAB, LANES), lambda b, t, k=k: (k * B * per_b + b * per_b + t, 0, 0))
            for k in range(TOP_K)]


def _residue_perm(tm, d):
    i = jnp.arange(tm)
    src = (i % (tm // d)) * d + i // (tm // d)
    return (src[:, None] == jnp.arange(tm)[None, :]).astype(BF16)


def _dil_in_kernel(h_ref, y0_ref, y1_ref, route_ref, kvg_ref, bg_ref, wk_ref, wv_ref, wq_ref, kgain_ref,
                   qgain_ref, p1_ref, p2_ref, hsum_ref, h2_ref, *qkv_refs):
    tm = h_ref.shape[1]
    w = B_OUT_WIDTH
    h2 = _moe_combine(h_ref[0], y0_ref, y1_ref, route_ref[0])
    h2_ref[0] = h2
    u = h2 * lax.rsqrt(jnp.mean(h2 * h2, axis=-1, keepdims=True) + EPS)
    xkv = (u * kvg_ref[...]).astype(BF16)
    xq = (u * bg_ref[...]).astype(BF16)

    def project(g):
        cols = slice(g * w, (g + 1) * w)
        return _dot(xq, wq_ref[:, cols]), _dot(xkv, wk_ref[:, cols]), _dot(xkv, wv_ref[:, cols])

    def finish(g, d, qkv):
        q, k, v = qkv
        cat = jnp.concatenate([(_head_rms(q, hsum_ref) * qgain_ref[...]).astype(BF16),
                               (_head_rms(k, hsum_ref) * kgain_ref[...]).astype(BF16),
                               v.astype(BF16)], axis=1)
        if d > 1:
            cat = _dot((p1_ref if g == 1 else p2_ref)[...], cat).astype(BF16)
        rows = tm // d
        for t in range(3):
            ref = qkv_refs[3 * g + t]
            for r in range(d):
                ref[0, r] = cat[r * rows:(r + 1) * rows, t * w:(t + 1) * w]

    nxt = project(0)
    for g, d in enumerate(B_DILATIONS):
        cur = nxt
        if g + 1 < B_GROUPS:
            nxt = project(g + 1)
        finish(g, d, cur)


def _dil_in(h, y, route, kv_norm, b_norm, kv_w, w_q, k_gain, q_gain):
    B, S, _ = h.shape
    tm = TOK_TILE
    const = lambda b, t: (0, 0)
    tok = lambda w: pl.BlockSpec((1, tm, w), lambda b, t: (b, t, 0))
    qkv_specs, qkv_shapes = [], []
    for d in B_DILATIONS:
        qkv_specs += [pl.BlockSpec((1, d, tm // d, B_OUT_WIDTH), lambda b, t: (b, 0, t, 0))] * 3
        qkv_shapes += [jax.ShapeDtypeStruct((B, d, S // d, B_OUT_WIDTH), BF16)] * 3
    outs = pl.pallas_call(
        _dil_in_kernel,
        grid=(B, S // tm),
        in_specs=[
            tok(D_MODEL),
            *_moe_out_specs(B, S, tm),
            tok(LANES),
            pl.BlockSpec((1, D_MODEL), const),
            pl.BlockSpec((1, D_MODEL), const),
            pl.BlockSpec((D_MODEL, B_WIDTH), const),
            pl.BlockSpec((D_MODEL, B_WIDTH), const),
            pl.BlockSpec((D_MODEL, B_WIDTH), const),
            pl.BlockSpec((1, B_OUT_WIDTH), const),
            pl.BlockSpec((1, B_OUT_WIDTH), const),
            pl.BlockSpec((tm, tm), const),
            pl.BlockSpec((tm, tm), const),
            pl.BlockSpec((HSUM_COLS, HSUM_COLS), const),
        ],
        out_specs=[tok(D_MODEL)] + qkv_specs,
        out_shape=[jax.ShapeDtypeStruct((B, S, D_MODEL), F32)] + qkv_shapes,
        compiler_params=_params("arbitrary", "arbitrary"),
        name="dil_in",
    )(h, y, y, route, kv_norm.reshape(1, D_MODEL), b_norm.reshape(1, D_MODEL),
      kv_w[:, :B_WIDTH].astype(BF16), kv_w[:, B_WIDTH:].astype(BF16), w_q.astype(BF16),
      jnp.tile(k_gain, B_HEADS_PER_GROUP).reshape(1, B_OUT_WIDTH),
      jnp.tile(q_gain * (SCALE * LOG2E), B_HEADS_PER_GROUP).reshape(1, B_OUT_WIDTH),
      _residue_perm(tm, B_DILATIONS[1]), _residue_perm(tm, B_DILATIONS[2]), _head_sum_matrix())
    return outs[0], [outs[1 + 3 * g:4 + 3 * g] for g in range(B_GROUPS)]


def _t5_bucket(dist):
    max_exact = NUM_BUCKETS // 2
    d_f = jnp.maximum(dist, max_exact).astype(F32)
    large = max_exact + (jnp.log(d_f / max_exact) / math.log(MAX_DISTANCE / max_exact)
                         * (NUM_BUCKETS - max_exact)).astype(jnp.int32)
    large = jnp.minimum(large, NUM_BUCKETS - 1)
    return jnp.where(dist < max_exact, dist, large)


def _branch_bias(rel_bias, g, d):
    a = jnp.arange(BLOCK)[:, None]
    b = jnp.arange(2 * BLOCK)[None, :]
    n = BLOCK + a - b
    band = (n >= 0) & (n <= B_WINDOWS[g] // d)
    onehot = (_t5_bucket(jnp.maximum(n, 0) * d)[..., None] == jnp.arange(NUM_BUCKETS)).astype(F32)
    table = rel_bias[:, g * B_HEADS_PER_GROUP:(g + 1) * B_HEADS_PER_GROUP].astype(F32)
    bias = jnp.einsum("abk,kh->hab", onehot, table, precision=lax.Precision.HIGHEST) * LOG2E
    return jnp.where(band[None], bias, NEG)


DIL_SUB = 4
DIL_AHEAD = 3


def _dil_attn_kernel(q_ref, kp_ref, kc_ref, vp_ref, vc_ref, bias_ref, o_ref, lse_ref):
    n = pl.program_id(2)
    lane = lax.broadcasted_iota(jnp.int32, (1, LANES), 1)
    first = lane < HEAD_DIM
    col = lax.broadcasted_iota(jnp.int32, (1, 2 * BLOCK), 1)
    dead = (n == 0) & (col < BLOCK)
    pairs = B_HEADS_PER_GROUP // 2

    def band(cur_ref, prev_ref, i, j):
        sl = slice(j * LANES, (j + 1) * LANES)
        if i == 0:
            return jnp.concatenate([prev_ref[0, 0, :, sl], cur_ref[0, 0, :BLOCK, sl]], axis=0)
        return cur_ref[0, 0, (i - 1) * BLOCK:(i + 1) * BLOCK, sl]

    def scores(i, j, hh):
        q = pltpu.bitcast(q_ref[0, 0, i * BLOCK:(i + 1) * BLOCK, j * LANES:(j + 1) * LANES], jnp.int32)
        qm = pltpu.bitcast(jnp.where(first if hh == 0 else jnp.logical_not(first), q, 0), BF16)
        return _dot_nt(qm, band(kc_ref, kp_ref, i, j))

    items = [(i, j, hh) for i in range(q_ref.shape[2] // BLOCK) for j in range(pairs) for hh in range(2)]
    st, sb, mx, pr, dn, outs = {}, {}, {}, {}, {}, {}
    state = {"m_blk": None, "den_blk": None}

    def stage_max(t):
        i, j, hh = t
        bias = bias_ref[2 * j + hh]
        sb[t] = st.pop(t) + (jnp.where(dead, NEG, bias) if i == 0 else bias)
        mx[t] = jnp.max(sb[t], axis=-1, keepdims=True)

    def stage_exp(t):
        pr[t] = jnp.exp2(sb.pop(t) - mx[t])
        dn[t] = jnp.sum(pr[t], axis=-1, keepdims=True)

    def stage_out(t):
        i, j, hh = t
        m, den = mx.pop(t), dn.pop(t)
        outs[hh] = _dot(pr.pop(t).astype(BF16), band(vc_ref, vp_ref, i, j)) * (1.0 / den)
        if j == 0 and hh == 0:
            state["m_blk"] = jnp.zeros((BLOCK, LANES), F32)
            state["den_blk"] = jnp.ones((BLOCK, LANES), F32)
        state["m_blk"] = jnp.where(lane == 2 * j + hh, m, state["m_blk"])
        state["den_blk"] = jnp.where(lane == 2 * j + hh, den, state["den_blk"])
        if hh == 1:
            o_ref[0, 0, i * BLOCK:(i + 1) * BLOCK, j * LANES:(j + 1) * LANES] = (
                jnp.where(first, outs[0], outs[1]).astype(BF16))
            if j == pairs - 1:
                lse = jnp.where(lane < B_HEADS_PER_GROUP, state["m_blk"] + jnp.log2(state["den_blk"]), 0.0)
                hi, mid, lo = _split3(lse)
                lse_ref[0, 0, i * BLOCK:(i + 1) * BLOCK, :] = (
                    hi + pltpu.roll(mid, 8, 1) + pltpu.roll(lo, 16, 1)).astype(BF16)

    n_items = len(items)
    for step in range(n_items + DIL_AHEAD + 2):
        if step < n_items:
            st[items[step]] = scores(*items[step])
        if 0 <= step - DIL_AHEAD < n_items:
            stage_max(items[step - DIL_AHEAD])
        if 0 <= step - DIL_AHEAD - 1 < n_items:
            stage_exp(items[step - DIL_AHEAD - 1])
        if 0 <= step - DIL_AHEAD - 2 < n_items:
            stage_out(items[step - DIL_AHEAD - 2])


def _dil_attn(q, k, v, bias, g):
    B, d, L, w = q.shape
    sub = min(DIL_SUB, L // BLOCK)
    rows = sub * BLOCK
    cur = pl.BlockSpec((1, 1, rows, w), lambda b, r, n: (b, r, n, 0))
    prev = pl.BlockSpec((1, 1, BLOCK, w), lambda b, r, n: (b, r, jnp.maximum(n * sub - 1, 0), 0))
    return pl.pallas_call(
        _dil_attn_kernel,
        grid=(B, d, L // rows),
        in_specs=[cur, prev, cur, prev, cur,
                  pl.BlockSpec((B_HEADS_PER_GROUP, BLOCK, 2 * BLOCK), lambda b, r, n: (0, 0, 0))],
        out_specs=[cur, pl.BlockSpec((1, 1, rows, LANES), lambda b, r, n: (b, r, n, 0))],
        out_shape=[
            jax.ShapeDtypeStruct((B, d, L, w), BF16),
            jax.ShapeDtypeStruct((B, d, L, LANES), BF16),
        ],
        compiler_params=_params("arbitrary", "arbitrary", "arbitrary"),
        name=f"dil_attn_{g}",
    )(q, k, k, v, v, bias)


def _dil_out_kernel(o0_ref, o1_ref, o2_ref, l0_ref, l1_ref, l2_ref, h_ref, wo_ref, fg_ref,
                    wrh_ref, wrl_ref, br_ref, p1t_ref, p2t_ref, h_out_ref, xn_ref, route_ref):
    tm = h_ref.shape[1]

    def natural(ref, pt_ref):
        x = ref[0].reshape(tm, ref.shape[3])
        return x.astype(F32) if pt_ref is None else _dot(pt_ref[...], x)

    def lse_of(ref, pt_ref):
        x = natural(ref, pt_ref)
        return x + pltpu.roll(x, LANES - 8, 1) + pltpu.roll(x, LANES - 16, 1)

    o0, o1, o2 = natural(o0_ref, None), natural(o1_ref, p1t_ref), natural(o2_ref, p2t_ref)
    l0, l1, l2 = lse_of(l0_ref, None), lse_of(l1_ref, p1t_ref), lse_of(l2_ref, p2t_ref)
    m = jnp.maximum(jnp.maximum(l0, l1), l2)
    e0, e1, e2 = jnp.exp2(l0 - m), jnp.exp2(l1 - m), jnp.exp2(l2 - m)
    den = e0 + e1 + e2
    row = lax.broadcasted_iota(jnp.int32, (LANES, B_OUT_WIDTH), 0)
    col = lax.broadcasted_iota(jnp.int32, (LANES, B_OUT_WIDTH), 1)
    spread = (jnp.right_shift(col, 6) == row).astype(BF16)

    def widen(a):
        hi = a.astype(BF16)
        lo = (a - hi.astype(F32)).astype(BF16)
        return _dot(hi, spread) + _dot(lo, spread)

    merged = widen(e0 / den) * o0 + widen(e1 / den) * o1 + widen(e2 / den) * o2
    h = h_ref[0] + _dot(merged.astype(BF16), wo_ref[...])
    _ffn_in(h, fg_ref, wrh_ref, wrl_ref, br_ref, h_out_ref, xn_ref, route_ref)


def _dil_out(os_, lses, h, w_out, ffn_g, router):
    B, S, _ = h.shape
    tm = TOK_TILE
    wrh, wrl, br = router
    const = lambda b, t: (0, 0)
    tok = lambda w: pl.BlockSpec((1, tm, w), lambda b, t: (b, t, 0))
    res = lambda w: [pl.BlockSpec((1, d, tm // d, w), lambda b, t: (b, 0, t, 0)) for d in B_DILATIONS]
    out_specs, out_shape = _token_out_specs(B, S, tm)
    return pl.pallas_call(
        _dil_out_kernel,
        grid=(B, S // tm),
        in_specs=res(B_OUT_WIDTH) + res(LANES) + [
            tok(D_MODEL),
            pl.BlockSpec((B_OUT_WIDTH, D_MODEL), const),
            pl.BlockSpec((1, D_MODEL), const),
            pl.BlockSpec((D_MODEL, LANES), const),
            pl.BlockSpec((D_MODEL, LANES), const),
            pl.BlockSpec((1, LANES), const),
            pl.BlockSpec((tm, tm), const),
            pl.BlockSpec((tm, tm), const),
        ],
        out_specs=out_specs,
        out_shape=out_shape,
        compiler_params=_params("arbitrary", "arbitrary"),
        name="dil_out",
    )(*os_, *lses, h, w_out.astype(BF16), ffn_g.reshape(1, D_MODEL), wrh, wrl, br,
      _residue_perm(tm, B_DILATIONS[1]).T, _residue_perm(tm, B_DILATIONS[2]).T)


def _final_kernel(h_ref, y0_ref, y1_ref, route_ref, o_ref):
    o_ref[0] = _moe_combine(h_ref[0], y0_ref, y1_ref, route_ref[0])


def _final(h, y, route):
    B, S, _ = h.shape
    tm = TOK_TILE
    tok = lambda w: pl.BlockSpec((1, tm, w), lambda b, t: (b, t, 0))
    return pl.pallas_call(
        _final_kernel,
        grid=(B, S // tm),
        in_specs=[tok(D_MODEL), *_moe_out_specs(B, S, tm), tok(LANES)],
        out_specs=tok(D_MODEL),
        out_shape=jax.ShapeDtypeStruct((B, S, D_MODEL), F32),
        compiler_params=_params("arbitrary", "arbitrary"),
        name="moe_final",
    )(h, y, y, route)


def kernel(x, a_norm, a_w_in, a_b_f, a_q_gain, a_k_gain, a_w_out, kv_norm, kv_w, kv_k_gain, rel_bias, b_norm, b_w_q, b_q_gain, b_w_out, ffn_norm, moe_w_group, moe_b_group, moe_w_expert, moe_b_expert, moe_w_gate, moe_w_up, moe_w_down):
    routers = [_router_weights(moe_w_group[l], moe_b_group[l], moe_w_expert[l], moe_b_expert[l])
               for l in range(2)]
    q, k, vt = _fox_in(x, a_norm[0], a_w_in[0], a_b_f[0], a_q_gain[0], a_k_gain[0])
    ot = _fox_attn(q, k, vt)
    h1, xn1, route1 = _fox_out(ot, x, a_w_out[0], ffn_norm[0], routers[0])
    y1 = _moe(xn1, route1, moe_w_gate[0], moe_w_up[0], moe_w_down[0])
    h2, qkv = _dil_in(h1, y1, route1, kv_norm, b_norm[0], kv_w, b_w_q[0], kv_k_gain, b_q_gain[0])
    outs, lses = [], []
    for g, d in enumerate(B_DILATIONS):
        o, lse = _dil_attn(*qkv[g], _branch_bias(rel_bias, g, d), g)
        outs.append(o)
        lses.append(lse)
    h3, xn3, route3 = _dil_out(outs, lses, h2, b_w_out[0], ffn_norm[1], routers[1])
    y3 = _moe(xn3, route3, moe_w_gate[1], moe_w_up[1], moe_w_down[1])
    return _final(h3, y3, route3)
```

```python
import functools
import math

import jax
import jax.numpy as jnp
from jax import lax
from jax.experimental import pallas as pl
from jax.experimental.pallas import tpu as pltpu

F32 = jnp.float32
BF16 = jnp.bfloat16

D_MODEL = 1024
HEAD_DIM = 64
A_HEADS = 16
B_GROUPS = 3
B_HEADS_PER_GROUP = 8
B_HEADS = 24
B_WIDTH = B_HEADS * HEAD_DIM
B_OUT_WIDTH = B_HEADS_PER_GROUP * HEAD_DIM
B_WINDOWS = (128, 512, 2048)
B_DILATIONS = (1, 4, 16)
BLOCK = 128
NUM_BUCKETS = 32
MAX_DISTANCE = 2048
N_GROUPS = 4
EXPERTS_PER_GROUP = 4
N_EXPERTS = 16
TOP_K = 2
D_EXPERT = 512
EPS = 1e-6
NEG = -1e30
SCALE = HEAD_DIM ** -0.5
LOG2E = 1.4426950408889634

LANES = 128
TOK_TILE = 512
VMEM_LIMIT = 56 * 1024 * 1024


def _params(*sem):
    return pltpu.CompilerParams(dimension_semantics=sem, vmem_limit_bytes=VMEM_LIMIT)


def _rms(x, g):
    return x * lax.rsqrt(jnp.mean(x * x, axis=-1, keepdims=True) + EPS) * g


def _split3(x):
    hi = x.astype(BF16).astype(F32)
    r = x - hi
    mid = r.astype(BF16).astype(F32)
    return hi, mid, r - mid


SLAB = D_MODEL // LANES


def _rows_from_slabs(ref):
    return jnp.concatenate([ref[:, c, :] for c in range(SLAB)], axis=1)


def _rows_to_slabs(ref, x):
    for c in range(SLAB):
        ref[:, c, :] = x[:, c * LANES:(c + 1) * LANES]


def _dot(a, b):
    return jnp.dot(a, b, preferred_element_type=F32)


def _dot_nt(a, b):
    return lax.dot_general(a, b, (((1,), (1,)), ((), ())), preferred_element_type=F32)


def _dot_tn(a, b):
    return lax.dot_general(a, b, (((0,), (0,)), ((), ())), preferred_element_type=F32)


HSUM_COLS = 256


def _head_sum_matrix():
    i = jnp.arange(HSUM_COLS) // HEAD_DIM
    return (i[:, None] == i[None, :]).astype(BF16)


def _head_rms(p, hsum_ref):
    outs = []
    for j in range(p.shape[1] // HSUM_COLS):
        s = p[:, j * HSUM_COLS:(j + 1) * HSUM_COLS]
        ss = _dot((s * s).astype(BF16), hsum_ref[...])
        outs.append(s * lax.rsqrt(ss * (1.0 / HEAD_DIM) + EPS))
    return outs[0] if len(outs) == 1 else jnp.concatenate(outs, axis=1)


FOX_IN_COLS = 256
FOX_IN_AHEAD = 2


def _fox_in_kernel(x_ref, g_ref, wqk_ref, wvt_ref, wf_ref, bf_ref, qg_ref, kg_ref, sel_ref, hsum_ref,
                   q_ref, k_ref, vt_ref, carry_ref):
    tm = x_ref.shape[1]
    aw = A_HEADS * HEAD_DIM

    @pl.when(pl.program_id(1) == 0)
    def _():
        carry_ref[...] = jnp.zeros_like(carry_ref)

    xb = _rms(x_ref[0], g_ref[...]).astype(BF16)
    lane = lax.broadcasted_iota(jnp.int32, (1, LANES), 1)
    z = _dot(xb, wf_ref[...]) + bf_ref[...]
    lf = jnp.minimum(z, 0.0) - jnp.log1p(jnp.exp(-jnp.abs(z)))
    lf = jnp.where(lane < A_HEADS, lf, 0.0)
    row = lax.broadcasted_iota(jnp.int32, (tm, tm), 0)
    col = lax.broadcasted_iota(jnp.int32, (tm, tm), 1)
    tri = (col <= row).astype(BF16)
    hi, mid, lo = _split3(lf)
    cum = (_dot(tri, hi.astype(BF16)) + _dot(tri, mid.astype(BF16))
           + _dot(tri, lo.astype(BF16))) + carry_ref[...]
    carry_ref[...] = cum[tm - 1:tm, :]
    c_hi, c_mid, c_lo = _split3(cum * (-LOG2E))
    cc = (c_hi + pltpu.roll(c_mid, A_HEADS, 1) + pltpu.roll(c_lo, 2 * A_HEADS, 1)).astype(BF16)

    head_lane = lane < HEAD_DIM
    ones_vec = jnp.where((lane >= HEAD_DIM) & (lane < HEAD_DIM + 3), 1.0, 0.0)
    per = FOX_IN_COLS // HEAD_DIM
    n_tiles = 2 * aw // FOX_IN_COLS

    def project(n):
        return _dot(xb, wqk_ref[:, n * FOX_IN_COLS:(n + 1) * FOX_IN_COLS])

    def finish(n, p):
        is_k = n * FOX_IN_COLS >= aw
        h0 = (n * FOX_IN_COLS - (aw if is_k else 0)) // HEAD_DIM
        if is_k:
            aug = _dot(cc, sel_ref[:, h0 * LANES:(h0 + per) * LANES])
        pn = _head_rms(p, hsum_ref)
        for i in range(per):
            s = pn[:, (i // 2) * LANES:(i // 2 + 1) * LANES]
            if i % 2:
                s = pltpu.roll(s, HEAD_DIM, 1)
            dst = slice((h0 + i) * LANES, (h0 + i + 1) * LANES)
            if is_k:
                k_ref[0, :, dst] = jnp.where(head_lane, s * kg_ref[...], aug[:, i * LANES:(i + 1) * LANES]).astype(BF16)
            else:
                q_ref[0, :, dst] = jnp.where(head_lane, s * qg_ref[...], ones_vec).astype(BF16)

    pending = {n: project(n) for n in range(min(FOX_IN_AHEAD, n_tiles))}
    for n in range(n_tiles):
        if n + FOX_IN_AHEAD < n_tiles:
            pending[n + FOX_IN_AHEAD] = project(n + FOX_IN_AHEAD)
        elif n + FOX_IN_AHEAD == n_tiles:
            vt_ref[0, 0] = _dot_nt(wvt_ref[...], xb).astype(BF16)
        finish(n, pending.pop(n))


def _fox_in(x, a_norm, w_in, b_f, q_gain, k_gain):
    B, S, _ = x.shape
    tm = TOK_TILE
    nt = S // tm
    aw = A_HEADS * HEAD_DIM
    wqk = w_in[:, :2 * aw].astype(BF16)
    wvt = w_in[:, 2 * aw:3 * aw].T.astype(BF16)
    wf = jnp.pad(w_in[:, 3 * aw:], ((0, 0), (0, LANES - A_HEADS))).astype(BF16)
    bf = jnp.pad(b_f, (0, LANES - A_HEADS)).reshape(1, LANES)
    qg = jnp.pad(q_gain * (SCALE * LOG2E), (0, LANES - HEAD_DIM)).reshape(1, LANES)
    kg = jnp.pad(k_gain, (0, LANES - HEAD_DIM)).reshape(1, LANES)
    src = jnp.arange(LANES)
    dst = (src % A_HEADS) * LANES + HEAD_DIM + src // A_HEADS
    sel = ((dst[:, None] == jnp.arange(A_HEADS * LANES)[None, :]) & (src[:, None] < 3 * A_HEADS)).astype(BF16)
    const = lambda b, t: (0, 0)
    return pl.pallas_call(
        _fox_in_kernel,
        grid=(B, nt),
        in_specs=[
            pl.BlockSpec((1, tm, D_MODEL), lambda b, t: (b, t, 0)),
            pl.BlockSpec((1, D_MODEL), const),
            pl.BlockSpec((D_MODEL, 2 * aw), const),
            pl.BlockSpec((aw, D_MODEL), const),
            pl.BlockSpec((D_MODEL, LANES), const),
            pl.BlockSpec((1, LANES), const),
            pl.BlockSpec((1, LANES), const),
            pl.BlockSpec((1, LANES), const),
            pl.BlockSpec((LANES, A_HEADS * LANES), const),
            pl.BlockSpec((HSUM_COLS, HSUM_COLS), const),
        ],
        out_specs=[
            pl.BlockSpec((1, tm, A_HEADS * LANES), lambda b, t: (b, t, 0)),
            pl.BlockSpec((1, tm, A_HEADS * LANES), lambda b, t: (b, t, 0)),
            pl.BlockSpec((1, 1, aw, tm), lambda b, t: (b, t, 0, 0)),
        ],
        out_shape=[
            jax.ShapeDtypeStruct((B, S, A_HEADS * LANES), BF16),
            jax.ShapeDtypeStruct((B, S, A_HEADS * LANES), BF16),
            jax.ShapeDtypeStruct((B, nt, aw, tm), BF16),
        ],
        scratch_shapes=[pltpu.VMEM((1, LANES), F32)],
        compiler_params=_params("arbitrary", "arbitrary"),
        name="fox_in",
    )(x, a_norm.reshape(1, D_MODEL), wqk, wvt, wf, bf, qg, kg, sel, _head_sum_matrix())


FOX_TQ = 2048
FOX_QCHUNK = 256
FOX_UNROLL = 4
FOX_AHEAD = 4


def _fox_attn_kernel(q_ref, k_ref, vt_ref, o_ref, *scratch):
    tq = q_ref.shape[1]
    tk = vt_ref.shape[3]
    qc = FOX_QCHUNK
    nc = tq // qc
    ratio = tq // tk
    qi = pl.program_id(2)
    acc_refs, m_refs = scratch[:nc], scratch[nc:]
    for c in range(nc):
        m_refs[c][...] = jnp.full_like(m_refs[c], NEG)
        acc_refs[c][...] = jnp.zeros_like(acc_refs[c])
    ones = jnp.ones((16, tk), BF16)

    def scores(kj, c):
        return _dot_nt(kj, q_ref[0, c * qc:(c + 1) * qc, :])

    def update(st, vj, c, mask):
        if mask is not None:
            st = jnp.where(mask, st, NEG)
        m_prev = m_refs[c][...]
        m_new = jnp.maximum(m_prev, jnp.max(st, axis=0, keepdims=True))
        p = jnp.exp2((st - m_new).astype(BF16))
        alpha = jnp.exp2(m_prev - m_new)
        acc_refs[c][...] = alpha * acc_refs[c][...] + _dot(vj, p)
        m_refs[c][...] = m_new

    def load(j):
        kj = k_ref[0, pl.ds(pl.multiple_of(j * tk, tk), tk), :]
        vj = jnp.concatenate([vt_ref[0, j], ones], axis=0)
        return kj, vj

    def run(work):
        st = [None] * len(work)
        for i in range(min(FOX_AHEAD, len(work))):
            st[i] = scores(work[i][0], work[i][2])
        for i, (kj, vj, c, mask) in enumerate(work):
            if i + FOX_AHEAD < len(work):
                st[i + FOX_AHEAD] = scores(work[i + FOX_AHEAD][0], work[i + FOX_AHEAD][2])
            update(st[i], vj, c, mask)
            st[i] = None

    def body(jj, carry):
        work = []
        for u in range(FOX_UNROLL):
            kj, vj = load(jj * FOX_UNROLL + u)
            work += [(kj, vj, c, None) for c in range(nc)]
        run(work)
        return carry

    lax.fori_loop(0, qi * (ratio // FOX_UNROLL), body, 0)
    work = []
    for d in range(ratio):
        kj, vj = load(qi * ratio + d)
        for c in range(nc):
            if d * tk > (c + 1) * qc - 1:
                continue
            if (d + 1) * tk - 1 <= c * qc:
                work.append((kj, vj, c, None))
            else:
                kpos = d * tk + lax.broadcasted_iota(jnp.int32, (tk, qc), 0)
                qpos = c * qc + lax.broadcasted_iota(jnp.int32, (tk, qc), 1)
                work.append((kj, vj, c, kpos <= qpos))
    run(work)
    for c in range(nc):
        acc = acc_refs[c][...]
        o_ref[0, :, c * qc:(c + 1) * qc] = (acc[:HEAD_DIM] / acc[HEAD_DIM:HEAD_DIM + 1]).astype(BF16)


def _fox_attn(q, k, vt):
    B, S, _ = q.shape
    nt, tk = vt.shape[1], vt.shape[3]
    tq = FOX_TQ
    assert tq % (tk * FOX_UNROLL) == 0 and tq % FOX_QCHUNK == 0
    return pl.pallas_call(
        _fox_attn_kernel,
        grid=(B, A_HEADS, S // tq),
        in_specs=[
            pl.BlockSpec((1, tq, LANES), lambda b, h, i: (b, i, h)),
            pl.BlockSpec((1, S, LANES), lambda b, h, i: (b, 0, h)),
            pl.BlockSpec((1, nt, HEAD_DIM, tk), lambda b, h, i: (b, 0, h, 0)),
        ],
        out_specs=pl.BlockSpec((1, HEAD_DIM, tq), lambda b, h, i: (b, h, i)),
        out_shape=jax.ShapeDtypeStruct((B, A_HEADS * HEAD_DIM, S), BF16),
        scratch_shapes=([pltpu.VMEM((HEAD_DIM + 16, FOX_QCHUNK), F32)] * (tq // FOX_QCHUNK)
                        + [pltpu.VMEM((1, FOX_QCHUNK), F32)] * (tq // FOX_QCHUNK)),
        compiler_params=_params("arbitrary", "arbitrary", "arbitrary"),
        name="fox_attn",
    )(q, k, vt)


def _route(xn, wrh_ref, wrl_ref, br_ref):
    xh = xn.astype(BF16)
    xl = (xn - xh.astype(F32)).astype(BF16)
    wh = wrh_ref[...]
    logits = _dot(xh, wh) + _dot(xl, wh) + _dot(xh, wrl_ref[...]) + br_ref[...]
    lane = lax.broadcasted_iota(jnp.int32, (1, LANES), 1)
    lanef = lane.astype(F32)
    far = float(LANES)

    gl = jnp.where(lane < N_GROUPS, logits, NEG)
    gm = jnp.max(gl, axis=-1, keepdims=True)
    g_val = 1.0 / jnp.sum(jnp.exp(gl - gm), axis=-1, keepdims=True)
    g_idx = jnp.min(jnp.where(gl == gm, lanef, far), axis=-1, keepdims=True)

    lo = N_GROUPS + EXPERTS_PER_GROUP * g_idx
    el = jnp.where((lanef >= lo) & (lanef < lo + EXPERTS_PER_GROUP), logits, NEG)
    em1 = jnp.max(el, axis=-1, keepdims=True)
    ez = jnp.sum(jnp.exp(el - em1), axis=-1, keepdims=True)
    i1 = jnp.min(jnp.where(el == em1, lanef, far), axis=-1, keepdims=True)
    el2 = jnp.where(lanef == i1, NEG, el)
    em2 = jnp.max(el2, axis=-1, keepdims=True)
    i2 = jnp.min(jnp.where(el2 == em2, lanef, far), axis=-1, keepdims=True)
    p1 = 1.0 / ez
    p2 = jnp.exp(em2 - em1) / ez
    den = p1 + p2
    gate1 = g_val * (p1 / den)
    gate2 = g_val * (p2 / den)
    return jnp.where(lane == 0, i1 - N_GROUPS,
                     jnp.where(lane == 1, i2 - N_GROUPS,
                               jnp.where(lane == 2, gate1,
                                         jnp.where(lane == 3, gate2, 0.0))))


def _router_weights(w_group, b_group, w_expert, b_expert):
    w = jnp.pad(jnp.concatenate([w_group, w_expert], axis=1),
                ((0, 0), (0, LANES - N_GROUPS - N_EXPERTS)))
    b = jnp.pad(jnp.concatenate([b_group, b_expert]), (0, LANES - N_GROUPS - N_EXPERTS))
    wh = w.astype(BF16)
    wl = (w - wh.astype(F32)).astype(BF16)
    return wh, wl, b.reshape(1, LANES)


def _ffn_in(h, fg_ref, wrh_ref, wrl_ref, br_ref, h_ref, xn_ref, route_ref):
    h_ref[0] = h
    xn = _rms(h, fg_ref[...])
    _rows_to_slabs(xn_ref.at[0], xn)
    route_ref[0] = _route(xn, wrh_ref, wrl_ref, br_ref)


def _fox_out_kernel(ot_ref, x_ref, wo_ref, fg_ref, wrh_ref, wrl_ref, br_ref,
                    h_ref, xn_ref, route_ref):
    h = x_ref[0] + _dot_tn(ot_ref[0], wo_ref[...])
    _ffn_in(h, fg_ref, wrh_ref, wrl_ref, br_ref, h_ref, xn_ref, route_ref)


def _token_out_specs(B, S, tm):
    specs = [
        pl.BlockSpec((1, tm, D_MODEL), lambda b, t: (b, t, 0)),
        pl.BlockSpec((1, tm, SLAB, LANES), lambda b, t: (b, t, 0, 0)),
        pl.BlockSpec((1, tm, LANES), lambda b, t: (b, t, 0)),
    ]
    shapes = [
        jax.ShapeDtypeStruct((B, S, D_MODEL), F32),
        jax.ShapeDtypeStruct((B, S, SLAB, LANES), F32),
        jax.ShapeDtypeStruct((B, S, LANES), F32),
    ]
    return specs, shapes


def _fox_out(ot, x, w_out, ffn_g, router):
    B, S, _ = x.shape
    tm = TOK_TILE
    wrh, wrl, br = router
    const = lambda b, t: (0, 0)
    out_specs, out_shape = _token_out_specs(B, S, tm)
    return pl.pallas_call(
        _fox_out_kernel,
        grid=(B, S // tm),
        in_specs=[
            pl.BlockSpec((1, A_HEADS * HEAD_DIM, tm), lambda b, t: (b, 0, t)),
            pl.BlockSpec((1, tm, D_MODEL), lambda b, t: (b, t, 0)),
            pl.BlockSpec((A_HEADS * HEAD_DIM, D_MODEL), const),
            pl.BlockSpec((1, D_MODEL), const),
            pl.BlockSpec((D_MODEL, LANES), const),
            pl.BlockSpec((D_MODEL, LANES), const),
            pl.BlockSpec((1, LANES), const),
        ],
        out_specs=out_specs,
        out_shape=out_shape,
        compiler_params=_params("arbitrary", "arbitrary"),
        name="fox_out",
    )(ot, x, w_out.astype(BF16), ffn_g.reshape(1, D_MODEL), wrh, wrl, br)


MOE_TILE = 256
PAIRS_PER_GROUP = EXPERTS_PER_GROUP * (EXPERTS_PER_GROUP - 1) // 2
N_CLASSES = N_GROUPS * PAIRS_PER_GROUP
_PAIR_LO = [a for a in range(EXPERTS_PER_GROUP) for b in range(a + 1, EXPERTS_PER_GROUP)]
_PAIR_HI = [b for a in range(EXPERTS_PER_GROUP) for b in range(a + 1, EXPERTS_PER_GROUP)]


def _moe_plan(route, n_tok):
    tm = MOE_TILE
    n_tiles = n_tok // tm + N_CLASSES
    e1, e2 = route[:, 0].astype(jnp.int32), route[:, 1].astype(jnp.int32)
    swap = e2 < e1
    lo, hi = jnp.where(swap, e2, e1), jnp.where(swap, e1, e2)
    gates = jnp.stack([jnp.where(swap, route[:, 3], route[:, 2]),
                       jnp.where(swap, route[:, 2], route[:, 3])], axis=1)
    pair_of = jnp.zeros((EXPERTS_PER_GROUP, EXPERTS_PER_GROUP), jnp.int32).at[
        jnp.array(_PAIR_LO), jnp.array(_PAIR_HI)].set(jnp.arange(PAIRS_PER_GROUP, dtype=jnp.int32))
    cls = (lo // EXPERTS_PER_GROUP) * PAIRS_PER_GROUP + pair_of[lo % EXPERTS_PER_GROUP, hi % EXPERTS_PER_GROUP]
    order = jnp.argsort(cls, stable=True).astype(jnp.int32)
    counts = jnp.sum(cls[:, None] == jnp.arange(N_CLASSES)[None, :], axis=0).astype(jnp.int32)
    tiles_per = (counts + tm - 1) // tm
    tile_end = jnp.cumsum(tiles_per)
    n_used = tile_end[-1]
    tid = jnp.arange(n_tiles, dtype=jnp.int32)
    tid_c = jnp.minimum(tid, n_used - 1)
    tile_c = jnp.minimum(jnp.sum(tile_end[None, :] <= tid_c[:, None], axis=1), N_CLASSES - 1).astype(jnp.int32)
    base = (tile_c // PAIRS_PER_GROUP) * EXPERTS_PER_GROUP
    tile_a = (base + jnp.array(_PAIR_LO, jnp.int32)[tile_c % PAIRS_PER_GROUP]).astype(jnp.int32)
    tile_b = (base + jnp.array(_PAIR_HI, jnp.int32)[tile_c % PAIRS_PER_GROUP]).astype(jnp.int32)
    within = (tid_c - (tile_end - tiles_per)[tile_c]) * tm
    valid = jnp.where(tid < n_used, jnp.clip(counts[tile_c] - within, 0, tm), 0).astype(jnp.int32)
    src = (jnp.cumsum(counts) - counts)[tile_c][:, None] + within[:, None] + jnp.arange(tm, dtype=jnp.int32)[None, :]
    live = jnp.arange(tm, dtype=jnp.int32)[None, :] < valid[:, None]
    token = order[jnp.clip(src, 0, n_tok - 1)]
    dump = n_tok + (tid % 2)[:, None] * tm + jnp.arange(tm, dtype=jnp.int32)[None, :]
    tok = jnp.where(live, token, 0).astype(jnp.int32)
    dst = jnp.where(live, token, dump).astype(jnp.int32)
    row_gates = jnp.where(live[..., None], gates[token], 0.0).reshape(n_tiles * tm, TOP_K)
    return tile_a, tile_b, valid, tok, dst, row_gates


def _moe_kernel(ta_ref, tb_ref, nv_ref, tok_hbm, dst_hbm, xn_hbm, gate_ref, wga_ref, wua_ref, wda_ref,
                wgb_ref, wub_ref, wdb_ref, out_hbm, tok_s, dst_s, xbuf, ybuf, isem, gsem, ssem):
    del ta_ref, tb_ref
    groups = xbuf.shape[1]
    tm = groups * 8
    i = pl.program_id(0)
    n_tiles = pl.num_programs(0)
    slot = lax.rem(i, 2)
    other = 1 - slot

    def idx_copies(t, s):
        rows = pl.ds(s * tm, tm)
        return (pltpu.make_async_copy(tok_hbm.at[t], tok_s.at[rows], isem.at[0, s]),
                pltpu.make_async_copy(dst_hbm.at[t], dst_s.at[rows], isem.at[1, s]))

    def gather_start(s):
        def body(g, c):
            base = s * tm + g * 8
            for j in range(8):
                pltpu.make_async_copy(xn_hbm.at[tok_s[base + j]], xbuf.at[s, g, :, j, :],
                                      gsem.at[s]).start(priority=j % 2)
            return c
        lax.fori_loop(0, groups, body, 0)

    def scatter_start(s):
        def body(g, c):
            base = s * tm + g * 8
            for j in range(8):
                pltpu.make_async_copy(ybuf.at[s, g, :, j, :], out_hbm.at[dst_s[base + j]],
                                      ssem.at[s]).start(priority=j % 2)
            return c
        lax.fori_loop(0, groups, body, 0)

    def wait_rows(buf, sem, s):
        pltpu.make_async_copy(buf.at[s], buf.at[s], sem.at[s]).wait()

    @pl.when(i == 0)
    def _():
        for cp in idx_copies(0, 0):
            cp.start()
        for cp in idx_copies(0, 0):
            cp.wait()
        gather_start(0)
        for cp in idx_copies(1, 1):
            cp.start()

    @pl.when(i + 1 < n_tiles)
    def _():
        for cp in idx_copies(i + 1, other):
            cp.wait()
        gather_start(other)

    wait_rows(xbuf, gsem, slot)

    @pl.when(i >= 2)
    def _():
        wait_rows(ybuf, ssem, slot)

    @pl.when(nv_ref[i] > 0)
    def _():
        xb = jnp.concatenate([xbuf[slot, :, c].reshape(tm, LANES) for c in range(SLAB)], axis=1).astype(BF16)
        ga, ua = _dot(xb, wga_ref[0]), _dot(xb, wua_ref[0])
        gb, ub = _dot(xb, wgb_ref[0]), _dot(xb, wub_ref[0])
        ya = _dot((jax.nn.silu(ga) * ua).astype(BF16), wda_ref[0])
        yb = _dot((jax.nn.silu(gb) * ub).astype(BF16), wdb_ref[0])
        gates = gate_ref[...]
        y = gates[:, 0:1] * ya + gates[:, 1:2] * yb
        for c in range(SLAB):
            ybuf[slot, :, c] = y[:, c * LANES:(c + 1) * LANES].reshape(groups, 8, LANES)

    scatter_start(slot)

    @pl.when(i + 2 < n_tiles)
    def _():
        for cp in idx_copies(i + 2, slot):
            cp.start()

    @pl.when(i == n_tiles - 1)
    def _():
        wait_rows(ybuf, ssem, other)
        wait_rows(ybuf, ssem, slot)


def _moe(xn, route, w_gate, w_up, w_down):
    B, S = xn.shape[:2]
    n_tok = B * S
    tm = MOE_TILE
    tile_a, tile_b, valid, tok, dst, row_gates = _moe_plan(route.reshape(n_tok, LANES), n_tok)
    n_tiles = tile_a.shape[0]
    up = lambda which: pl.BlockSpec((1, D_MODEL, D_EXPERT), lambda i, ta, tb, nv: ((ta, tb)[which][i], 0, 0))
    down = lambda which: pl.BlockSpec((1, D_EXPERT, D_MODEL), lambda i, ta, tb, nv: ((ta, tb)[which][i], 0, 0))
    grid_spec = pltpu.PrefetchScalarGridSpec(
        num_scalar_prefetch=3,
        grid=(n_tiles,),
        in_specs=[
            pl.BlockSpec(memory_space=pl.ANY),
            pl.BlockSpec(memory_space=pl.ANY),
            pl.BlockSpec(memory_space=pl.ANY),
            pl.BlockSpec((tm, TOP_K), lambda i, ta, tb, nv: (i, 0)),
            up(0), up(0), down(0), up(1), up(1), down(1),
        ],
        out_specs=pl.BlockSpec(memory_space=pl.ANY),
        scratch_shapes=[
            pltpu.SMEM((2 * tm,), jnp.int32),
            pltpu.SMEM((2 * tm,), jnp.int32),
            pltpu.VMEM((2, tm // 8, SLAB, 8, LANES), F32),
            pltpu.VMEM((2, tm // 8, SLAB, 8, LANES), F32),
            pltpu.SemaphoreType.DMA((2, 2)),
            pltpu.SemaphoreType.DMA((2,)),
            pltpu.SemaphoreType.DMA((2,)),
        ],
    )
    wg, wu, wd = w_gate.astype(BF16), w_up.astype(BF16), w_down.astype(BF16)
    return pl.pallas_call(
        _moe_kernel,
        grid_spec=grid_spec,
        out_shape=jax.ShapeDtypeStruct((n_tok + 2 * tm, SLAB, LANES), F32),
        compiler_params=_params("arbitrary"),
        name="moe_experts",
    )(tile_a, tile_b, valid, tok, dst, xn.reshape(n_tok, SLAB, LANES), row_gates, wg, wu, wd, wg, wu, wd)


def _moe_out_spec(S, tm):
    return pl.BlockSpec((tm, SLAB, LANES), lambda b, t: (b * (S // tm) + t, 0, 0))


def _residue_perm(tm, d):
    i = jnp.arange(tm)
    src = (i % (tm // d)) * d + i // (tm // d)
    return (src[:, None] == jnp.arange(tm)[None, :]).astype(BF16)


def _dil_in_kernel(h_ref, y_ref, kvg_ref, bg_ref, wk_ref, wv_ref, wq_ref, kgain_ref,
                   qgain_ref, p1_ref, p2_ref, hsum_ref, h2_ref, *qkv_refs):
    tm = h_ref.shape[1]
    w = B_OUT_WIDTH
    h2 = h_ref[0] + _rows_from_slabs(y_ref)
    h2_ref[0] = h2
    u = h2 * lax.rsqrt(jnp.mean(h2 * h2, axis=-1, keepdims=True) + EPS)
    xkv = (u * kvg_ref[...]).astype(BF16)
    xq = (u * bg_ref[...]).astype(BF16)

    def project(g):
        cols = slice(g * w, (g + 1) * w)
        return _dot(xq, wq_ref[:, cols]), _dot(xkv, wk_ref[:, cols]), _dot(xkv, wv_ref[:, cols])

    def finish(g, d, qkv):
        q, k, v = qkv
        cat = jnp.concatenate([(_head_rms(q, hsum_ref) * qgain_ref[...]).astype(BF16),
                               (_head_rms(k, hsum_ref) * kgain_ref[...]).astype(BF16),
                               v.astype(BF16)], axis=1)
        if d > 1:
            cat = _dot((p1_ref if g == 1 else p2_ref)[...], cat).astype(BF16)
        rows = tm // d
        for t in range(3):
            ref = qkv_refs[3 * g + t]
            for r in range(d):
                ref[0, r] = cat[r * rows:(r + 1) * rows, t * w:(t + 1) * w]

    nxt = project(0)
    for g, d in enumerate(B_DILATIONS):
        cur = nxt
        if g + 1 < B_GROUPS:
            nxt = project(g + 1)
        finish(g, d, cur)


def _dil_in(h, y, kv_norm, b_norm, kv_w, w_q, k_gain, q_gain):
    B, S, _ = h.shape
    tm = TOK_TILE
    const = lambda b, t: (0, 0)
    tok = lambda w: pl.BlockSpec((1, tm, w), lambda b, t: (b, t, 0))
    qkv_specs, qkv_shapes = [], []
    for d in B_DILATIONS:
        qkv_specs += [pl.BlockSpec((1, d, tm // d, B_OUT_WIDTH), lambda b, t: (b, 0, t, 0))] * 3
        qkv_shapes += [jax.ShapeDtypeStruct((B, d, S // d, B_OUT_WIDTH), BF16)] * 3
    outs = pl.pallas_call(
        _dil_in_kernel,
        grid=(B, S // tm),
        in_specs=[
            tok(D_MODEL),
            _moe_out_spec(S, tm),
            pl.BlockSpec((1, D_MODEL), const),
            pl.BlockSpec((1, D_MODEL), const),
            pl.BlockSpec((D_MODEL, B_WIDTH), const),
            pl.BlockSpec((D_MODEL, B_WIDTH), const),
            pl.BlockSpec((D_MODEL, B_WIDTH), const),
            pl.BlockSpec((1, B_OUT_WIDTH), const),
            pl.BlockSpec((1, B_OUT_WIDTH), const),
            pl.BlockSpec((tm, tm), const),
            pl.BlockSpec((tm, tm), const),
            pl.BlockSpec((HSUM_COLS, HSUM_COLS), const),
        ],
        out_specs=[tok(D_MODEL)] + qkv_specs,
        out_shape=[jax.ShapeDtypeStruct((B, S, D_MODEL), F32)] + qkv_shapes,
        compiler_params=_params("arbitrary", "arbitrary"),
        name="dil_in",
    )(h, y, kv_norm.reshape(1, D_MODEL), b_norm.reshape(1, D_MODEL),
      kv_w[:, :B_WIDTH].astype(BF16), kv_w[:, B_WIDTH:].astype(BF16), w_q.astype(BF16),
      jnp.tile(k_gain, B_HEADS_PER_GROUP).reshape(1, B_OUT_WIDTH),
      jnp.tile(q_gain * (SCALE * LOG2E), B_HEADS_PER_GROUP).reshape(1, B_OUT_WIDTH),
      _residue_perm(tm, B_DILATIONS[1]), _residue_perm(tm, B_DILATIONS[2]), _head_sum_matrix())
    return outs[0], [outs[1 + 3 * g:4 + 3 * g] for g in range(B_GROUPS)]


def _t5_bucket(dist):
    max_exact = NUM_BUCKETS // 2
    d_f = jnp.maximum(dist, max_exact).astype(F32)
    large = max_exact + (jnp.log(d_f / max_exact) / math.log(MAX_DISTANCE / max_exact)
                         * (NUM_BUCKETS - max_exact)).astype(jnp.int32)
    large = jnp.minimum(large, NUM_BUCKETS - 1)
    return jnp.where(dist < max_exact, dist, large)


def _branch_bias(rel_bias, g, d):
    a = jnp.arange(BLOCK)[:, None]
    b = jnp.arange(2 * BLOCK)[None, :]
    n = BLOCK + a - b
    band = (n >= 0) & (n <= B_WINDOWS[g] // d)
    onehot = (_t5_bucket(jnp.maximum(n, 0) * d)[..., None] == jnp.arange(NUM_BUCKETS)).astype(F32)
    table = rel_bias[:, g * B_HEADS_PER_GROUP:(g + 1) * B_HEADS_PER_GROUP].astype(F32)
    bias = jnp.einsum("abk,kh->hab", onehot, table, precision=lax.Precision.HIGHEST) * LOG2E
    return jnp.where(band[None], bias, NEG)


DIL_SUB = 4
DIL_AHEAD = 3


def _dil_attn_kernel(q_ref, kp_ref, kc_ref, vp_ref, vc_ref, bias_ref, o_ref, lse_ref):
    n = pl.program_id(2)
    lane = lax.broadcasted_iota(jnp.int32, (1, LANES), 1)
    first = lane < HEAD_DIM
    col = lax.broadcasted_iota(jnp.int32, (1, 2 * BLOCK), 1)
    dead = (n == 0) & (col < BLOCK)
    pairs = B_HEADS_PER_GROUP // 2

    def band(cur_ref, prev_ref, i, j):
        sl = slice(j * LANES, (j + 1) * LANES)
        if i == 0:
            return jnp.concatenate([prev_ref[0, 0, :, sl], cur_ref[0, 0, :BLOCK, sl]], axis=0)
        return cur_ref[0, 0, (i - 1) * BLOCK:(i + 1) * BLOCK, sl]

    def scores(i, j, hh):
        q = pltpu.bitcast(q_ref[0, 0, i * BLOCK:(i + 1) * BLOCK, j * LANES:(j + 1) * LANES], jnp.int32)
        qm = pltpu.bitcast(jnp.where(first if hh == 0 else jnp.logical_not(first), q, 0), BF16)
        return _dot_nt(qm, band(kc_ref, kp_ref, i, j))

    items = [(i, j, hh) for i in range(q_ref.shape[2] // BLOCK) for j in range(pairs) for hh in range(2)]
    st, sb, mx, pr, dn, outs = {}, {}, {}, {}, {}, {}
    state = {"m_blk": None, "den_blk": None}

    def stage_max(t):
        i, j, hh = t
        bias = bias_ref[2 * j + hh]
        sb[t] = st.pop(t) + (jnp.where(dead, NEG, bias) if i == 0 else bias)
        mx[t] = jnp.max(sb[t], axis=-1, keepdims=True)

    def stage_exp(t):
        pr[t] = jnp.exp2(sb.pop(t) - mx[t])
        dn[t] = jnp.sum(pr[t], axis=-1, keepdims=True)

    def stage_out(t):
        i, j, hh = t
        m, den = mx.pop(t), dn.pop(t)
        outs[hh] = _dot(pr.pop(t).astype(BF16), band(vc_ref, vp_ref, i, j)) * (1.0 / den)
        if j == 0 and hh == 0:
            state["m_blk"] = jnp.zeros((BLOCK, LANES), F32)
            state["den_blk"] = jnp.ones((BLOCK, LANES), F32)
        state["m_blk"] = jnp.where(lane == 2 * j + hh, m, state["m_blk"])
        state["den_blk"] = jnp.where(lane == 2 * j + hh, den, state["den_blk"])
        if hh == 1:
            o_ref[0, 0, i * BLOCK:(i + 1) * BLOCK, j * LANES:(j + 1) * LANES] = (
                jnp.where(first, outs[0], outs[1]).astype(BF16))
            if j == pairs - 1:
                lse = jnp.where(lane < B_HEADS_PER_GROUP, state["m_blk"] + jnp.log2(state["den_blk"]), 0.0)
                hi, mid, lo = _split3(lse)
                lse_ref[0, 0, i * BLOCK:(i + 1) * BLOCK, :] = (
                    hi + pltpu.roll(mid, 8, 1) + pltpu.roll(lo, 16, 1)).astype(BF16)

    n_items = len(items)
    for step in range(n_items + DIL_AHEAD + 2):
        if step < n_items:
            st[items[step]] = scores(*items[step])
        if 0 <= step - DIL_AHEAD < n_items:
            stage_max(items[step - DIL_AHEAD])
        if 0 <= step - DIL_AHEAD - 1 < n_items:
            stage_exp(items[step - DIL_AHEAD - 1])
        if 0 <= step - DIL_AHEAD - 2 < n_items:
            stage_out(items[step - DIL_AHEAD - 2])


def _dil_attn(q, k, v, bias, g):
    B, d, L, w = q.shape
    sub = min(DIL_SUB, L // BLOCK)
    rows = sub * BLOCK
    cur = pl.BlockSpec((1, 1, rows, w), lambda b, r, n: (b, r, n, 0))
    prev = pl.BlockSpec((1, 1, BLOCK, w), lambda b, r, n: (b, r, jnp.maximum(n * sub - 1, 0), 0))
    return pl.pallas_call(
        _dil_attn_kernel,
        grid=(B, d, L // rows),
        in_specs=[cur, prev, cur, prev, cur,
                  pl.BlockSpec((B_HEADS_PER_GROUP, BLOCK, 2 * BLOCK), lambda b, r, n: (0, 0, 0))],
        out_specs=[cur, pl.BlockSpec((1, 1, rows, LANES), lambda b, r, n: (b, r, n, 0))],
        out_shape=[
            jax.ShapeDtypeStruct((B, d, L, w), BF16),
            jax.ShapeDtypeStruct((B, d, L, LANES), BF16),
        ],
        compiler_params=_params("arbitrary", "arbitrary", "arbitrary"),
        name=f"dil_attn_{g}",
    )(q, k, k, v, v, bias)


def _dil_out_kernel(o0_ref, o1_ref, o2_ref, l0_ref, l1_ref, l2_ref, h_ref, wo_ref, fg_ref,
                    wrh_ref, wrl_ref, br_ref, p1t_ref, p2t_ref, h_out_ref, xn_ref, route_ref):
    tm = h_ref.shape[1]

    def natural(ref, pt_ref):
        x = ref[0].reshape(tm, ref.shape[3])
        return x.astype(F32) if pt_ref is None else _dot(pt_ref[...], x)

    def lse_of(ref, pt_ref):
        x = natural(ref, pt_ref)
        return x + pltpu.roll(x, LANES - 8, 1) + pltpu.roll(x, LANES - 16, 1)

    o0, o1, o2 = natural(o0_ref, None), natural(o1_ref, p1t_ref), natural(o2_ref, p2t_ref)
    l0, l1, l2 = lse_of(l0_ref, None), lse_of(l1_ref, p1t_ref), lse_of(l2_ref, p2t_ref)
    m = jnp.maximum(jnp.maximum(l0, l1), l2)
    e0, e1, e2 = jnp.exp2(l0 - m), jnp.exp2(l1 - m), jnp.exp2(l2 - m)
    den = e0 + e1 + e2
    row = lax.broadcasted_iota(jnp.int32, (LANES, B_OUT_WIDTH), 0)
    col = lax.broadcasted_iota(jnp.int32, (LANES, B_OUT_WIDTH), 1)
    spread = (jnp.right_shift(col, 6) == row).astype(BF16)

    def widen(a):
        hi = a.astype(BF16)
        lo = (a - hi.astype(F32)).astype(BF16)
        return _dot(hi, spread) + _dot(lo, spread)

    merged = widen(e0 / den) * o0 + widen(e1 / den) * o1 + widen(e2 / den) * o2
    h = h_ref[0] + _dot(merged.astype(BF16), wo_ref[...])
    _ffn_in(h, fg_ref, wrh_ref, wrl_ref, br_ref, h_out_ref, xn_ref, route_ref)


def _dil_out(os_, lses, h, w_out, ffn_g, router):
    B, S, _ = h.shape
    tm = TOK_TILE
    wrh, wrl, br = router
    const = lambda b, t: (0, 0)
    tok = lambda w: pl.BlockSpec((1, tm, w), lambda b, t: (b, t, 0))
    res = lambda w: [pl.BlockSpec((1, d, tm // d, w), lambda b, t: (b, 0, t, 0)) for d in B_DILATIONS]
    out_specs, out_shape = _token_out_specs(B, S, tm)
    return pl.pallas_call(
        _dil_out_kernel,
        grid=(B, S // tm),
        in_specs=res(B_OUT_WIDTH) + res(LANES) + [
            tok(D_MODEL),
            pl.BlockSpec((B_OUT_WIDTH, D_MODEL), const),
            pl.BlockSpec((1, D_MODEL), const),
            pl.BlockSpec((D_MODEL, LANES), const),
            pl.BlockSpec((D_MODEL, LANES), const),
            pl.BlockSpec((1, LANES), const),
            pl.BlockSpec((tm, tm), const),
            pl.BlockSpec((tm, tm), const),
        ],
        out_specs=out_specs,
        out_shape=out_shape,
        compiler_params=_params("arbitrary", "arbitrary"),
        name="dil_out",
    )(*os_, *lses, h, w_out.astype(BF16), ffn_g.reshape(1, D_MODEL), wrh, wrl, br,
      _residue_perm(tm, B_DILATIONS[1]).T, _residue_perm(tm, B_DILATIONS[2]).T)


def _final_kernel(h_ref, y_ref, o_ref):
    o_ref[0] = h_ref[0] + _rows_from_slabs(y_ref)


def _final(h, y):
    B, S, _ = h.shape
    tm = TOK_TILE
    tok = lambda w: pl.BlockSpec((1, tm, w), lambda b, t: (b, t, 0))
    return pl.pallas_call(
        _final_kernel,
        grid=(B, S // tm),
        in_specs=[tok(D_MODEL), _moe_out_spec(S, tm)],
        out_specs=tok(D_MODEL),
        out_shape=jax.ShapeDtypeStruct((B, S, D_MODEL), F32),
        compiler_params=_params("arbitrary", "arbitrary"),
        name="moe_final",
    )(h, y)


def kernel(x, a_norm, a_w_in, a_b_f, a_q_gain, a_k_gain, a_w_out, kv_norm, kv_w, kv_k_gain, rel_bias, b_norm, b_w_q, b_q_gain, b_w_out, ffn_norm, moe_w_group, moe_b_group, moe_w_expert, moe_b_expert, moe_w_gate, moe_w_up, moe_w_down):
    routers = [_router_weights(moe_w_group[l], moe_b_group[l], moe_w_expert[l], moe_b_expert[l])
               for l in range(2)]
    q, k, vt = _fox_in(x, a_norm[0], a_w_in[0], a_b_f[0], a_q_gain[0], a_k_gain[0])
    ot = _fox_attn(q, k, vt)
    h1, xn1, route1 = _fox_out(ot, x, a_w_out[0], ffn_norm[0], routers[0])
    y1 = _moe(xn1, route1, moe_w_gate[0], moe_w_up[0], moe_w_down[0])
    h2, qkv = _dil_in(h1, y1, kv_norm, b_norm[0], kv_w, b_w_q[0], kv_k_gain, b_q_gain[0])
    outs, lses = [], []
    for g, d in enumerate(B_DILATIONS):
        o, lse = _dil_attn(*qkv[g], _branch_bias(rel_bias, g, d), g)
        outs.append(o)
        lses.append(lse)
    h3, xn3, route3 = _dil_out(outs, lses, h2, b_w_out[0], ffn_norm[1], routers[1])
    y3 = _moe(xn3, route3, moe_w_gate[1], moe_w_up[1], moe_w_down[1])
    return _final(h3, y3)
```

```python
import functools
import math

import jax
import jax.numpy as jnp
from jax import lax
from jax.experimental import pallas as pl
from jax.experimental.pallas import tpu as pltpu

F32 = jnp.float32
BF16 = jnp.bfloat16

D_MODEL = 1024
HEAD_DIM = 64
A_HEADS = 16
B_GROUPS = 3
B_HEADS_PER_GROUP = 8
B_HEADS = 24
B_WIDTH = B_HEADS * HEAD_DIM
B_OUT_WIDTH = B_HEADS_PER_GROUP * HEAD_DIM
B_WINDOWS = (128, 512, 2048)
B_DILATIONS = (1, 4, 16)
BLOCK = 128
NUM_BUCKETS = 32
MAX_DISTANCE = 2048
N_GROUPS = 4
EXPERTS_PER_GROUP = 4
N_EXPERTS = 16
TOP_K = 2
D_EXPERT = 512
EPS = 1e-6
NEG = -1e30
SCALE = HEAD_DIM ** -0.5
LOG2E = 1.4426950408889634

LANES = 128
TOK_TILE = 512
VMEM_LIMIT = 56 * 1024 * 1024


def _params(*sem):
    return pltpu.CompilerParams(dimension_semantics=sem, vmem_limit_bytes=VMEM_LIMIT)


def _rms(x, g):
    return x * lax.rsqrt(jnp.mean(x * x, axis=-1, keepdims=True) + EPS) * g


def _split3(x):
    hi = x.astype(BF16).astype(F32)
    r = x - hi
    mid = r.astype(BF16).astype(F32)
    return hi, mid, r - mid


SLAB = D_MODEL // LANES


def _rows_from_slabs(ref):
    return jnp.concatenate([ref[:, c, :] for c in range(SLAB)], axis=1)


def _rows_to_slabs(ref, x):
    for c in range(SLAB):
        ref[:, c, :] = x[:, c * LANES:(c + 1) * LANES]


def _dot(a, b):
    return jnp.dot(a, b, preferred_element_type=F32)


def _dot_nt(a, b):
    return lax.dot_general(a, b, (((1,), (1,)), ((), ())), preferred_element_type=F32)


def _dot_tn(a, b):
    return lax.dot_general(a, b, (((0,), (0,)), ((), ())), preferred_element_type=F32)


HSUM_COLS = 256


def _head_sum_matrix():
    i = jnp.arange(HSUM_COLS) // HEAD_DIM
    return (i[:, None] == i[None, :]).astype(BF16)


def _head_rms(p, hsum_ref):
    outs = []
    for j in range(p.shape[1] // HSUM_COLS):
        s = p[:, j * HSUM_COLS:(j + 1) * HSUM_COLS]
        ss = _dot((s * s).astype(BF16), hsum_ref[...])
        outs.append(s * lax.rsqrt(ss * (1.0 / HEAD_DIM) + EPS))
    return outs[0] if len(outs) == 1 else jnp.concatenate(outs, axis=1)


FOX_IN_COLS = 256
FOX_IN_AHEAD = 2


def _fox_in_kernel(x_ref, g_ref, wqk_ref, wvt_ref, wf_ref, bf_ref, qg_ref, kg_ref, sel_ref, hsum_ref,
                   q_ref, k_ref, vt_ref, carry_ref):
    tm = x_ref.shape[1]
    aw = A_HEADS * HEAD_DIM

    @pl.when(pl.program_id(1) == 0)
    def _():
        carry_ref[...] = jnp.zeros_like(carry_ref)

    xb = _rms(x_ref[0], g_ref[...]).astype(BF16)
    lane = lax.broadcasted_iota(jnp.int32, (1, LANES), 1)
    z = _dot(xb, wf_ref[...]) + bf_ref[...]
    lf = jnp.minimum(z, 0.0) - jnp.log1p(jnp.exp(-jnp.abs(z)))
    lf = jnp.where(lane < A_HEADS, lf, 0.0)
    row = lax.broadcasted_iota(jnp.int32, (tm, tm), 0)
    col = lax.broadcasted_iota(jnp.int32, (tm, tm), 1)
    tri = (col <= row).astype(BF16)
    hi, mid, lo = _split3(lf)
    cum = (_dot(tri, hi.astype(BF16)) + _dot(tri, mid.astype(BF16))
           + _dot(tri, lo.astype(BF16))) + carry_ref[...]
    carry_ref[...] = cum[tm - 1:tm, :]
    c_hi, c_mid, c_lo = _split3(cum * (-LOG2E))
    cc = (c_hi + pltpu.roll(c_mid, A_HEADS, 1) + pltpu.roll(c_lo, 2 * A_HEADS, 1)).astype(BF16)

    head_lane = lane < HEAD_DIM
    ones_vec = jnp.where((lane >= HEAD_DIM) & (lane < HEAD_DIM + 3), 1.0, 0.0)
    per = FOX_IN_COLS // HEAD_DIM
    n_tiles = 2 * aw // FOX_IN_COLS

    def project(n):
        return _dot(xb, wqk_ref[:, n * FOX_IN_COLS:(n + 1) * FOX_IN_COLS])

    def finish(n, p):
        is_k = n * FOX_IN_COLS >= aw
        h0 = (n * FOX_IN_COLS - (aw if is_k else 0)) // HEAD_DIM
        if is_k:
            aug = _dot(cc, sel_ref[:, h0 * LANES:(h0 + per) * LANES])
        pn = _head_rms(p, hsum_ref)
        for i in range(per):
            s = pn[:, (i // 2) * LANES:(i // 2 + 1) * LANES]
            if i % 2:
                s = pltpu.roll(s, HEAD_DIM, 1)
            dst = slice((h0 + i) * LANES, (h0 + i + 1) * LANES)
            if is_k:
                k_ref[0, :, dst] = jnp.where(head_lane, s * kg_ref[...], aug[:, i * LANES:(i + 1) * LANES]).astype(BF16)
            else:
                q_ref[0, :, dst] = jnp.where(head_lane, s * qg_ref[...], ones_vec).astype(BF16)

    pending = {n: project(n) for n in range(min(FOX_IN_AHEAD, n_tiles))}
    for n in range(n_tiles):
        if n + FOX_IN_AHEAD < n_tiles:
            pending[n + FOX_IN_AHEAD] = project(n + FOX_IN_AHEAD)
        elif n + FOX_IN_AHEAD == n_tiles:
            vt_ref[0, 0] = _dot_nt(wvt_ref[...], xb).astype(BF16)
        finish(n, pending.pop(n))


def _fox_in(x, a_norm, w_in, b_f, q_gain, k_gain):
    B, S, _ = x.shape
    tm = TOK_TILE
    nt = S // tm
    aw = A_HEADS * HEAD_DIM
    wqk = w_in[:, :2 * aw].astype(BF16)
    wvt = w_in[:, 2 * aw:3 * aw].T.astype(BF16)
    wf = jnp.pad(w_in[:, 3 * aw:], ((0, 0), (0, LANES - A_HEADS))).astype(BF16)
    bf = jnp.pad(b_f, (0, LANES - A_HEADS)).reshape(1, LANES)
    qg = jnp.pad(q_gain * (SCALE * LOG2E), (0, LANES - HEAD_DIM)).reshape(1, LANES)
    kg = jnp.pad(k_gain, (0, LANES - HEAD_DIM)).reshape(1, LANES)
    src = jnp.arange(LANES)
    dst = (src % A_HEADS) * LANES + HEAD_DIM + src // A_HEADS
    sel = ((dst[:, None] == jnp.arange(A_HEADS * LANES)[None, :]) & (src[:, None] < 3 * A_HEADS)).astype(BF16)
    const = lambda b, t: (0, 0)
    return pl.pallas_call(
        _fox_in_kernel,
        grid=(B, nt),
        in_specs=[
            pl.BlockSpec((1, tm, D_MODEL), lambda b, t: (b, t, 0)),
            pl.BlockSpec((1, D_MODEL), const),
            pl.BlockSpec((D_MODEL, 2 * aw), const),
            pl.BlockSpec((aw, D_MODEL), const),
            pl.BlockSpec((D_MODEL, LANES), const),
            pl.BlockSpec((1, LANES), const),
            pl.BlockSpec((1, LANES), const),
            pl.BlockSpec((1, LANES), const),
            pl.BlockSpec((LANES, A_HEADS * LANES), const),
            pl.BlockSpec((HSUM_COLS, HSUM_COLS), const),
        ],
        out_specs=[
            pl.BlockSpec((1, tm, A_HEADS * LANES), lambda b, t: (b, t, 0)),
            pl.BlockSpec((1, tm, A_HEADS * LANES), lambda b, t: (b, t, 0)),
            pl.BlockSpec((1, 1, aw, tm), lambda b, t: (b, t, 0, 0)),
        ],
        out_shape=[
            jax.ShapeDtypeStruct((B, S, A_HEADS * LANES), BF16),
            jax.ShapeDtypeStruct((B, S, A_HEADS * LANES), BF16),
            jax.ShapeDtypeStruct((B, nt, aw, tm), BF16),
        ],
        scratch_shapes=[pltpu.VMEM((1, LANES), F32)],
        compiler_params=_params("arbitrary", "arbitrary"),
        name="fox_in",
    )(x, a_norm.reshape(1, D_MODEL), wqk, wvt, wf, bf, qg, kg, sel, _head_sum_matrix())


FOX_TQ = 2048
FOX_QCHUNK = 256
FOX_UNROLL = 4
FOX_AHEAD = 4


def _fox_attn_kernel(q_ref, k_ref, vt_ref, o_ref, *scratch):
    tq = q_ref.shape[1]
    tk = vt_ref.shape[3]
    qc = FOX_QCHUNK
    nc = tq // qc
    ratio = tq // tk
    qi = pl.program_id(2)
    acc_refs, m_refs = scratch[:nc], scratch[nc:]
    for c in range(nc):
        m_refs[c][...] = jnp.full_like(m_refs[c], NEG)
        acc_refs[c][...] = jnp.zeros_like(acc_refs[c])
    ones = jnp.ones((16, tk), BF16)

    def scores(kj, c):
        return _dot_nt(kj, q_ref[0, c * qc:(c + 1) * qc, :])

    def update(st, vj, c, mask):
        if mask is not None:
            st = jnp.where(mask, st, NEG)
        m_prev = m_refs[c][...]
        m_new = jnp.maximum(m_prev, jnp.max(st, axis=0, keepdims=True))
        p = jnp.exp2((st - m_new).astype(BF16))
        alpha = jnp.exp2(m_prev - m_new)
        acc_refs[c][...] = alpha * acc_refs[c][...] + _dot(vj, p)
        m_refs[c][...] = m_new

    def load(j):
        kj = k_ref[0, pl.ds(pl.multiple_of(j * tk, tk), tk), :]
        vj = jnp.concatenate([vt_ref[0, j], ones], axis=0)
        return kj, vj

    def run(work):
        st = [None] * len(work)
        for i in range(min(FOX_AHEAD, len(work))):
            st[i] = scores(work[i][0], work[i][2])
        for i, (kj, vj, c, mask) in enumerate(work):
            if i + FOX_AHEAD < len(work):
                st[i + FOX_AHEAD] = scores(work[i + FOX_AHEAD][0], work[i + FOX_AHEAD][2])
            update(st[i], vj, c, mask)
            st[i] = None

    def body(jj, carry):
        work = []
        for u in range(FOX_UNROLL):
            kj, vj = load(jj * FOX_UNROLL + u)
            work += [(kj, vj, c, None) for c in range(nc)]
        run(work)
        return carry

    lax.fori_loop(0, qi * (ratio // FOX_UNROLL), body, 0)
    work = []
    for d in range(ratio):
        kj, vj = load(qi * ratio + d)
        for c in range(nc):
            if d * tk > (c + 1) * qc - 1:
                continue
            if (d + 1) * tk - 1 <= c * qc:
                work.append((kj, vj, c, None))
            else:
                kpos = d * tk + lax.broadcasted_iota(jnp.int32, (tk, qc), 0)
                qpos = c * qc + lax.broadcasted_iota(jnp.int32, (tk, qc), 1)
                work.append((kj, vj, c, kpos <= qpos))
    run(work)
    for c in range(nc):
        acc = acc_refs[c][...]
        o_ref[0, :, c * qc:(c + 1) * qc] = (acc[:HEAD_DIM] / acc[HEAD_DIM:HEAD_DIM + 1]).astype(BF16)


def _fox_attn(q, k, vt):
    B, S, _ = q.shape
    nt, tk = vt.shape[1], vt.shape[3]
    tq = FOX_TQ
    assert tq % (tk * FOX_UNROLL) == 0 and tq % FOX_QCHUNK == 0
    return pl.pallas_call(
        _fox_attn_kernel,
        grid=(B, A_HEADS, S // tq),
        in_specs=[
            pl.BlockSpec((1, tq, LANES), lambda b, h, i: (b, i, h)),
            pl.BlockSpec((1, S, LANES), lambda b, h, i: (b, 0, h)),
            pl.BlockSpec((1, nt, HEAD_DIM, tk), lambda b, h, i: (b, 0, h, 0)),
        ],
        out_specs=pl.BlockSpec((1, HEAD_DIM, tq), lambda b, h, i: (b, h, i)),
        out_shape=jax.ShapeDtypeStruct((B, A_HEADS * HEAD_DIM, S), BF16),
        scratch_shapes=([pltpu.VMEM((HEAD_DIM + 16, FOX_QCHUNK), F32)] * (tq // FOX_QCHUNK)
                        + [pltpu.VMEM((1, FOX_QCHUNK), F32)] * (tq // FOX_QCHUNK)),
        compiler_params=_params("arbitrary", "arbitrary", "arbitrary"),
        name="fox_attn",
    )(q, k, vt)


def _route(xn, wrh_ref, wrl_ref, br_ref):
    xh = xn.astype(BF16)
    xl = (xn - xh.astype(F32)).astype(BF16)
    wh = wrh_ref[...]
    logits = _dot(xh, wh) + _dot(xl, wh) + _dot(xh, wrl_ref[...]) + br_ref[...]
    lane = lax.broadcasted_iota(jnp.int32, (1, LANES), 1)
    lanef = lane.astype(F32)
    far = float(LANES)

    gl = jnp.where(lane < N_GROUPS, logits, NEG)
    gm = jnp.max(gl, axis=-1, keepdims=True)
    g_val = 1.0 / jnp.sum(jnp.exp(gl - gm), axis=-1, keepdims=True)
    g_idx = jnp.min(jnp.where(gl == gm, lanef, far), axis=-1, keepdims=True)

    lo = N_GROUPS + EXPERTS_PER_GROUP * g_idx
    el = jnp.where((lanef >= lo) & (lanef < lo + EXPERTS_PER_GROUP), logits, NEG)
    em1 = jnp.max(el, axis=-1, keepdims=True)
    ez = jnp.sum(jnp.exp(el - em1), axis=-1, keepdims=True)
    i1 = jnp.min(jnp.where(el == em1, lanef, far), axis=-1, keepdims=True)
    el2 = jnp.where(lanef == i1, NEG, el)
    em2 = jnp.max(el2, axis=-1, keepdims=True)
    i2 = jnp.min(jnp.where(el2 == em2, lanef, far), axis=-1, keepdims=True)
    p1 = 1.0 / ez
    p2 = jnp.exp(em2 - em1) / ez
    den = p1 + p2
    gate1 = g_val * (p1 / den)
    gate2 = g_val * (p2 / den)
    return jnp.where(lane == 0, i1 - N_GROUPS,
                     jnp.where(lane == 1, i2 - N_GROUPS,
                               jnp.where(lane == 2, gate1,
                                         jnp.where(lane == 3, gate2, 0.0))))


def _router_weights(w_group, b_group, w_expert, b_expert):
    w = jnp.pad(jnp.concatenate([w_group, w_expert], axis=1),
                ((0, 0), (0, LANES - N_GROUPS - N_EXPERTS)))
    b = jnp.pad(jnp.concatenate([b_group, b_expert]), (0, LANES - N_GROUPS - N_EXPERTS))
    wh = w.astype(BF16)
    wl = (w - wh.astype(F32)).astype(BF16)
    return wh, wl, b.reshape(1, LANES)


def _ffn_in(h, fg_ref, wrh_ref, wrl_ref, br_ref, h_ref, xn_ref, route_ref):
    h_ref[0] = h
    xn = _rms(h, fg_ref[...])
    _rows_to_slabs(xn_ref.at[0], xn)
    route_ref[0] = _route(xn, wrh_ref, wrl_ref, br_ref)


def _fox_out_kernel(ot_ref, x_ref, wo_ref, fg_ref, wrh_ref, wrl_ref, br_ref,
                    h_ref, xn_ref, route_ref):
    h = x_ref[0] + _dot_tn(ot_ref[0], wo_ref[...])
    _ffn_in(h, fg_ref, wrh_ref, wrl_ref, br_ref, h_ref, xn_ref, route_ref)


def _token_out_specs(B, S, tm):
    specs = [
        pl.BlockSpec((1, tm, D_MODEL), lambda b, t: (b, t, 0)),
        pl.BlockSpec((1, tm, SLAB, LANES), lambda b, t: (b, t, 0, 0)),
        pl.BlockSpec((1, tm, LANES), lambda b, t: (b, t, 0)),
    ]
    shapes = [
        jax.ShapeDtypeStruct((B, S, D_MODEL), F32),
        jax.ShapeDtypeStruct((B, S, SLAB, LANES), F32),
        jax.ShapeDtypeStruct((B, S, LANES), F32),
    ]
    return specs, shapes


def _fox_out(ot, x, w_out, ffn_g, router):
    B, S, _ = x.shape
    tm = TOK_TILE
    wrh, wrl, br = router
    const = lambda b, t: (0, 0)
    out_specs, out_shape = _token_out_specs(B, S, tm)
    return pl.pallas_call(
        _fox_out_kernel,
        grid=(B, S // tm),
        in_specs=[
            pl.BlockSpec((1, A_HEADS * HEAD_DIM, tm), lambda b, t: (b, 0, t)),
            pl.BlockSpec((1, tm, D_MODEL), lambda b, t: (b, t, 0)),
            pl.BlockSpec((A_HEADS * HEAD_DIM, D_MODEL), const),
            pl.BlockSpec((1, D_MODEL), const),
            pl.BlockSpec((D_MODEL, LANES), const),
            pl.BlockSpec((D_MODEL, LANES), const),
            pl.BlockSpec((1, LANES), const),
        ],
        out_specs=out_specs,
        out_shape=out_shape,
        compiler_params=_params("arbitrary", "arbitrary"),
        name="fox_out",
    )(ot, x, w_out.astype(BF16), ffn_g.reshape(1, D_MODEL), wrh, wrl, br)


MOE_TILE = 256
PAIRS_PER_GROUP = EXPERTS_PER_GROUP * (EXPERTS_PER_GROUP - 1) // 2
N_CLASSES = N_GROUPS * PAIRS_PER_GROUP
_PAIR_LO = [a for a in range(EXPERTS_PER_GROUP) for b in range(a + 1, EXPERTS_PER_GROUP)]
_PAIR_HI = [b for a in range(EXPERTS_PER_GROUP) for b in range(a + 1, EXPERTS_PER_GROUP)]


def _moe_plan(route, n_tok):
    tm = MOE_TILE
    n_tiles = n_tok // tm + N_CLASSES
    e1, e2 = route[:, 0].astype(jnp.int32), route[:, 1].astype(jnp.int32)
    swap = e2 < e1
    lo, hi = jnp.where(swap, e2, e1), jnp.where(swap, e1, e2)
    gates = jnp.stack([jnp.where(swap, route[:, 3], route[:, 2]),
                       jnp.where(swap, route[:, 2], route[:, 3])], axis=1)
    pair_of = jnp.zeros((EXPERTS_PER_GROUP, EXPERTS_PER_GROUP), jnp.int32).at[
        jnp.array(_PAIR_LO), jnp.array(_PAIR_HI)].set(jnp.arange(PAIRS_PER_GROUP, dtype=jnp.int32))
    cls = (lo // EXPERTS_PER_GROUP) * PAIRS_PER_GROUP + pair_of[lo % EXPERTS_PER_GROUP, hi % EXPERTS_PER_GROUP]
    counts = jnp.sum(cls[:, None] == jnp.arange(N_CLASSES)[None, :], axis=0).astype(jnp.int32)
    tiles_per = (counts + tm - 1) // tm
    tile_end = jnp.cumsum(tiles_per)
    n_used = tile_end[-1]
    tid = jnp.arange(n_tiles, dtype=jnp.int32)
    tid_c = jnp.minimum(tid, n_used - 1)
    tile_c = jnp.minimum(jnp.sum(tile_end[None, :] <= tid_c[:, None], axis=1), N_CLASSES - 1).astype(jnp.int32)
    base = (tile_c // PAIRS_PER_GROUP) * EXPERTS_PER_GROUP
    tile_a = (base + jnp.array(_PAIR_LO, jnp.int32)[tile_c % PAIRS_PER_GROUP]).astype(jnp.int32)
    tile_b = (base + jnp.array(_PAIR_HI, jnp.int32)[tile_c % PAIRS_PER_GROUP]).astype(jnp.int32)
    within = (tid_c - (tile_end - tiles_per)[tile_c]) * tm
    valid = jnp.where(tid < n_used, jnp.clip(counts[tile_c] - within, 0, tm), 0).astype(jnp.int32)
    j = jnp.arange(tm, dtype=jnp.int32)[None, :]
    unused = 2 * N_CLASSES + 1
    pad_key = jnp.where(j < (tiles_per * tm - counts)[:, None],
                        2 * jnp.arange(N_CLASSES, dtype=jnp.int32)[:, None] + 1, unused).reshape(-1)
    n_pad = N_CLASSES * tm
    keys = jnp.concatenate([2 * cls, pad_key])
    vals = jnp.concatenate([jnp.arange(n_tok, dtype=jnp.int32), jnp.zeros((n_pad,), jnp.int32)])
    g_lo = jnp.concatenate([gates[:, 0], jnp.zeros((n_pad,), F32)])
    g_hi = jnp.concatenate([gates[:, 1], jnp.zeros((n_pad,), F32)])
    keys, tok, g_lo, g_hi = lax.sort((keys, vals, g_lo, g_hi), num_keys=1, is_stable=True)
    live = (keys % 2 == 0).reshape(n_tiles, tm)
    tok = tok.reshape(n_tiles, tm)
    dst = jnp.where(live, tok, n_tok + (tid % 2)[:, None] * tm + j).astype(jnp.int32)
    return tile_a, tile_b, valid, tok, dst, jnp.stack([g_lo, g_hi], axis=1)


def _moe_kernel(ta_ref, tb_ref, nv_ref, tok_hbm, dst_hbm, xn_hbm, gate_ref, wga_ref, wua_ref, wda_ref,
                wgb_ref, wub_ref, wdb_ref, out_hbm, tok_s, dst_s, xbuf, ybuf, isem, gsem, ssem):
    del ta_ref, tb_ref
    groups = xbuf.shape[1]
    tm = groups * 8
    i = pl.program_id(0)
    n_tiles = pl.num_programs(0)
    slot = lax.rem(i, 2)
    other = 1 - slot

    def idx_copies(t, s):
        rows = pl.ds(s * tm, tm)
        return (pltpu.make_async_copy(tok_hbm.at[t], tok_s.at[rows], isem.at[0, s]),
                pltpu.make_async_copy(dst_hbm.at[t], dst_s.at[rows], isem.at[1, s]))

    def gather_start(s):
        def body(g, c):
            base = s * tm + g * 8
            for j in range(8):
                pltpu.make_async_copy(xn_hbm.at[tok_s[base + j]], xbuf.at[s, g, :, j, :],
                                      gsem.at[s]).start(priority=j % 2)
            return c
        lax.fori_loop(0, groups, body, 0)

    def scatter_start(s):
        def body(g, c):
            base = s * tm + g * 8
            for j in range(8):
                pltpu.make_async_copy(ybuf.at[s, g, :, j, :], out_hbm.at[dst_s[base + j]],
                                      ssem.at[s]).start(priority=j % 2)
            return c
        lax.fori_loop(0, groups, body, 0)

    def wait_rows(buf, sem, s):
        pltpu.make_async_copy(buf.at[s], buf.at[s], sem.at[s]).wait()

    @pl.when(i == 0)
    def _():
        for cp in idx_copies(0, 0):
            cp.start()
        for cp in idx_copies(0, 0):
            cp.wait()
        gather_start(0)
        for cp in idx_copies(1, 1):
            cp.start()

    @pl.when(i + 1 < n_tiles)
    def _():
        for cp in idx_copies(i + 1, other):
            cp.wait()
        gather_start(other)

    wait_rows(xbuf, gsem, slot)

    @pl.when(i >= 2)
    def _():
        wait_rows(ybuf, ssem, slot)

    @pl.when(nv_ref[i] > 0)
    def _():
        xb = jnp.concatenate([xbuf[slot, :, c].reshape(tm, LANES) for c in range(SLAB)], axis=1).astype(BF16)
        ga, ua = _dot(xb, wga_ref[0]), _dot(xb, wua_ref[0])
        gb, ub = _dot(xb, wgb_ref[0]), _dot(xb, wub_ref[0])
        ya = _dot((jax.nn.silu(ga) * ua).astype(BF16), wda_ref[0])
        yb = _dot((jax.nn.silu(gb) * ub).astype(BF16), wdb_ref[0])
        gates = gate_ref[...]
        y = gates[:, 0:1] * ya + gates[:, 1:2] * yb
        for c in range(SLAB):
            ybuf[slot, :, c] = y[:, c * LANES:(c + 1) * LANES].reshape(groups, 8, LANES)

    scatter_start(slot)

    @pl.when(i + 2 < n_tiles)
    def _():
        for cp in idx_copies(i + 2, slot):
            cp.start()

    @pl.when(i == n_tiles - 1)
    def _():
        wait_rows(ybuf, ssem, other)
        wait_rows(ybuf, ssem, slot)


def _moe(xn, route, w_gate, w_up, w_down):
    B, S = xn.shape[:2]
    n_tok = B * S
    tm = MOE_TILE
    tile_a, tile_b, valid, tok, dst, row_gates = _moe_plan(route.reshape(n_tok, LANES), n_tok)
    n_tiles = tile_a.shape[0]
    up = lambda which: pl.BlockSpec((1, D_MODEL, D_EXPERT), lambda i, ta, tb, nv: ((ta, tb)[which][i], 0, 0))
    down = lambda which: pl.BlockSpec((1, D_EXPERT, D_MODEL), lambda i, ta, tb, nv: ((ta, tb)[which][i], 0, 0))
    grid_spec = pltpu.PrefetchScalarGridSpec(
        num_scalar_prefetch=3,
        grid=(n_tiles,),
        in_specs=[
            pl.BlockSpec(memory_space=pl.ANY),
            pl.BlockSpec(memory_space=pl.ANY),
            pl.BlockSpec(memory_space=pl.ANY),
            pl.BlockSpec((tm, TOP_K), lambda i, ta, tb, nv: (i, 0)),
            up(0), up(0), down(0), up(1), up(1), down(1),
        ],
        out_specs=pl.BlockSpec(memory_space=pl.ANY),
        scratch_shapes=[
            pltpu.SMEM((2 * tm,), jnp.int32),
            pltpu.SMEM((2 * tm,), jnp.int32),
            pltpu.VMEM((2, tm // 8, SLAB, 8, LANES), F32),
            pltpu.VMEM((2, tm // 8, SLAB, 8, LANES), F32),
            pltpu.SemaphoreType.DMA((2, 2)),
            pltpu.SemaphoreType.DMA((2,)),
            pltpu.SemaphoreType.DMA((2,)),
        ],
    )
    wg, wu, wd = w_gate.astype(BF16), w_up.astype(BF16), w_down.astype(BF16)
    return pl.pallas_call(
        _moe_kernel,
        grid_spec=grid_spec,
        out_shape=jax.ShapeDtypeStruct((n_tok + 2 * tm, SLAB, LANES), F32),
        compiler_params=_params("arbitrary"),
        name="moe_experts",
    )(tile_a, tile_b, valid, tok, dst, xn.reshape(n_tok, SLAB, LANES), row_gates, wg, wu, wd, wg, wu, wd)


def _moe_out_spec(S, tm):
    return pl.BlockSpec((tm, SLAB, LANES), lambda b, t: (b * (S // tm) + t, 0, 0))


def _residue_perm(tm, d):
    i = jnp.arange(tm)
    src = (i % (tm // d)) * d + i // (tm // d)
    return (src[:, None] == jnp.arange(tm)[None, :]).astype(BF16)


def _dil_in_kernel(h_ref, y_ref, kvg_ref, bg_ref, wk_ref, wv_ref, wq_ref, kgain_ref,
                   qgain_ref, p1_ref, p2_ref, hsum_ref, h2_ref, *qkv_refs):
    tm = h_ref.shape[1]
    w = B_OUT_WIDTH
    h2 = h_ref[0] + _rows_from_slabs(y_ref)
    h2_ref[0] = h2
    u = h2 * lax.rsqrt(jnp.mean(h2 * h2, axis=-1, keepdims=True) + EPS)
    xkv = (u * kvg_ref[...]).astype(BF16)
    xq = (u * bg_ref[...]).astype(BF16)

    def project(g):
        cols = slice(g * w, (g + 1) * w)
        return _dot(xq, wq_ref[:, cols]), _dot(xkv, wk_ref[:, cols]), _dot(xkv, wv_ref[:, cols])

    def finish(g, d, qkv):
        q, k, v = qkv
        cat = jnp.concatenate([(_head_rms(q, hsum_ref) * qgain_ref[...]).astype(BF16),
                               (_head_rms(k, hsum_ref) * kgain_ref[...]).astype(BF16),
                               v.astype(BF16)], axis=1)
        if d > 1:
            cat = _dot((p1_ref if g == 1 else p2_ref)[...], cat).astype(BF16)
        rows = tm // d
        for t in range(3):
            ref = qkv_refs[3 * g + t]
            for r in range(d):
                ref[0, r] = cat[r * rows:(r + 1) * rows, t * w:(t + 1) * w]

    nxt = project(0)
    for g, d in enumerate(B_DILATIONS):
        cur = nxt
        if g + 1 < B_GROUPS:
            nxt = project(g + 1)
        finish(g, d, cur)


def _dil_in(h, y, kv_norm, b_norm, kv_w, w_q, k_gain, q_gain):
    B, S, _ = h.shape
    tm = TOK_TILE
    const = lambda b, t: (0, 0)
    tok = lambda w: pl.BlockSpec((1, tm, w), lambda b, t: (b, t, 0))
    qkv_specs, qkv_shapes = [], []
    for d in B_DILATIONS:
        qkv_specs += [pl.BlockSpec((1, d, tm // d, B_OUT_WIDTH), lambda b, t: (b, 0, t, 0))] * 3
        qkv_shapes += [jax.ShapeDtypeStruct((B, d, S // d, B_OUT_WIDTH), BF16)] * 3
    outs = pl.pallas_call(
        _dil_in_kernel,
        grid=(B, S // tm),
        in_specs=[
            tok(D_MODEL),
            _moe_out_spec(S, tm),
            pl.BlockSpec((1, D_MODEL), const),
            pl.BlockSpec((1, D_MODEL), const),
            pl.BlockSpec((D_MODEL, B_WIDTH), const),
            pl.BlockSpec((D_MODEL, B_WIDTH), const),
            pl.BlockSpec((D_MODEL, B_WIDTH), const),
            pl.BlockSpec((1, B_OUT_WIDTH), const),
            pl.BlockSpec((1, B_OUT_WIDTH), const),
            pl.BlockSpec((tm, tm), const),
            pl.BlockSpec((tm, tm), const),
            pl.BlockSpec((HSUM_COLS, HSUM_COLS), const),
        ],
        out_specs=[tok(D_MODEL)] + qkv_specs,
        out_shape=[jax.ShapeDtypeStruct((B, S, D_MODEL), F32)] + qkv_shapes,
        compiler_params=_params("arbitrary", "arbitrary"),
        name="dil_in",
    )(h, y, kv_norm.reshape(1, D_MODEL), b_norm.reshape(1, D_MODEL),
      kv_w[:, :B_WIDTH].astype(BF16), kv_w[:, B_WIDTH:].astype(BF16), w_q.astype(BF16),
      jnp.tile(k_gain, B_HEADS_PER_GROUP).reshape(1, B_OUT_WIDTH),
      jnp.tile(q_gain * (SCALE * LOG2E), B_HEADS_PER_GROUP).reshape(1, B_OUT_WIDTH),
      _residue_perm(tm, B_DILATIONS[1]), _residue_perm(tm, B_DILATIONS[2]), _head_sum_matrix())
    return outs[0], [outs[1 + 3 * g:4 + 3 * g] for g in range(B_GROUPS)]


def _t5_bucket(dist):
    max_exact = NUM_BUCKETS // 2
    d_f = jnp.maximum(dist, max_exact).astype(F32)
    large = max_exact + (jnp.log(d_f / max_exact) / math.log(MAX_DISTANCE / max_exact)
                         * (NUM_BUCKETS - max_exact)).astype(jnp.int32)
    large = jnp.minimum(large, NUM_BUCKETS - 1)
    return jnp.where(dist < max_exact, dist, large)


def _branch_bias(rel_bias, g, d):
    a = jnp.arange(BLOCK)[:, None]
    b = jnp.arange(2 * BLOCK)[None, :]
    n = BLOCK + a - b
    band = (n >= 0) & (n <= B_WINDOWS[g] // d)
    onehot = (_t5_bucket(jnp.maximum(n, 0) * d)[..., None] == jnp.arange(NUM_BUCKETS)).astype(F32)
    table = rel_bias[:, g * B_HEADS_PER_GROUP:(g + 1) * B_HEADS_PER_GROUP].astype(F32)
    bias = jnp.einsum("abk,kh->hab", onehot, table, precision=lax.Precision.HIGHEST) * LOG2E
    return jnp.where(band[None], bias, NEG)


DIL_SUB = 4
DIL_AHEAD = 3


def _dil_attn_kernel(q_ref, kp_ref, kc_ref, vp_ref, vc_ref, bias_ref, o_ref, lse_ref):
    n = pl.program_id(2)
    lane = lax.broadcasted_iota(jnp.int32, (1, LANES), 1)
    first = lane < HEAD_DIM
    col = lax.broadcasted_iota(jnp.int32, (1, 2 * BLOCK), 1)
    dead = (n == 0) & (col < BLOCK)
    pairs = B_HEADS_PER_GROUP // 2

    def band(cur_ref, prev_ref, i, j):
        sl = slice(j * LANES, (j + 1) * LANES)
        if i == 0:
            return jnp.concatenate([prev_ref[0, 0, :, sl], cur_ref[0, 0, :BLOCK, sl]], axis=0)
        return cur_ref[0, 0, (i - 1) * BLOCK:(i + 1) * BLOCK, sl]

    def scores(i, j, hh):
        q = pltpu.bitcast(q_ref[0, 0, i * BLOCK:(i + 1) * BLOCK, j * LANES:(j + 1) * LANES], jnp.int32)
        qm = pltpu.bitcast(jnp.where(first if hh == 0 else jnp.logical_not(first), q, 0), BF16)
        return _dot_nt(qm, band(kc_ref, kp_ref, i, j))

    items = [(i, j, hh) for i in range(q_ref.shape[2] // BLOCK) for j in range(pairs) for hh in range(2)]
    st, sb, mx, pr, dn, outs = {}, {}, {}, {}, {}, {}
    state = {"m_blk": None, "den_blk": None}

    def stage_max(t):
        i, j, hh = t
        bias = bias_ref[2 * j + hh]
        sb[t] = st.pop(t) + (jnp.where(dead, NEG, bias) if i == 0 else bias)
        mx[t] = jnp.max(sb[t], axis=-1, keepdims=True)

    def stage_exp(t):
        pr[t] = jnp.exp2(sb.pop(t) - mx[t])
        dn[t] = jnp.sum(pr[t], axis=-1, keepdims=True)

    def stage_out(t):
        i, j, hh = t
        m, den = mx.pop(t), dn.pop(t)
        outs[hh] = _dot(pr.pop(t).astype(BF16), band(vc_ref, vp_ref, i, j)) * (1.0 / den)
        if j == 0 and hh == 0:
            state["m_blk"] = jnp.zeros((BLOCK, LANES), F32)
            state["den_blk"] = jnp.ones((BLOCK, LANES), F32)
        state["m_blk"] = jnp.where(lane == 2 * j + hh, m, state["m_blk"])
        state["den_blk"] = jnp.where(lane == 2 * j + hh, den, state["den_blk"])
        if hh == 1:
            o_ref[0, 0, i * BLOCK:(i + 1) * BLOCK, j * LANES:(j + 1) * LANES] = (
                jnp.where(first, outs[0], outs[1]).astype(BF16))
            if j == pairs - 1:
                lse = jnp.where(lane < B_HEADS_PER_GROUP, state["m_blk"] + jnp.log2(state["den_blk"]), 0.0)
                hi, mid, lo = _split3(lse)
                lse_ref[0, 0, i * BLOCK:(i + 1) * BLOCK, :] = (
                    hi + pltpu.roll(mid, 8, 1) + pltpu.roll(lo, 16, 1)).astype(BF16)

    n_items = len(items)
    for step in range(n_items + DIL_AHEAD + 2):
        if step < n_items:
            st[items[step]] = scores(*items[step])
        if 0 <= step - DIL_AHEAD < n_items:
            stage_max(items[step - DIL_AHEAD])
        if 0 <= step - DIL_AHEAD - 1 < n_items:
            stage_exp(items[step - DIL_AHEAD - 1])
        if 0 <= step - DIL_AHEAD - 2 < n_items:
            stage_out(items[step - DIL_AHEAD - 2])


def _dil_attn(q, k, v, bias, g):
    B, d, L, w = q.shape
    sub = min(DIL_SUB, L // BLOCK)
    rows = sub * BLOCK
    cur = pl.BlockSpec((1, 1, rows, w), lambda b, r, n: (b, r, n, 0))
    prev = pl.BlockSpec((1, 1, BLOCK, w), lambda b, r, n: (b, r, jnp.maximum(n * sub - 1, 0), 0))
    return pl.pallas_call(
        _dil_attn_kernel,
        grid=(B, d, L // rows),
        in_specs=[cur, prev, cur, prev, cur,
                  pl.BlockSpec((B_HEADS_PER_GROUP, BLOCK, 2 * BLOCK), lambda b, r, n: (0, 0, 0))],
        out_specs=[cur, pl.BlockSpec((1, 1, rows, LANES), lambda b, r, n: (b, r, n, 0))],
        out_shape=[
            jax.ShapeDtypeStruct((B, d, L, w), BF16),
            jax.ShapeDtypeStruct((B, d, L, LANES), BF16),
        ],
        compiler_params=_params("arbitrary", "arbitrary", "arbitrary"),
        name=f"dil_attn_{g}",
    )(q, k, k, v, v, bias)


def _dil_out_kernel(o0_ref, o1_ref, o2_ref, l0_ref, l1_ref, l2_ref, h_ref, wo_ref, fg_ref,
                    wrh_ref, wrl_ref, br_ref, p1t_ref, p2t_ref, h_out_ref, xn_ref, route_ref):
    tm = h_ref.shape[1]

    def natural(ref, pt_ref):
        x = ref[0].reshape(tm, ref.shape[3])
        return x.astype(F32) if pt_ref is None else _dot(pt_ref[...], x)

    def lse_of(ref, pt_ref):
        x = natural(ref, pt_ref)
        return x + pltpu.roll(x, LANES - 8, 1) + pltpu.roll(x, LANES - 16, 1)

    o0, o1, o2 = natural(o0_ref, None), natural(o1_ref, p1t_ref), natural(o2_ref, p2t_ref)
    l0, l1, l2 = lse_of(l0_ref, None), lse_of(l1_ref, p1t_ref), lse_of(l2_ref, p2t_ref)
    m = jnp.maximum(jnp.maximum(l0, l1), l2)
    e0, e1, e2 = jnp.exp2(l0 - m), jnp.exp2(l1 - m), jnp.exp2(l2 - m)
    den = e0 + e1 + e2
    row = lax.broadcasted_iota(jnp.int32, (LANES, B_OUT_WIDTH), 0)
    col = lax.broadcasted_iota(jnp.int32, (LANES, B_OUT_WIDTH), 1)
    spread = (jnp.right_shift(col, 6) == row).astype(BF16)

    def widen(a):
        hi = a.astype(BF16)
        lo = (a - hi.astype(F32)).astype(BF16)
        return _dot(hi, spread) + _dot(lo, spread)

    merged = widen(e0 / den) * o0 + widen(e1 / den) * o1 + widen(e2 / den) * o2
    h = h_ref[0] + _dot(merged.astype(BF16), wo_ref[...])
    _ffn_in(h, fg_ref, wrh_ref, wrl_ref, br_ref, h_out_ref, xn_ref, route_ref)


def _dil_out(os_, lses, h, w_out, ffn_g, router):
    B, S, _ = h.shape
    tm = TOK_TILE
    wrh, wrl, br = router
    const = lambda b, t: (0, 0)
    tok = lambda w: pl.BlockSpec((1, tm, w), lambda b, t: (b, t, 0))
    res = lambda w: [pl.BlockSpec((1, d, tm // d, w), lambda b, t: (b, 0, t, 0)) for d in B_DILATIONS]
    out_specs, out_shape = _token_out_specs(B, S, tm)
    return pl.pallas_call(
        _dil_out_kernel,
        grid=(B, S // tm),
        in_specs=res(B_OUT_WIDTH) + res(LANES) + [
            tok(D_MODEL),
            pl.BlockSpec((B_OUT_WIDTH, D_MODEL), const),
            pl.BlockSpec((1, D_MODEL), const),
            pl.BlockSpec((D_MODEL, LANES), const),
            pl.BlockSpec((D_MODEL, LANES), const),
            pl.BlockSpec((1, LANES), const),
            pl.BlockSpec((tm, tm), const),
            pl.BlockSpec((tm, tm), const),
        ],
        out_specs=out_specs,
        out_shape=out_shape,
        compiler_params=_params("arbitrary", "arbitrary"),
        name="dil_out",
    )(*os_, *lses, h, w_out.astype(BF16), ffn_g.reshape(1, D_MODEL), wrh, wrl, br,
      _residue_perm(tm, B_DILATIONS[1]).T, _residue_perm(tm, B_DILATIONS[2]).T)


def _final_kernel(h_ref, y_ref, o_ref):
    o_ref[0] = h_ref[0] + _rows_from_slabs(y_ref)


def _final(h, y):
    B, S, _ = h.shape
    tm = TOK_TILE
    tok = lambda w: pl.BlockSpec((1, tm, w), lambda b, t: (b, t, 0))
    return pl.pallas_call(
        _final_kernel,
        grid=(B, S // tm),
        in_specs=[tok(D_MODEL), _moe_out_spec(S, tm)],
        out_specs=tok(D_MODEL),
        out_shape=jax.ShapeDtypeStruct((B, S, D_MODEL), F32),
        compiler_params=_params("arbitrary", "arbitrary"),
        name="moe_final",
    )(h, y)


def kernel(x, a_norm, a_w_in, a_b_f, a_q_gain, a_k_gain, a_w_out, kv_norm, kv_w, kv_k_gain, rel_bias, b_norm, b_w_q, b_q_gain, b_w_out, ffn_norm, moe_w_group, moe_b_group, moe_w_expert, moe_b_expert, moe_w_gate, moe_w_up, moe_w_down):
    routers = [_router_weights(moe_w_group[l], moe_b_group[l], moe_w_expert[l], moe_b_expert[l])
               for l in range(2)]
    q, k, vt = _fox_in(x, a_norm[0], a_w_in[0], a_b_f[0], a_q_gain[0], a_k_gain[0])
    ot = _fox_attn(q, k, vt)
    h1, xn1, route1 = _fox_out(ot, x, a_w_out[0], ffn_norm[0], routers[0])
    y1 = _moe(xn1, route1, moe_w_gate[0], moe_w_up[0], moe_w_down[0])
    h2, qkv = _dil_in(h1, y1, kv_norm, b_norm[0], kv_w, b_w_q[0], kv_k_gain, b_q_gain[0])
    outs, lses = [], []
    for g, d in enumerate(B_DILATIONS):
        o, lse = _dil_attn(*qkv[g], _branch_bias(rel_bias, g, d), g)
        outs.append(o)
        lses.append(lse)
    h3, xn3, route3 = _dil_out(outs, lses, h2, b_w_out[0], ffn_norm[1], routers[1])
    y3 = _moe(xn3, route3, moe_w_gate[1], moe_w_up[1], moe_w_down[1])
    return _final(h3, y3)
```

```python
import functools
import math

import jax
import jax.numpy as jnp
from jax import lax
from jax.experimental import pallas as pl
from jax.experimental.pallas import tpu as pltpu

F32 = jnp.float32
BF16 = jnp.bfloat16

D_MODEL = 1024
HEAD_DIM = 64
A_HEADS = 16
B_GROUPS = 3
B_HEADS_PER_GROUP = 8
B_HEADS = 24
B_WIDTH = B_HEADS * HEAD_DIM
B_OUT_WIDTH = B_HEADS_PER_GROUP * HEAD_DIM
B_WINDOWS = (128, 512, 2048)
B_DILATIONS = (1, 4, 16)
BLOCK = 128
NUM_BUCKETS = 32
MAX_DISTANCE = 2048
N_GROUPS = 4
EXPERTS_PER_GROUP = 4
N_EXPERTS = 16
TOP_K = 2
D_EXPERT = 512
EPS = 1e-6
NEG = -1e30
SCALE = HEAD_DIM ** -0.5
LOG2E = 1.4426950408889634

LANES = 128
TOK_TILE = 512
VMEM_LIMIT = 56 * 1024 * 1024


def _params(*sem):
    return pltpu.CompilerParams(dimension_semantics=sem, vmem_limit_bytes=VMEM_LIMIT)


def _rms(x, g):
    return x * lax.rsqrt(jnp.mean(x * x, axis=-1, keepdims=True) + EPS) * g


def _split3(x):
    hi = x.astype(BF16).astype(F32)
    r = x - hi
    mid = r.astype(BF16).astype(F32)
    return hi, mid, r - mid


SLAB = D_MODEL // LANES


def _rows_from_slabs(ref):
    return jnp.concatenate([ref[:, c, :] for c in range(SLAB)], axis=1)


def _rows_to_slabs(ref, x):
    for c in range(SLAB):
        ref[:, c, :] = x[:, c * LANES:(c + 1) * LANES]


def _dot(a, b):
    return jnp.dot(a, b, preferred_element_type=F32)


def _dot_nt(a, b):
    return lax.dot_general(a, b, (((1,), (1,)), ((), ())), preferred_element_type=F32)


def _dot_tn(a, b):
    return lax.dot_general(a, b, (((0,), (0,)), ((), ())), preferred_element_type=F32)


HSUM_COLS = 256


def _head_sum_matrix():
    i = jnp.arange(HSUM_COLS) // HEAD_DIM
    return (i[:, None] == i[None, :]).astype(BF16)


def _head_rms(p, hsum_ref):
    outs = []
    for j in range(p.shape[1] // HSUM_COLS):
        s = p[:, j * HSUM_COLS:(j + 1) * HSUM_COLS]
        ss = _dot((s * s).astype(BF16), hsum_ref[...])
        outs.append(s * lax.rsqrt(ss * (1.0 / HEAD_DIM) + EPS))
    return outs[0] if len(outs) == 1 else jnp.concatenate(outs, axis=1)


FOX_IN_COLS = 256
FOX_IN_AHEAD = 2


def _fox_in_kernel(x_ref, g_ref, wqk_ref, wvt_ref, wf_ref, bf_ref, qg_ref, kg_ref, sel_ref, hsum_ref,
                   q_ref, k_ref, vt_ref, carry_ref):
    tm = x_ref.shape[1]
    aw = A_HEADS * HEAD_DIM

    @pl.when(pl.program_id(1) == 0)
    def _():
        carry_ref[...] = jnp.zeros_like(carry_ref)

    xb = _rms(x_ref[0], g_ref[...]).astype(BF16)
    lane = lax.broadcasted_iota(jnp.int32, (1, LANES), 1)
    z = _dot(xb, wf_ref[...]) + bf_ref[...]
    lf = jnp.minimum(z, 0.0) - jnp.log1p(jnp.exp(-jnp.abs(z)))
    lf = jnp.where(lane < A_HEADS, lf, 0.0)
    row = lax.broadcasted_iota(jnp.int32, (tm, tm), 0)
    col = lax.broadcasted_iota(jnp.int32, (tm, tm), 1)
    tri = (col <= row).astype(BF16)
    hi, mid, lo = _split3(lf)
    cum = (_dot(tri, hi.astype(BF16)) + _dot(tri, mid.astype(BF16))
           + _dot(tri, lo.astype(BF16))) + carry_ref[...]
    carry_ref[...] = cum[tm - 1:tm, :]
    c_hi, c_mid, c_lo = _split3(cum * (-LOG2E))
    cc = (c_hi + pltpu.roll(c_mid, A_HEADS, 1) + pltpu.roll(c_lo, 2 * A_HEADS, 1)).astype(BF16)

    head_lane = lane < HEAD_DIM
    ones_vec = jnp.where((lane >= HEAD_DIM) & (lane < HEAD_DIM + 3), 1.0, 0.0)
    per = FOX_IN_COLS // HEAD_DIM
    n_tiles = 2 * aw // FOX_IN_COLS

    def project(n):
        return _dot(xb, wqk_ref[:, n * FOX_IN_COLS:(n + 1) * FOX_IN_COLS])

    def finish(n, p):
        is_k = n * FOX_IN_COLS >= aw
        h0 = (n * FOX_IN_COLS - (aw if is_k else 0)) // HEAD_DIM
        if is_k:
            aug = _dot(cc, sel_ref[:, h0 * LANES:(h0 + per) * LANES])
        pn = _head_rms(p, hsum_ref)
        for i in range(per):
            s = pn[:, (i // 2) * LANES:(i // 2 + 1) * LANES]
            if i % 2:
                s = pltpu.roll(s, HEAD_DIM, 1)
            dst = slice((h0 + i) * LANES, (h0 + i + 1) * LANES)
            if is_k:
                k_ref[0, :, dst] = jnp.where(head_lane, s * kg_ref[...], aug[:, i * LANES:(i + 1) * LANES]).astype(BF16)
            else:
                q_ref[0, :, dst] = jnp.where(head_lane, s * qg_ref[...], ones_vec).astype(BF16)

    pending = {n: project(n) for n in range(min(FOX_IN_AHEAD, n_tiles))}
    for n in range(n_tiles):
        if n + FOX_IN_AHEAD < n_tiles:
            pending[n + FOX_IN_AHEAD] = project(n + FOX_IN_AHEAD)
        elif n + FOX_IN_AHEAD == n_tiles:
            vt_ref[0, 0] = _dot_nt(wvt_ref[...], xb).astype(BF16)
        finish(n, pending.pop(n))


def _fox_in(x, a_norm, w_in, b_f, q_gain, k_gain):
    B, S, _ = x.shape
    tm = TOK_TILE
    nt = S // tm
    aw = A_HEADS * HEAD_DIM
    wqk = w_in[:, :2 * aw].astype(BF16)
    wvt = w_in[:, 2 * aw:3 * aw].T.astype(BF16)
    wf = jnp.pad(w_in[:, 3 * aw:], ((0, 0), (0, LANES - A_HEADS))).astype(BF16)
    bf = jnp.pad(b_f, (0, LANES - A_HEADS)).reshape(1, LANES)
    qg = jnp.pad(q_gain * (SCALE * LOG2E), (0, LANES - HEAD_DIM)).reshape(1, LANES)
    kg = jnp.pad(k_gain, (0, LANES - HEAD_DIM)).reshape(1, LANES)
    src = jnp.arange(LANES)
    dst = (src % A_HEADS) * LANES + HEAD_DIM + src // A_HEADS
    sel = ((dst[:, None] == jnp.arange(A_HEADS * LANES)[None, :]) & (src[:, None] < 3 * A_HEADS)).astype(BF16)
    const = lambda b, t: (0, 0)
    return pl.pallas_call(
        _fox_in_kernel,
        grid=(B, nt),
        in_specs=[
            pl.BlockSpec((1, tm, D_MODEL), lambda b, t: (b, t, 0)),
            pl.BlockSpec((1, D_MODEL), const),
            pl.BlockSpec((D_MODEL, 2 * aw), const),
            pl.BlockSpec((aw, D_MODEL), const),
            pl.BlockSpec((D_MODEL, LANES), const),
            pl.BlockSpec((1, LANES), const),
            pl.BlockSpec((1, LANES), const),
            pl.BlockSpec((1, LANES), const),
            pl.BlockSpec((LANES, A_HEADS * LANES), const),
            pl.BlockSpec((HSUM_COLS, HSUM_COLS), const),
        ],
        out_specs=[
            pl.BlockSpec((1, tm, A_HEADS * LANES), lambda b, t: (b, t, 0)),
            pl.BlockSpec((1, tm, A_HEADS * LANES), lambda b, t: (b, t, 0)),
            pl.BlockSpec((1, 1, aw, tm), lambda b, t: (b, t, 0, 0)),
        ],
        out_shape=[
            jax.ShapeDtypeStruct((B, S, A_HEADS * LANES), BF16),
            jax.ShapeDtypeStruct((B, S, A_HEADS * LANES), BF16),
            jax.ShapeDtypeStruct((B, nt, aw, tm), BF16),
        ],
        scratch_shapes=[pltpu.VMEM((1, LANES), F32)],
        compiler_params=_params("arbitrary", "arbitrary"),
        name="fox_in",
    )(x, a_norm.reshape(1, D_MODEL), wqk, wvt, wf, bf, qg, kg, sel, _head_sum_matrix())


FOX_TQ = 2048
FOX_QCHUNK = 256
FOX_UNROLL = 4
FOX_AHEAD = 4


def _fox_attn_kernel(q_ref, k_ref, vt_ref, o_ref, *scratch):
    tq = q_ref.shape[1]
    tk = vt_ref.shape[3]
    qc = FOX_QCHUNK
    nc = tq // qc
    ratio = tq // tk
    qi = pl.program_id(2)
    acc_refs, m_refs = scratch[:nc], scratch[nc:]
    for c in range(nc):
        m_refs[c][...] = jnp.full_like(m_refs[c], NEG)
        acc_refs[c][...] = jnp.zeros_like(acc_refs[c])
    ones = jnp.ones((16, tk), BF16)

    def scores(kj, c):
        return _dot_nt(kj, q_ref[0, c * qc:(c + 1) * qc, :])

    def update(st, vj, c, mask):
        if mask is not None:
            st = jnp.where(mask, st, NEG)
        m_prev = m_refs[c][...]
        m_new = jnp.maximum(m_prev, jnp.max(st, axis=0, keepdims=True))
        p = jnp.exp2((st - m_new).astype(BF16))
        alpha = jnp.exp2(m_prev - m_new)
        acc_refs[c][...] = alpha * acc_refs[c][...] + _dot(vj, p)
        m_refs[c][...] = m_new

    def load(j):
        kj = k_ref[0, pl.ds(pl.multiple_of(j * tk, tk), tk), :]
        vj = jnp.concatenate([vt_ref[0, j], ones], axis=0)
        return kj, vj

    def run(work):
        st = [None] * len(work)
        for i in range(min(FOX_AHEAD, len(work))):
            st[i] = scores(work[i][0], work[i][2])
        for i, (kj, vj, c, mask) in enumerate(work):
            if i + FOX_AHEAD < len(work):
                st[i + FOX_AHEAD] = scores(work[i + FOX_AHEAD][0], work[i + FOX_AHEAD][2])
            update(st[i], vj, c, mask)
            st[i] = None

    def body(jj, carry):
        work = []
        for u in range(FOX_UNROLL):
            kj, vj = load(jj * FOX_UNROLL + u)
            work += [(kj, vj, c, None) for c in range(nc)]
        run(work)
        return carry

    lax.fori_loop(0, qi * (ratio // FOX_UNROLL), body, 0)
    work = []
    for d in range(ratio):
        kj, vj = load(qi * ratio + d)
        for c in range(nc):
            if d * tk > (c + 1) * qc - 1:
                continue
            if (d + 1) * tk - 1 <= c * qc:
                work.append((kj, vj, c, None))
            else:
                kpos = d * tk + lax.broadcasted_iota(jnp.int32, (tk, qc), 0)
                qpos = c * qc + lax.broadcasted_iota(jnp.int32, (tk, qc), 1)
                work.append((kj, vj, c, kpos <= qpos))
    run(work)
    for c in range(nc):
        acc = acc_refs[c][...]
        o_ref[0, :, c * qc:(c + 1) * qc] = (acc[:HEAD_DIM] / acc[HEAD_DIM:HEAD_DIM + 1]).astype(BF16)


def _fox_attn(q, k, vt):
    B, S, _ = q.shape
    nt, tk = vt.shape[1], vt.shape[3]
    tq = FOX_TQ
    assert tq % (tk * FOX_UNROLL) == 0 and tq % FOX_QCHUNK == 0
    return pl.pallas_call(
        _fox_attn_kernel,
        grid=(B, A_HEADS, S // tq),
        in_specs=[
            pl.BlockSpec((1, tq, LANES), lambda b, h, i: (b, i, h)),
            pl.BlockSpec((1, S, LANES), lambda b, h, i: (b, 0, h)),
            pl.BlockSpec((1, nt, HEAD_DIM, tk), lambda b, h, i: (b, 0, h, 0)),
        ],
        out_specs=pl.BlockSpec((1, HEAD_DIM, tq), lambda b, h, i: (b, h, i)),
        out_shape=jax.ShapeDtypeStruct((B, A_HEADS * HEAD_DIM, S), BF16),
        scratch_shapes=([pltpu.VMEM((HEAD_DIM + 16, FOX_QCHUNK), F32)] * (tq // FOX_QCHUNK)
                        + [pltpu.VMEM((1, FOX_QCHUNK), F32)] * (tq // FOX_QCHUNK)),
        compiler_params=_params("arbitrary", "arbitrary", "arbitrary"),
        name="fox_attn",
    )(q, k, vt)


def _route(xn, wrh_ref, wrl_ref, br_ref):
    xh = xn.astype(BF16)
    xl = (xn - xh.astype(F32)).astype(BF16)
    wh = wrh_ref[...]
    logits = _dot(xh, wh) + _dot(xl, wh) + _dot(xh, wrl_ref[...]) + br_ref[...]
    lane = lax.broadcasted_iota(jnp.int32, (1, LANES), 1)
    lanef = lane.astype(F32)
    far = float(LANES)

    gl = jnp.where(lane < N_GROUPS, logits, NEG)
    gm = jnp.max(gl, axis=-1, keepdims=True)
    g_val = 1.0 / jnp.sum(jnp.exp(gl - gm), axis=-1, keepdims=True)
    g_idx = jnp.min(jnp.where(gl == gm, lanef, far), axis=-1, keepdims=True)

    lo = N_GROUPS + EXPERTS_PER_GROUP * g_idx
    el = jnp.where((lanef >= lo) & (lanef < lo + EXPERTS_PER_GROUP), logits, NEG)
    em1 = jnp.max(el, axis=-1, keepdims=True)
    ez = jnp.sum(jnp.exp(el - em1), axis=-1, keepdims=True)
    i1 = jnp.min(jnp.where(el == em1, lanef, far), axis=-1, keepdims=True)
    el2 = jnp.where(lanef == i1, NEG, el)
    em2 = jnp.max(el2, axis=-1, keepdims=True)
    i2 = jnp.min(jnp.where(el2 == em2, lanef, far), axis=-1, keepdims=True)
    p1 = 1.0 / ez
    p2 = jnp.exp(em2 - em1) / ez
    den = p1 + p2
    gate1 = g_val * (p1 / den)
    gate2 = g_val * (p2 / den)
    return jnp.where(lane == 0, i1 - N_GROUPS,
                     jnp.where(lane == 1, i2 - N_GROUPS,
                               jnp.where(lane == 2, gate1,
                                         jnp.where(lane == 3, gate2, 0.0))))


def _router_weights(w_group, b_group, w_expert, b_expert):
    w = jnp.pad(jnp.concatenate([w_group, w_expert], axis=1),
                ((0, 0), (0, LANES - N_GROUPS - N_EXPERTS)))
    b = jnp.pad(jnp.concatenate([b_group, b_expert]), (0, LANES - N_GROUPS - N_EXPERTS))
    wh = w.astype(BF16)
    wl = (w - wh.astype(F32)).astype(BF16)
    return wh, wl, b.reshape(1, LANES)


def _ffn_in(h, fg_ref, wrh_ref, wrl_ref, br_ref, h_ref, xn_hbm, route_ref, xbuf, sem):
    tm = h.shape[0]
    groups = tm // 8
    b, t = pl.program_id(0), pl.program_id(1)
    step = b * pl.num_programs(1) + t
    last = pl.num_programs(0) * pl.num_programs(1) - 1
    slot = lax.rem(step, 2)

    def copies(s, bb, tt):
        return [pltpu.make_async_copy(xbuf.at[s, :, c], xn_hbm.at[bb, pl.ds(tt * groups, groups), :, c, :], sem.at[s])
                for c in range(SLAB)]

    def wait_slot(s):
        pltpu.make_async_copy(xbuf.at[s], xbuf.at[s], sem.at[s]).wait()

    h_ref[0] = h
    xn = _rms(h, fg_ref[...])

    @pl.when(step >= 2)
    def _():
        wait_slot(slot)

    for c in range(SLAB):
        xbuf[slot, :, c] = xn[:, c * LANES:(c + 1) * LANES].reshape(groups, 8, LANES)
    for cp in copies(slot, b, t):
        cp.start()
    route_ref[0] = _route(xn, wrh_ref, wrl_ref, br_ref)

    @pl.when(step == last)
    def _():
        @pl.when(step >= 1)
        def _():
            wait_slot(1 - slot)
        wait_slot(slot)


def _fox_out_kernel(ot_ref, x_ref, wo_ref, fg_ref, wrh_ref, wrl_ref, br_ref,
                    h_ref, xn_hbm, route_ref, xbuf, sem):
    h = x_ref[0] + _dot(ot_ref[0].T, wo_ref[...])
    _ffn_in(h, fg_ref, wrh_ref, wrl_ref, br_ref, h_ref, xn_hbm, route_ref, xbuf, sem)


def _token_out_specs(B, S, tm):
    specs = [
        pl.BlockSpec((1, tm, D_MODEL), lambda b, t: (b, t, 0)),
        pl.BlockSpec(memory_space=pl.ANY),
        pl.BlockSpec((1, tm, LANES), lambda b, t: (b, t, 0)),
    ]
    shapes = [
        jax.ShapeDtypeStruct((B, S, D_MODEL), F32),
        jax.ShapeDtypeStruct((B, S // 8, 8, SLAB, LANES), F32),
        jax.ShapeDtypeStruct((B, S, LANES), F32),
    ]
    scratch = [pltpu.VMEM((2, tm // 8, SLAB, 8, LANES), F32), pltpu.SemaphoreType.DMA((2,))]
    return specs, shapes, scratch


def _fox_out(ot, x, w_out, ffn_g, router):
    B, S, _ = x.shape
    tm = TOK_TILE
    wrh, wrl, br = router
    const = lambda b, t: (0, 0)
    out_specs, out_shape, scratch = _token_out_specs(B, S, tm)
    return pl.pallas_call(
        _fox_out_kernel,
        grid=(B, S // tm),
        in_specs=[
            pl.BlockSpec((1, A_HEADS * HEAD_DIM, tm), lambda b, t: (b, 0, t)),
            pl.BlockSpec((1, tm, D_MODEL), lambda b, t: (b, t, 0)),
            pl.BlockSpec((A_HEADS * HEAD_DIM, D_MODEL), const),
            pl.BlockSpec((1, D_MODEL), const),
            pl.BlockSpec((D_MODEL, LANES), const),
            pl.BlockSpec((D_MODEL, LANES), const),
            pl.BlockSpec((1, LANES), const),
        ],
        out_specs=out_specs,
        out_shape=out_shape,
        scratch_shapes=scratch,
        compiler_params=_params("arbitrary", "arbitrary"),
        name="fox_out",
    )(ot, x, w_out.astype(BF16), ffn_g.reshape(1, D_MODEL), wrh, wrl, br)


MOE_TILE = 256
PAIRS_PER_GROUP = EXPERTS_PER_GROUP * (EXPERTS_PER_GROUP - 1) // 2
N_CLASSES = N_GROUPS * PAIRS_PER_GROUP
_PAIR_LO = [a for a in range(EXPERTS_PER_GROUP) for b in range(a + 1, EXPERTS_PER_GROUP)]
_PAIR_HI = [b for a in range(EXPERTS_PER_GROUP) for b in range(a + 1, EXPERTS_PER_GROUP)]


def _moe_plan(route, n_tok):
    tm = MOE_TILE
    n_tiles = n_tok // tm + N_CLASSES
    e1, e2 = route[:, 0].astype(jnp.int32), route[:, 1].astype(jnp.int32)
    swap = e2 < e1
    lo, hi = jnp.where(swap, e2, e1), jnp.where(swap, e1, e2)
    gates = jnp.stack([jnp.where(swap, route[:, 3], route[:, 2]),
                       jnp.where(swap, route[:, 2], route[:, 3])], axis=1)
    la, lb = lo % EXPERTS_PER_GROUP, hi % EXPERTS_PER_GROUP
    pair = (la * (2 * EXPERTS_PER_GROUP - 1 - la)) // 2 + (lb - la - 1)
    cls = (lo // EXPERTS_PER_GROUP) * PAIRS_PER_GROUP + pair
    counts = jnp.sum(cls[:, None] == jnp.arange(N_CLASSES)[None, :], axis=0).astype(jnp.int32)
    tiles_per = (counts + tm - 1) // tm
    tile_end = jnp.cumsum(tiles_per)
    n_used = tile_end[-1]
    tid = jnp.arange(n_tiles, dtype=jnp.int32)
    tid_c = jnp.minimum(tid, n_used - 1)
    tile_c = jnp.minimum(jnp.sum(tile_end[None, :] <= tid_c[:, None], axis=1), N_CLASSES - 1).astype(jnp.int32)
    base = (tile_c // PAIRS_PER_GROUP) * EXPERTS_PER_GROUP
    tile_a = (base + jnp.array(_PAIR_LO, jnp.int32)[tile_c % PAIRS_PER_GROUP]).astype(jnp.int32)
    tile_b = (base + jnp.array(_PAIR_HI, jnp.int32)[tile_c % PAIRS_PER_GROUP]).astype(jnp.int32)
    within = (tid_c - (tile_end - tiles_per)[tile_c]) * tm
    valid = jnp.where(tid < n_used, jnp.clip(counts[tile_c] - within, 0, tm), 0).astype(jnp.int32)
    j = jnp.arange(tm, dtype=jnp.int32)[None, :]
    unused = 2 * N_CLASSES + 1
    pad_key = jnp.where(j < (tiles_per * tm - counts)[:, None],
                        2 * jnp.arange(N_CLASSES, dtype=jnp.int32)[:, None] + 1, unused).reshape(-1)
    n_pad = N_CLASSES * tm
    keys = jnp.concatenate([2 * cls, pad_key])
    vals = jnp.concatenate([jnp.arange(n_tok, dtype=jnp.int32), jnp.zeros((n_pad,), jnp.int32)])
    g_lo = jnp.concatenate([gates[:, 0], jnp.zeros((n_pad,), F32)])
    g_hi = jnp.concatenate([gates[:, 1], jnp.zeros((n_pad,), F32)])
    keys, tok, g_lo, g_hi = lax.sort((keys, vals, g_lo, g_hi), num_keys=1, is_stable=True)
    live = (keys % 2 == 0).reshape(n_tiles, tm)
    tok = tok.reshape(n_tiles, tm)
    dst = jnp.where(live, tok, n_tok + (tid % 2)[:, None] * tm + j).astype(jnp.int32)
    return tile_a, tile_b, valid, tok, dst, jnp.stack([g_lo, g_hi], axis=1)


def _moe_kernel(ta_ref, tb_ref, nv_ref, tok_hbm, dst_hbm, xn_hbm, gate_ref, wga_ref, wua_ref, wda_ref,
                wgb_ref, wub_ref, wdb_ref, out_hbm, tok_s, dst_s, xbuf, ybuf, isem, gsem, ssem):
    del ta_ref, tb_ref
    groups = xbuf.shape[1]
    tm = groups * 8
    i = pl.program_id(0)
    n_tiles = pl.num_programs(0)
    slot = lax.rem(i, 2)
    other = 1 - slot

    def idx_copies(t, s):
        rows = pl.ds(s * tm, tm)
        return (pltpu.make_async_copy(tok_hbm.at[t], tok_s.at[rows], isem.at[0, s]),
                pltpu.make_async_copy(dst_hbm.at[t], dst_s.at[rows], isem.at[1, s]))

    def gather_start(s):
        def body(g, c):
            base = s * tm + g * 8
            for j in range(8):
                pltpu.make_async_copy(xn_hbm.at[tok_s[base + j]], xbuf.at[s, g, :, j, :],
                                      gsem.at[s]).start(priority=j % 2)
            return c
        lax.fori_loop(0, groups, body, 0)

    def scatter_start(s):
        def body(g, c):
            base = s * tm + g * 8
            for j in range(8):
                pltpu.make_async_copy(ybuf.at[s, g, :, j, :], out_hbm.at[dst_s[base + j]],
                                      ssem.at[s]).start(priority=j % 2)
            return c
        lax.fori_loop(0, groups, body, 0)

    def wait_rows(buf, sem, s):
        pltpu.make_async_copy(buf.at[s], buf.at[s], sem.at[s]).wait()

    @pl.when(i == 0)
    def _():
        for cp in idx_copies(0, 0):
            cp.start()
        for cp in idx_copies(0, 0):
            cp.wait()
        gather_start(0)
        for cp in idx_copies(1, 1):
            cp.start()

    @pl.when(i + 1 < n_tiles)
    def _():
        for cp in idx_copies(i + 1, other):
            cp.wait()
        gather_start(other)

    wait_rows(xbuf, gsem, slot)

    @pl.when(i >= 2)
    def _():
        wait_rows(ybuf, ssem, slot)

    @pl.when(nv_ref[i] > 0)
    def _():
        xb = jnp.concatenate([xbuf[slot, :, c].reshape(tm, LANES) for c in range(SLAB)], axis=1).astype(BF16)
        ga, ua = _dot(xb, wga_ref[0]), _dot(xb, wua_ref[0])
        gb, ub = _dot(xb, wgb_ref[0]), _dot(xb, wub_ref[0])
        ya = _dot((jax.nn.silu(ga) * ua).astype(BF16), wda_ref[0])
        yb = _dot((jax.nn.silu(gb) * ub).astype(BF16), wdb_ref[0])
        gates = gate_ref[...]
        y = gates[:, 0:1] * ya + gates[:, 1:2] * yb
        for c in range(SLAB):
            ybuf[slot, :, c] = y[:, c * LANES:(c + 1) * LANES].reshape(groups, 8, LANES)

    scatter_start(slot)

    @pl.when(i + 2 < n_tiles)
    def _():
        for cp in idx_copies(i + 2, slot):
            cp.start()

    @pl.when(i == n_tiles - 1)
    def _():
        wait_rows(ybuf, ssem, other)
        wait_rows(ybuf, ssem, slot)


def _moe(xn, route, w_gate, w_up, w_down):
    n_tok = xn.shape[0] * xn.shape[1] * xn.shape[2]
    tm = MOE_TILE
    tile_a, tile_b, valid, tok, dst, row_gates = _moe_plan(route.reshape(n_tok, LANES), n_tok)
    n_tiles = tile_a.shape[0]
    up = lambda which: pl.BlockSpec((1, D_MODEL, D_EXPERT), lambda i, ta, tb, nv: ((ta, tb)[which][i], 0, 0))
    down = lambda which: pl.BlockSpec((1, D_EXPERT, D_MODEL), lambda i, ta, tb, nv: ((ta, tb)[which][i], 0, 0))
    grid_spec = pltpu.PrefetchScalarGridSpec(
        num_scalar_prefetch=3,
        grid=(n_tiles,),
        in_specs=[
            pl.BlockSpec(memory_space=pl.ANY),
            pl.BlockSpec(memory_space=pl.ANY),
            pl.BlockSpec(memory_space=pl.ANY),
            pl.BlockSpec((tm, TOP_K), lambda i, ta, tb, nv: (i, 0)),
            up(0), up(0), down(0), up(1), up(1), down(1),
        ],
        out_specs=pl.BlockSpec(memory_space=pl.ANY),
        scratch_shapes=[
            pltpu.SMEM((2 * tm,), jnp.int32),
            pltpu.SMEM((2 * tm,), jnp.int32),
            pltpu.VMEM((2, tm // 8, SLAB, 8, LANES), F32),
            pltpu.VMEM((2, tm // 8, SLAB, 8, LANES), F32),
            pltpu.SemaphoreType.DMA((2, 2)),
            pltpu.SemaphoreType.DMA((2,)),
            pltpu.SemaphoreType.DMA((2,)),
        ],
    )
    wg, wu, wd = w_gate.astype(BF16), w_up.astype(BF16), w_down.astype(BF16)
    return pl.pallas_call(
        _moe_kernel,
        grid_spec=grid_spec,
        out_shape=jax.ShapeDtypeStruct((n_tok + 2 * tm, SLAB, LANES), F32),
        compiler_params=_params("arbitrary"),
        name="moe_experts",
    )(tile_a, tile_b, valid, tok, dst, xn.reshape(n_tok, SLAB, LANES), row_gates, wg, wu, wd, wg, wu, wd)


def _moe_out_spec(S, tm):
    return pl.BlockSpec((tm, SLAB, LANES), lambda b, t: (b * (S // tm) + t, 0, 0))


def _residue_perm(tm, d):
    i = jnp.arange(tm)
    src = (i % (tm // d)) * d + i // (tm // d)
    return (src[:, None] == jnp.arange(tm)[None, :]).astype(BF16)


def _dil_in_kernel(h_ref, y_ref, kvg_ref, bg_ref, wk_ref, wv_ref, wq_ref, kgain_ref,
                   qgain_ref, p1_ref, p2_ref, hsum_ref, h2_ref, *qkv_refs):
    tm = h_ref.shape[1]
    w = B_OUT_WIDTH
    h2 = h_ref[0] + _rows_from_slabs(y_ref)
    h2_ref[0] = h2
    u = h2 * lax.rsqrt(jnp.mean(h2 * h2, axis=-1, keepdims=True) + EPS)
    xkv = (u * kvg_ref[...]).astype(BF16)
    xq = (u * bg_ref[...]).astype(BF16)

    def project(g):
        cols = slice(g * w, (g + 1) * w)
        return _dot(xq, wq_ref[:, cols]), _dot(xkv, wk_ref[:, cols]), _dot(xkv, wv_ref[:, cols])

    def finish(g, d, qkv):
        q, k, v = qkv
        cat = jnp.concatenate([(_head_rms(q, hsum_ref) * qgain_ref[...]).astype(BF16),
                               (_head_rms(k, hsum_ref) * kgain_ref[...]).astype(BF16),
                               v.astype(BF16)], axis=1)
        if d > 1:
            cat = _dot((p1_ref if g == 1 else p2_ref)[...], cat).astype(BF16)
        rows = tm // d
        for t in range(3):
            ref = qkv_refs[3 * g + t]
            for r in range(d):
                ref[0, r] = cat[r * rows:(r + 1) * rows, t * w:(t + 1) * w]

    nxt = project(0)
    for g, d in enumerate(B_DILATIONS):
        cur = nxt
        if g + 1 < B_GROUPS:
            nxt = project(g + 1)
        finish(g, d, cur)


def _dil_in(h, y, kv_norm, b_norm, kv_w, w_q, k_gain, q_gain):
    B, S, _ = h.shape
    tm = TOK_TILE
    const = lambda b, t: (0, 0)
    tok = lambda w: pl.BlockSpec((1, tm, w), lambda b, t: (b, t, 0))
    qkv_specs, qkv_shapes = [], []
    for d in B_DILATIONS:
        qkv_specs += [pl.BlockSpec((1, d, tm // d, B_OUT_WIDTH), lambda b, t: (b, 0, t, 0))] * 3
        qkv_shapes += [jax.ShapeDtypeStruct((B, d, S // d, B_OUT_WIDTH), BF16)] * 3
    outs = pl.pallas_call(
        _dil_in_kernel,
        grid=(B, S // tm),
        in_specs=[
            tok(D_MODEL),
            _moe_out_spec(S, tm),
            pl.BlockSpec((1, D_MODEL), const),
            pl.BlockSpec((1, D_MODEL), const),
            pl.BlockSpec((D_MODEL, B_WIDTH), const),
            pl.BlockSpec((D_MODEL, B_WIDTH), const),
            pl.BlockSpec((D_MODEL, B_WIDTH), const),
            pl.BlockSpec((1, B_OUT_WIDTH), const),
            pl.BlockSpec((1, B_OUT_WIDTH), const),
            pl.BlockSpec((tm, tm), const),
            pl.BlockSpec((tm, tm), const),
            pl.BlockSpec((HSUM_COLS, HSUM_COLS), const),
        ],
        out_specs=[tok(D_MODEL)] + qkv_specs,
        out_shape=[jax.ShapeDtypeStruct((B, S, D_MODEL), F32)] + qkv_shapes,
        compiler_params=_params("arbitrary", "arbitrary"),
        name="dil_in",
    )(h, y, kv_norm.reshape(1, D_MODEL), b_norm.reshape(1, D_MODEL),
      kv_w[:, :B_WIDTH].astype(BF16), kv_w[:, B_WIDTH:].astype(BF16), w_q.astype(BF16),
      jnp.tile(k_gain, B_HEADS_PER_GROUP).reshape(1, B_OUT_WIDTH),
      jnp.tile(q_gain * (SCALE * LOG2E), B_HEADS_PER_GROUP).reshape(1, B_OUT_WIDTH),
      _residue_perm(tm, B_DILATIONS[1]), _residue_perm(tm, B_DILATIONS[2]), _head_sum_matrix())
    return outs[0], [outs[1 + 3 * g:4 + 3 * g] for g in range(B_GROUPS)]


def _t5_bucket(dist):
    max_exact = NUM_BUCKETS // 2
    d_f = jnp.maximum(dist, max_exact).astype(F32)
    large = max_exact + (jnp.log(d_f / max_exact) / math.log(MAX_DISTANCE / max_exact)
                         * (NUM_BUCKETS - max_exact)).astype(jnp.int32)
    large = jnp.minimum(large, NUM_BUCKETS - 1)
    return jnp.where(dist < max_exact, dist, large)


def _branch_bias(rel_bias, g, d):
    a = jnp.arange(BLOCK)[:, None]
    b = jnp.arange(2 * BLOCK)[None, :]
    n = BLOCK + a - b
    band = (n >= 0) & (n <= B_WINDOWS[g] // d)
    onehot = (_t5_bucket(jnp.maximum(n, 0) * d)[..., None] == jnp.arange(NUM_BUCKETS)).astype(F32)
    table = rel_bias[:, g * B_HEADS_PER_GROUP:(g + 1) * B_HEADS_PER_GROUP].astype(F32)
    bias = jnp.einsum("abk,kh->hab", onehot, table, precision=lax.Precision.HIGHEST) * LOG2E
    return jnp.where(band[None], bias, NEG)


DIL_SUB = 4
DIL_AHEAD = 3


def _dil_attn_kernel(q_ref, kp_ref, kc_ref, vp_ref, vc_ref, bias_ref, o_ref, lse_ref):
    n = pl.program_id(2)
    lane = lax.broadcasted_iota(jnp.int32, (1, LANES), 1)
    first = lane < HEAD_DIM
    col = lax.broadcasted_iota(jnp.int32, (1, 2 * BLOCK), 1)
    dead = (n == 0) & (col < BLOCK)
    pairs = B_HEADS_PER_GROUP // 2

    def band(cur_ref, prev_ref, i, j):
        sl = slice(j * LANES, (j + 1) * LANES)
        if i == 0:
            return jnp.concatenate([prev_ref[0, 0, :, sl], cur_ref[0, 0, :BLOCK, sl]], axis=0)
        return cur_ref[0, 0, (i - 1) * BLOCK:(i + 1) * BLOCK, sl]

    def scores(i, j, hh):
        q = pltpu.bitcast(q_ref[0, 0, i * BLOCK:(i + 1) * BLOCK, j * LANES:(j + 1) * LANES], jnp.int32)
        qm = pltpu.bitcast(jnp.where(first if hh == 0 else jnp.logical_not(first), q, 0), BF16)
        return _dot_nt(qm, band(kc_ref, kp_ref, i, j))

    items = [(i, j, hh) for i in range(q_ref.shape[2] // BLOCK) for j in range(pairs) for hh in range(2)]
    st, sb, mx, pr, dn, outs = {}, {}, {}, {}, {}, {}
    state = {"m_blk": None, "den_blk": None}

    def stage_max(t):
        i, j, hh = t
        bias = bias_ref[2 * j + hh]
        sb[t] = st.pop(t) + (jnp.where(dead, NEG, bias) if i == 0 else bias)
        mx[t] = jnp.max(sb[t], axis=-1, keepdims=True)

    def stage_exp(t):
        pr[t] = jnp.exp2(sb.pop(t) - mx[t])
        dn[t] = jnp.sum(pr[t], axis=-1, keepdims=True)

    def stage_out(t):
        i, j, hh = t
        m, den = mx.pop(t), dn.pop(t)
        outs[hh] = _dot(pr.pop(t).astype(BF16), band(vc_ref, vp_ref, i, j)) * (1.0 / den)
        if j == 0 and hh == 0:
            state["m_blk"] = jnp.zeros((BLOCK, LANES), F32)
            state["den_blk"] = jnp.ones((BLOCK, LANES), F32)
        state["m_blk"] = jnp.where(lane == 2 * j + hh, m, state["m_blk"])
        state["den_blk"] = jnp.where(lane == 2 * j + hh, den, state["den_blk"])
        if hh == 1:
            o_ref[0, 0, i * BLOCK:(i + 1) * BLOCK, j * LANES:(j + 1) * LANES] = (
                jnp.where(first, outs[0], outs[1]).astype(BF16))
            if j == pairs - 1:
                lse = jnp.where(lane < B_HEADS_PER_GROUP, state["m_blk"] + jnp.log2(state["den_blk"]), 0.0)
                hi, mid, lo = _split3(lse)
                lse_ref[0, 0, i * BLOCK:(i + 1) * BLOCK, :] = (
                    hi + pltpu.roll(mid, 8, 1) + pltpu.roll(lo, 16, 1)).astype(BF16)

    n_items = len(items)
    for step in range(n_items + DIL_AHEAD + 2):
        if step < n_items:
            st[items[step]] = scores(*items[step])
        if 0 <= step - DIL_AHEAD < n_items:
            stage_max(items[step - DIL_AHEAD])
        if 0 <= step - DIL_AHEAD - 1 < n_items:
            stage_exp(items[step - DIL_AHEAD - 1])
        if 0 <= step - DIL_AHEAD - 2 < n_items:
            stage_out(items[step - DIL_AHEAD - 2])


def _dil_attn(q, k, v, bias, g):
    B, d, L, w = q.shape
    sub = min(DIL_SUB, L // BLOCK)
    rows = sub * BLOCK
    cur = pl.BlockSpec((1, 1, rows, w), lambda b, r, n: (b, r, n, 0))
    prev = pl.BlockSpec((1, 1, BLOCK, w), lambda b, r, n: (b, r, jnp.maximum(n * sub - 1, 0), 0))
    return pl.pallas_call(
        _dil_attn_kernel,
        grid=(B, d, L // rows),
        in_specs=[cur, prev, cur, prev, cur,
                  pl.BlockSpec((B_HEADS_PER_GROUP, BLOCK, 2 * BLOCK), lambda b, r, n: (0, 0, 0))],
        out_specs=[cur, pl.BlockSpec((1, 1, rows, LANES), lambda b, r, n: (b, r, n, 0))],
        out_shape=[
            jax.ShapeDtypeStruct((B, d, L, w), BF16),
            jax.ShapeDtypeStruct((B, d, L, LANES), BF16),
        ],
        compiler_params=_params("arbitrary", "arbitrary", "arbitrary"),
        name=f"dil_attn_{g}",
    )(q, k, k, v, v, bias)


def _dil_out_kernel(o0_ref, o1_ref, o2_ref, l0_ref, l1_ref, l2_ref, h_ref, wo_ref, fg_ref,
                    wrh_ref, wrl_ref, br_ref, p1t_ref, p2t_ref, h_out_ref, xn_hbm, route_ref, xbuf, sem):
    tm = h_ref.shape[1]

    def natural(ref, pt_ref):
        x = ref[0].reshape(tm, ref.shape[3])
        return x.astype(F32) if pt_ref is None else _dot(pt_ref[...], x)

    def lse_of(ref, pt_ref):
        x = natural(ref, pt_ref)
        return x + pltpu.roll(x, LANES - 8, 1) + pltpu.roll(x, LANES - 16, 1)

    o0, o1, o2 = natural(o0_ref, None), natural(o1_ref, p1t_ref), natural(o2_ref, p2t_ref)
    l0, l1, l2 = lse_of(l0_ref, None), lse_of(l1_ref, p1t_ref), lse_of(l2_ref, p2t_ref)
    m = jnp.maximum(jnp.maximum(l0, l1), l2)
    e0, e1, e2 = jnp.exp2(l0 - m), jnp.exp2(l1 - m), jnp.exp2(l2 - m)
    den = e0 + e1 + e2
    row = lax.broadcasted_iota(jnp.int32, (LANES, B_OUT_WIDTH), 0)
    col = lax.broadcasted_iota(jnp.int32, (LANES, B_OUT_WIDTH), 1)
    spread = (jnp.right_shift(col, 6) == row).astype(BF16)

    def widen(a):
        hi = a.astype(BF16)
        lo = (a - hi.astype(F32)).astype(BF16)
        return _dot(hi, spread) + _dot(lo, spread)

    merged = widen(e0 / den) * o0 + widen(e1 / den) * o1 + widen(e2 / den) * o2
    h = h_ref[0] + _dot(merged.astype(BF16), wo_ref[...])
    _ffn_in(h, fg_ref, wrh_ref, wrl_ref, br_ref, h_out_ref, xn_hbm, route_ref, xbuf, sem)


def _dil_out(os_, lses, h, w_out, ffn_g, router):
    B, S, _ = h.shape
    tm = TOK_TILE
    wrh, wrl, br = router
    const = lambda b, t: (0, 0)
    tok = lambda w: pl.BlockSpec((1, tm, w), lambda b, t: (b, t, 0))
    res = lambda w: [pl.BlockSpec((1, d, tm // d, w), lambda b, t: (b, 0, t, 0)) for d in B_DILATIONS]
    out_specs, out_shape, scratch = _token_out_specs(B, S, tm)
    return pl.pallas_call(
        _dil_out_kernel,
        grid=(B, S // tm),
        in_specs=res(B_OUT_WIDTH) + res(LANES) + [
            tok(D_MODEL),
            pl.BlockSpec((B_OUT_WIDTH, D_MODEL), const),
            pl.BlockSpec((1, D_MODEL), const),
            pl.BlockSpec((D_MODEL, LANES), const),
            pl.BlockSpec((D_MODEL, LANES), const),
            pl.BlockSpec((1, LANES), const),
            pl.BlockSpec((tm, tm), const),
            pl.BlockSpec((tm, tm), const),
        ],
        out_specs=out_specs,
        out_shape=out_shape,
        scratch_shapes=scratch,
        compiler_params=_params("arbitrary", "arbitrary"),
        name="dil_out",
    )(*os_, *lses, h, w_out.astype(BF16), ffn_g.reshape(1, D_MODEL), wrh, wrl, br,
      _residue_perm(tm, B_DILATIONS[1]).T, _residue_perm(tm, B_DILATIONS[2]).T)


def _final_kernel(h_ref, y_ref, o_ref):
    o_ref[0] = h_ref[0] + _rows_from_slabs(y_ref)


def _final(h, y):
    B, S, _ = h.shape
    tm = TOK_TILE
    tok = lambda w: pl.BlockSpec((1, tm, w), lambda b, t: (b, t, 0))
    return pl.pallas_call(
        _final_kernel,
        grid=(B, S // tm),
        in_specs=[tok(D_MODEL), _moe_out_spec(S, tm)],
        out_specs=tok(D_MODEL),
        out_shape=jax.ShapeDtypeStruct((B, S, D_MODEL), F32),
        compiler_params=_params("arbitrary", "arbitrary"),
        name="moe_final",
    )(h, y)


def kernel(x, a_norm, a_w_in, a_b_f, a_q_gain, a_k_gain, a_w_out, kv_norm, kv_w, kv_k_gain, rel_bias, b_norm, b_w_q, b_q_gain, b_w_out, ffn_norm, moe_w_group, moe_b_group, moe_w_expert, moe_b_expert, moe_w_gate, moe_w_up, moe_w_down):
    routers = [_router_weights(moe_w_group[l], moe_b_group[l], moe_w_expert[l], moe_b_expert[l])
               for l in range(2)]
    q, k, vt = _fox_in(x, a_norm[0], a_w_in[0], a_b_f[0], a_q_gain[0], a_k_gain[0])
    ot = _fox_attn(q, k, vt)
    h1, xn1, route1 = _fox_out(ot, x, a_w_out[0], ffn_norm[0], routers[0])
    y1 = _moe(xn1, route1, moe_w_gate[0], moe_w_up[0], moe_w_down[0])
    h2, qkv = _dil_in(h1, y1, kv_norm, b_norm[0], kv_w, b_w_q[0], kv_k_gain, b_q_gain[0])
    outs, lses = [], []
    for g, d in enumerate(B_DILATIONS):
        o, lse = _dil_attn(*qkv[g], _branch_bias(rel_bias, g, d), g)
        outs.append(o)
        lses.append(lse)
    h3, xn3, route3 = _dil_out(outs, lses, h2, b_w_out[0], ffn_norm[1], routers[1])
    y3 = _moe(xn3, route3, moe_w_gate[1], moe_w_up[1], moe_w_down[1])
    return _final(h3, y3)
```

```python
import functools
import math

import jax
import jax.numpy as jnp
from jax import lax
from jax.experimental import pallas as pl
from jax.experimental.pallas import tpu as pltpu

F32 = jnp.float32
BF16 = jnp.bfloat16

D_MODEL = 1024
HEAD_DIM = 64
A_HEADS = 16
B_GROUPS = 3
B_HEADS_PER_GROUP = 8
B_HEADS = 24
B_WIDTH = B_HEADS * HEAD_DIM
B_OUT_WIDTH = B_HEADS_PER_GROUP * HEAD_DIM
B_WINDOWS = (128, 512, 2048)
B_DILATIONS = (1, 4, 16)
BLOCK = 128
NUM_BUCKETS = 32
MAX_DISTANCE = 2048
N_GROUPS = 4
EXPERTS_PER_GROUP = 4
N_EXPERTS = 16
TOP_K = 2
D_EXPERT = 512
EPS = 1e-6
NEG = -1e30
SCALE = HEAD_DIM ** -0.5
LOG2E = 1.4426950408889634

LANES = 128
TOK_TILE = 512
VMEM_LIMIT = 56 * 1024 * 1024


def _params(*sem):
    return pltpu.CompilerParams(dimension_semantics=sem, vmem_limit_bytes=VMEM_LIMIT)


def _rms(x, g):
    return x * lax.rsqrt(jnp.mean(x * x, axis=-1, keepdims=True) + EPS) * g


def _split3(x):
    hi = x.astype(BF16).astype(F32)
    r = x - hi
    mid = r.astype(BF16).astype(F32)
    return hi, mid, r - mid


SLAB = D_MODEL // LANES


def _rows_from_slabs(ref):
    return jnp.concatenate([ref[:, c, :] for c in range(SLAB)], axis=1)


def _rows_to_slabs(ref, x):
    for c in range(SLAB):
        ref[:, c, :] = x[:, c * LANES:(c + 1) * LANES]


def _dot(a, b):
    return jnp.dot(a, b, preferred_element_type=F32)


def _dot_nt(a, b):
    return lax.dot_general(a, b, (((1,), (1,)), ((), ())), preferred_element_type=F32)


def _dot_tn(a, b):
    return lax.dot_general(a, b, (((0,), (0,)), ((), ())), preferred_element_type=F32)


HSUM_COLS = 256


def _head_sum_matrix():
    i = jnp.arange(HSUM_COLS) // HEAD_DIM
    return (i[:, None] == i[None, :]).astype(BF16)


def _head_rms(p, hsum_ref):
    outs = []
    for j in range(p.shape[1] // HSUM_COLS):
        s = p[:, j * HSUM_COLS:(j + 1) * HSUM_COLS]
        ss = _dot((s * s).astype(BF16), hsum_ref[...])
        outs.append(s * lax.rsqrt(ss * (1.0 / HEAD_DIM) + EPS))
    return outs[0] if len(outs) == 1 else jnp.concatenate(outs, axis=1)


FOX_IN_COLS = 256
FOX_IN_AHEAD = 2


def _fox_in_kernel(x_ref, g_ref, wqk_ref, wvt_ref, wf_ref, bf_ref, qg_ref, kg_ref, sel_ref, hsum_ref,
                   q_ref, k_ref, vt_ref, carry_ref):
    tm = x_ref.shape[1]
    aw = A_HEADS * HEAD_DIM

    @pl.when(pl.program_id(1) == 0)
    def _():
        carry_ref[...] = jnp.zeros_like(carry_ref)

    xb = _rms(x_ref[0], g_ref[...]).astype(BF16)
    lane = lax.broadcasted_iota(jnp.int32, (1, LANES), 1)
    z = _dot(xb, wf_ref[...]) + bf_ref[...]
    lf = jnp.minimum(z, 0.0) - jnp.log1p(jnp.exp(-jnp.abs(z)))
    lf = jnp.where(lane < A_HEADS, lf, 0.0)
    row = lax.broadcasted_iota(jnp.int32, (tm, tm), 0)
    col = lax.broadcasted_iota(jnp.int32, (tm, tm), 1)
    tri = (col <= row).astype(BF16)
    hi, mid, lo = _split3(lf)
    cum = (_dot(tri, hi.astype(BF16)) + _dot(tri, mid.astype(BF16))
           + _dot(tri, lo.astype(BF16))) + carry_ref[...]
    carry_ref[...] = cum[tm - 1:tm, :]
    c_hi, c_mid, c_lo = _split3(cum * (-LOG2E))
    cc = (c_hi + pltpu.roll(c_mid, A_HEADS, 1) + pltpu.roll(c_lo, 2 * A_HEADS, 1)).astype(BF16)

    head_lane = lane < HEAD_DIM
    ones_vec = jnp.where((lane >= HEAD_DIM) & (lane < HEAD_DIM + 3), 1.0, 0.0)
    per = FOX_IN_COLS // HEAD_DIM
    n_tiles = 2 * aw // FOX_IN_COLS

    def project(n):
        return _dot(xb, wqk_ref[:, n * FOX_IN_COLS:(n + 1) * FOX_IN_COLS])

    def finish(n, p):
        is_k = n * FOX_IN_COLS >= aw
        h0 = (n * FOX_IN_COLS - (aw if is_k else 0)) // HEAD_DIM
        if is_k:
            aug = _dot(cc, sel_ref[:, h0 * LANES:(h0 + per) * LANES])
        pn = _head_rms(p, hsum_ref)
        for i in range(per):
            s = pn[:, (i // 2) * LANES:(i // 2 + 1) * LANES]
            if i % 2:
                s = pltpu.roll(s, HEAD_DIM, 1)
            dst = slice((h0 + i) * LANES, (h0 + i + 1) * LANES)
            if is_k:
                k_ref[0, :, dst] = jnp.where(head_lane, s * kg_ref[...], aug[:, i * LANES:(i + 1) * LANES]).astype(BF16)
            else:
                q_ref[0, :, dst] = jnp.where(head_lane, s * qg_ref[...], ones_vec).astype(BF16)

    pending = {n: project(n) for n in range(min(FOX_IN_AHEAD, n_tiles))}
    for n in range(n_tiles):
        if n + FOX_IN_AHEAD < n_tiles:
            pending[n + FOX_IN_AHEAD] = project(n + FOX_IN_AHEAD)
        elif n + FOX_IN_AHEAD == n_tiles:
            vt_ref[0, 0] = _dot_nt(wvt_ref[...], xb).astype(BF16)
        finish(n, pending.pop(n))


def _fox_in(x, a_norm, w_in, b_f, q_gain, k_gain):
    B, S, _ = x.shape
    tm = TOK_TILE
    nt = S // tm
    aw = A_HEADS * HEAD_DIM
    wqk = w_in[:, :2 * aw].astype(BF16)
    wvt = w_in[:, 2 * aw:3 * aw].T.astype(BF16)
    wf = jnp.pad(w_in[:, 3 * aw:], ((0, 0), (0, LANES - A_HEADS))).astype(BF16)
    bf = jnp.pad(b_f, (0, LANES - A_HEADS)).reshape(1, LANES)
    qg = jnp.pad(q_gain * (SCALE * LOG2E), (0, LANES - HEAD_DIM)).reshape(1, LANES)
    kg = jnp.pad(k_gain, (0, LANES - HEAD_DIM)).reshape(1, LANES)
    src = jnp.arange(LANES)
    dst = (src % A_HEADS) * LANES + HEAD_DIM + src // A_HEADS
    sel = ((dst[:, None] == jnp.arange(A_HEADS * LANES)[None, :]) & (src[:, None] < 3 * A_HEADS)).astype(BF16)
    const = lambda b, t: (0, 0)
    return pl.pallas_call(
        _fox_in_kernel,
        grid=(B, nt),
        in_specs=[
            pl.BlockSpec((1, tm, D_MODEL), lambda b, t: (b, t, 0)),
            pl.BlockSpec((1, D_MODEL), const),
            pl.BlockSpec((D_MODEL, 2 * aw), const),
            pl.BlockSpec((aw, D_MODEL), const),
            pl.BlockSpec((D_MODEL, LANES), const),
            pl.BlockSpec((1, LANES), const),
            pl.BlockSpec((1, LANES), const),
            pl.BlockSpec((1, LANES), const),
            pl.BlockSpec((LANES, A_HEADS * LANES), const),
            pl.BlockSpec((HSUM_COLS, HSUM_COLS), const),
        ],
        out_specs=[
            pl.BlockSpec((1, tm, A_HEADS * LANES), lambda b, t: (b, t, 0)),
            pl.BlockSpec((1, tm, A_HEADS * LANES), lambda b, t: (b, t, 0)),
            pl.BlockSpec((1, 1, aw, tm), lambda b, t: (b, t, 0, 0)),
        ],
        out_shape=[
            jax.ShapeDtypeStruct((B, S, A_HEADS * LANES), BF16),
            jax.ShapeDtypeStruct((B, S, A_HEADS * LANES), BF16),
            jax.ShapeDtypeStruct((B, nt, aw, tm), BF16),
        ],
        scratch_shapes=[pltpu.VMEM((1, LANES), F32)],
        compiler_params=_params("arbitrary", "arbitrary"),
        name="fox_in",
    )(x, a_norm.reshape(1, D_MODEL), wqk, wvt, wf, bf, qg, kg, sel, _head_sum_matrix())


FOX_TQ = 2048
FOX_QCHUNK = 256
FOX_UNROLL = 4
FOX_AHEAD = 8
FOX_KSUB = 256


def _fox_attn_kernel(q_ref, k_ref, vt_ref, o_ref, *scratch):
    tq = q_ref.shape[1]
    tk = vt_ref.shape[3]
    qc = FOX_QCHUNK
    nc = tq // qc
    ratio = tq // tk
    qi = pl.program_id(2)
    acc_refs, m_refs = scratch[:nc], scratch[nc:]
    for c in range(nc):
        m_refs[c][...] = jnp.full_like(m_refs[c], NEG)
        acc_refs[c][...] = jnp.zeros_like(acc_refs[c])

    def scores(kj, c):
        return _dot_nt(kj, q_ref[0, c * qc:(c + 1) * qc, :])

    def update(st, vj, c, mask):
        if mask is not None:
            st = jnp.where(mask, st, NEG)
        m_prev = m_refs[c][...]
        m_new = jnp.maximum(m_prev, jnp.max(st, axis=0, keepdims=True))
        p = jnp.exp2(st - m_new).astype(BF16)
        alpha = jnp.exp2(m_prev - m_new)
        acc_refs[c][...] = alpha * acc_refs[c][...] + _dot(vj, p)
        m_refs[c][...] = m_new

    ks = FOX_KSUB
    ones = jnp.ones((16, ks), BF16)

    def load(j, part):
        kj = k_ref[0, pl.ds(pl.multiple_of(j * tk, tk) + part * ks, ks), :]
        vj = jnp.concatenate([vt_ref[0, j, :, part * ks:(part + 1) * ks], ones], axis=0)
        return kj, vj

    def run(work):
        st = [None] * len(work)
        for i in range(min(FOX_AHEAD, len(work))):
            st[i] = scores(work[i][0], work[i][2])
        for i, (kj, vj, c, mask) in enumerate(work):
            if i + FOX_AHEAD < len(work):
                st[i + FOX_AHEAD] = scores(work[i + FOX_AHEAD][0], work[i + FOX_AHEAD][2])
            update(st[i], vj, c, mask)
            st[i] = None

    def body(jj, carry):
        work = []
        for u in range(FOX_UNROLL):
            for part in range(tk // ks):
                kj, vj = load(jj * FOX_UNROLL + u, part)
                work += [(kj, vj, c, None) for c in range(nc)]
        run(work)
        return carry

    lax.fori_loop(0, qi * (ratio // FOX_UNROLL), body, 0)
    work = []
    for d in range(ratio * (tk // ks)):
        kj, vj = load(qi * ratio + d // (tk // ks), d % (tk // ks))
        for c in range(nc):
            if d * ks > (c + 1) * qc - 1:
                continue
            if (d + 1) * ks - 1 <= c * qc:
                work.append((kj, vj, c, None))
            else:
                kpos = d * ks + lax.broadcasted_iota(jnp.int32, (ks, qc), 0)
                qpos = c * qc + lax.broadcasted_iota(jnp.int32, (ks, qc), 1)
                work.append((kj, vj, c, kpos <= qpos))
    run(work)
    for c in range(nc):
        acc = acc_refs[c][...]
        o_ref[0, :, c * qc:(c + 1) * qc] = (acc[:HEAD_DIM] / acc[HEAD_DIM:HEAD_DIM + 1]).astype(BF16)


def _fox_attn(q, k, vt):
    B, S, _ = q.shape
    nt, tk = vt.shape[1], vt.shape[3]
    tq = FOX_TQ
    assert tq % (tk * FOX_UNROLL) == 0 and tq % FOX_QCHUNK == 0
    return pl.pallas_call(
        _fox_attn_kernel,
        grid=(B, A_HEADS, S // tq),
        in_specs=[
            pl.BlockSpec((1, tq, LANES), lambda b, h, i: (b, i, h)),
            pl.BlockSpec((1, S, LANES), lambda b, h, i: (b, 0, h)),
            pl.BlockSpec((1, nt, HEAD_DIM, tk), lambda b, h, i: (b, 0, h, 0)),
        ],
        out_specs=pl.BlockSpec((1, HEAD_DIM, tq), lambda b, h, i: (b, h, i)),
        out_shape=jax.ShapeDtypeStruct((B, A_HEADS * HEAD_DIM, S), BF16),
        scratch_shapes=([pltpu.VMEM((HEAD_DIM + 16, FOX_QCHUNK), F32)] * (tq // FOX_QCHUNK)
                        + [pltpu.VMEM((1, FOX_QCHUNK), F32)] * (tq // FOX_QCHUNK)),
        compiler_params=_params("arbitrary", "arbitrary", "arbitrary"),
        name="fox_attn",
    )(q, k, vt)


def _route(xn, wrh_ref, wrl_ref, br_ref):
    xh = xn.astype(BF16)
    xl = (xn - xh.astype(F32)).astype(BF16)
    wh = wrh_ref[...]
    logits = _dot(xh, wh) + _dot(xl, wh) + _dot(xh, wrl_ref[...]) + br_ref[...]
    lane = lax.broadcasted_iota(jnp.int32, (1, LANES), 1)
    lanef = lane.astype(F32)
    far = float(LANES)

    gl = jnp.where(lane < N_GROUPS, logits, NEG)
    gm = jnp.max(gl, axis=-1, keepdims=True)
    g_val = 1.0 / jnp.sum(jnp.exp(gl - gm), axis=-1, keepdims=True)
    g_idx = jnp.min(jnp.where(gl == gm, lanef, far), axis=-1, keepdims=True)

    lo = N_GROUPS + EXPERTS_PER_GROUP * g_idx
    el = jnp.where((lanef >= lo) & (lanef < lo + EXPERTS_PER_GROUP), logits, NEG)
    em1 = jnp.max(el, axis=-1, keepdims=True)
    ez = jnp.sum(jnp.exp(el - em1), axis=-1, keepdims=True)
    i1 = jnp.min(jnp.where(el == em1, lanef, far), axis=-1, keepdims=True)
    el2 = jnp.where(lanef == i1, NEG, el)
    em2 = jnp.max(el2, axis=-1, keepdims=True)
    i2 = jnp.min(jnp.where(el2 == em2, lanef, far), axis=-1, keepdims=True)
    p1 = 1.0 / ez
    p2 = jnp.exp(em2 - em1) / ez
    den = p1 + p2
    gate1 = g_val * (p1 / den)
    gate2 = g_val * (p2 / den)
    return jnp.where(lane == 0, i1 - N_GROUPS,
                     jnp.where(lane == 1, i2 - N_GROUPS,
                               jnp.where(lane == 2, gate1,
                                         jnp.where(lane == 3, gate2, 0.0))))


def _router_weights(w_group, b_group, w_expert, b_expert):
    w = jnp.pad(jnp.concatenate([w_group, w_expert], axis=1),
                ((0, 0), (0, LANES - N_GROUPS - N_EXPERTS)))
    b = jnp.pad(jnp.concatenate([b_group, b_expert]), (0, LANES - N_GROUPS - N_EXPERTS))
    wh = w.astype(BF16)
    wl = (w - wh.astype(F32)).astype(BF16)
    return wh, wl, b.reshape(1, LANES)


def _ffn_in(h, fg_ref, wrh_ref, wrl_ref, br_ref, h_ref, xn_hbm, route_ref, xbuf, sem):
    tm = h.shape[0]
    groups = tm // 8
    b, t = pl.program_id(0), pl.program_id(1)
    step = b * pl.num_programs(1) + t
    last = pl.num_programs(0) * pl.num_programs(1) - 1
    slot = lax.rem(step, 2)

    def copies(s, bb, tt):
        return [pltpu.make_async_copy(xbuf.at[s, :, c], xn_hbm.at[bb, pl.ds(tt * groups, groups), :, c, :], sem.at[s])
                for c in range(SLAB)]

    def wait_slot(s):
        pltpu.make_async_copy(xbuf.at[s], xbuf.at[s], sem.at[s]).wait()

    h_ref[0] = h
    xn = _rms(h, fg_ref[...])

    @pl.when(step >= 2)
    def _():
        wait_slot(slot)

    for c in range(SLAB):
        xbuf[slot, :, c] = xn[:, c * LANES:(c + 1) * LANES].reshape(groups, 8, LANES)
    for cp in copies(slot, b, t):
        cp.start()
    route_ref[0] = _route(xn, wrh_ref, wrl_ref, br_ref)

    @pl.when(step == last)
    def _():
        @pl.when(step >= 1)
        def _():
            wait_slot(1 - slot)
        wait_slot(slot)


def _fox_out_kernel(ot_ref, x_ref, wo_ref, fg_ref, wrh_ref, wrl_ref, br_ref,
                    h_ref, xn_hbm, route_ref, xbuf, sem):
    h = x_ref[0] + _dot(ot_ref[0].T, wo_ref[...])
    _ffn_in(h, fg_ref, wrh_ref, wrl_ref, br_ref, h_ref, xn_hbm, route_ref, xbuf, sem)


def _token_out_specs(B, S, tm):
    specs = [
        pl.BlockSpec((1, tm, D_MODEL), lambda b, t: (b, t, 0)),
        pl.BlockSpec(memory_space=pl.ANY),
        pl.BlockSpec((1, tm, LANES), lambda b, t: (b, t, 0)),
    ]
    shapes = [
        jax.ShapeDtypeStruct((B, S, D_MODEL), F32),
        jax.ShapeDtypeStruct((B, S // 8, 8, SLAB, LANES), F32),
        jax.ShapeDtypeStruct((B, S, LANES), F32),
    ]
    scratch = [pltpu.VMEM((2, tm // 8, SLAB, 8, LANES), F32), pltpu.SemaphoreType.DMA((2,))]
    return specs, shapes, scratch


def _fox_out(ot, x, w_out, ffn_g, router):
    B, S, _ = x.shape
    tm = TOK_TILE
    wrh, wrl, br = router
    const = lambda b, t: (0, 0)
    out_specs, out_shape, scratch = _token_out_specs(B, S, tm)
    return pl.pallas_call(
        _fox_out_kernel,
        grid=(B, S // tm),
        in_specs=[
            pl.BlockSpec((1, A_HEADS * HEAD_DIM, tm), lambda b, t: (b, 0, t)),
            pl.BlockSpec((1, tm, D_MODEL), lambda b, t: (b, t, 0)),
            pl.BlockSpec((A_HEADS * HEAD_DIM, D_MODEL), const),
            pl.BlockSpec((1, D_MODEL), const),
            pl.BlockSpec((D_MODEL, LANES), const),
            pl.BlockSpec((D_MODEL, LANES), const),
            pl.BlockSpec((1, LANES), const),
        ],
        out_specs=out_specs,
        out_shape=out_shape,
        scratch_shapes=scratch,
        compiler_params=_params("arbitrary", "arbitrary"),
        name="fox_out",
    )(ot, x, w_out.astype(BF16), ffn_g.reshape(1, D_MODEL), wrh, wrl, br)


MOE_TILE = 256
MOE_CHUNKS = 4
PAIRS_PER_GROUP = EXPERTS_PER_GROUP * (EXPERTS_PER_GROUP - 1) // 2
N_CLASSES = N_GROUPS * PAIRS_PER_GROUP
_PAIR_LO = [a for a in range(EXPERTS_PER_GROUP) for b in range(a + 1, EXPERTS_PER_GROUP)]
_PAIR_HI = [b for a in range(EXPERTS_PER_GROUP) for b in range(a + 1, EXPERTS_PER_GROUP)]


def _moe_plan(route, n_tok):
    tm = MOE_TILE
    n_tiles = n_tok // tm + N_CLASSES
    e1, e2 = route[:, 0].astype(jnp.int32), route[:, 1].astype(jnp.int32)
    swap = e2 < e1
    lo, hi = jnp.where(swap, e2, e1), jnp.where(swap, e1, e2)
    gates = jnp.stack([jnp.where(swap, route[:, 3], route[:, 2]),
                       jnp.where(swap, route[:, 2], route[:, 3])], axis=1)
    la, lb = lo % EXPERTS_PER_GROUP, hi % EXPERTS_PER_GROUP
    pair = (la * (2 * EXPERTS_PER_GROUP - 1 - la)) // 2 + (lb - la - 1)
    cls = (lo // EXPERTS_PER_GROUP) * PAIRS_PER_GROUP + pair
    counts = jnp.sum(cls[:, None] == jnp.arange(N_CLASSES)[None, :], axis=0).astype(jnp.int32)
    tiles_per = (counts + tm - 1) // tm
    tile_end = jnp.cumsum(tiles_per)
    n_used = tile_end[-1]
    tid = jnp.arange(n_tiles, dtype=jnp.int32)
    tid_c = jnp.minimum(tid, n_used - 1)
    tile_c = jnp.minimum(jnp.sum(tile_end[None, :] <= tid_c[:, None], axis=1), N_CLASSES - 1).astype(jnp.int32)
    base = (tile_c // PAIRS_PER_GROUP) * EXPERTS_PER_GROUP
    tile_a = (base + jnp.array(_PAIR_LO, jnp.int32)[tile_c % PAIRS_PER_GROUP]).astype(jnp.int32)
    tile_b = (base + jnp.array(_PAIR_HI, jnp.int32)[tile_c % PAIRS_PER_GROUP]).astype(jnp.int32)
    within = (tid_c - (tile_end - tiles_per)[tile_c]) * tm
    valid = jnp.where(tid < n_used, jnp.clip(counts[tile_c] - within, 0, tm), 0).astype(jnp.int32)
    j = jnp.arange(tm, dtype=jnp.int32)[None, :]
    unused = 2 * N_CLASSES + 1
    pad_key = jnp.where(j < (tiles_per * tm - counts)[:, None],
                        2 * jnp.arange(N_CLASSES, dtype=jnp.int32)[:, None] + 1, unused).reshape(-1)
    n_pad = N_CLASSES * tm
    keys = jnp.concatenate([2 * cls, pad_key])
    vals = jnp.concatenate([jnp.arange(n_tok, dtype=jnp.int32), jnp.zeros((n_pad,), jnp.int32)])
    g_lo = jnp.concatenate([gates[:, 0], jnp.zeros((n_pad,), F32)])
    g_hi = jnp.concatenate([gates[:, 1], jnp.zeros((n_pad,), F32)])
    keys, tok, g_lo, g_hi = lax.sort((keys, vals, g_lo, g_hi), num_keys=1, is_stable=True)
    live = (keys % 2 == 0).reshape(n_tiles, tm)
    tok = tok.reshape(n_tiles, tm)
    dst = jnp.where(live, tok, n_tok + (tid % 2)[:, None] * tm + j).astype(jnp.int32)
    return tile_a, tile_b, valid, tok, dst, jnp.stack([g_lo, g_hi], axis=1)


def _moe_kernel(ta_ref, tb_ref, nv_ref, tok_hbm, dst_hbm, xn_hbm, gate_ref, wga_ref, wua_ref, wda_ref,
                wgb_ref, wub_ref, wdb_ref, out_hbm, tok_s, dst_s, xbuf, ybuf, tsem, dsem, gsem, ssem):
    del ta_ref, tb_ref
    groups = xbuf.shape[1]
    tm = groups * 8
    i = pl.program_id(0)
    n_tiles = pl.num_programs(0)
    slot = lax.rem(i, 2)
    other = 1 - slot

    def table_copies(t):
        mod = (lambda a, n: a % n) if isinstance(t, int) else lax.rem
        ts, ds_ = mod(t, 2), mod(t + 1, 3)
        return (pltpu.make_async_copy(tok_hbm.at[t + 1], tok_s.at[pl.ds(ts * tm, tm)], tsem.at[ts]),
                pltpu.make_async_copy(dst_hbm.at[t + 1], dst_s.at[pl.ds(ds_ * tm, tm)], dsem.at[ds_]))

    def gather_row(s, base, g, j):
        pltpu.make_async_copy(xn_hbm.at[tok_s[base + j]], xbuf.at[s, g, :, j, :], gsem.at[s]).start(priority=j % 2)

    def scatter_row(s, base, g, j):
        pltpu.make_async_copy(ybuf.at[s, g, :, j, :], out_hbm.at[dst_s[base + j]], ssem.at[s]).start(priority=j % 2)

    def gather_loop(s):
        def body(g, c):
            for j in range(8):
                gather_row(s, s * tm + g * 8, g, j)
            return c
        lax.fori_loop(0, groups, body, 0)

    def scatter_loop(s, dslot):
        def body(g, c):
            for j in range(8):
                scatter_row(s, dslot * tm + g * 8, g, j)
            return c
        lax.fori_loop(0, groups, body, 0)

    def wait_rows(buf, sem, s):
        pltpu.make_async_copy(buf.at[s], buf.at[s], sem.at[s]).wait()

    @pl.when(i == 0)
    def _():
        for cp in table_copies(-1) + table_copies(0):
            cp.start()
        for cp in table_copies(-1) + table_copies(0):
            cp.wait()
        gather_loop(0)
        for cp in table_copies(1):
            cp.start()
        ybuf[1] = jnp.zeros(ybuf.shape[1:], F32)

    for cp in table_copies(i + 1):
        cp.wait()
    wait_rows(xbuf, gsem, slot)

    @pl.when(i >= 1)
    def _():
        wait_rows(ybuf, ssem, slot)

    prev_dst = lax.rem(i, 3)

    def neighbour_dmas(chunk):
        per = groups // MOE_CHUNKS
        for g in range(chunk * per, (chunk + 1) * per):
            for j in range(8):
                scatter_row(other, prev_dst * tm + g * 8, g, j)
                gather_row(other, other * tm + g * 8, g, j)

    @pl.when(nv_ref[i] > 0)
    def _():
        xb = jnp.concatenate([xbuf[slot, :, c].reshape(tm, LANES) for c in range(SLAB)], axis=1).astype(BF16)
        neighbour_dmas(0)
        ga, ua = _dot(xb, wga_ref[0]), _dot(xb, wua_ref[0])
        neighbour_dmas(1)
        gb, ub = _dot(xb, wgb_ref[0]), _dot(xb, wub_ref[0])
        neighbour_dmas(2)
        ya = _dot((jax.nn.silu(ga) * ua).astype(BF16), wda_ref[0])
        neighbour_dmas(3)
        yb = _dot((jax.nn.silu(gb) * ub).astype(BF16), wdb_ref[0])
        gates = gate_ref[...]
        y = gates[:, 0:1] * ya + gates[:, 1:2] * yb
        for c in range(SLAB):
            ybuf[slot, :, c] = y[:, c * LANES:(c + 1) * LANES].reshape(groups, 8, LANES)

    @pl.when(nv_ref[i] <= 0)
    def _():
        scatter_loop(other, prev_dst)
        gather_loop(other)

    @pl.when(i + 2 <= n_tiles)
    def _():
        for cp in table_copies(i + 2):
            cp.start()

    @pl.when(i == n_tiles - 1)
    def _():
        scatter_loop(slot, lax.rem(i + 1, 3))
        wait_rows(ybuf, ssem, other)
        wait_rows(ybuf, ssem, slot)
        wait_rows(xbuf, gsem, other)


def _moe(xn, route, w_gate, w_up, w_down):
    n_tok = xn.shape[0] * xn.shape[1] * xn.shape[2]
    tm = MOE_TILE
    tile_a, tile_b, valid, tok, dst, row_gates = _moe_plan(route.reshape(n_tok, LANES), n_tok)
    n_tiles = tile_a.shape[0]
    up = lambda which: pl.BlockSpec((1, D_MODEL, D_EXPERT), lambda i, ta, tb, nv: ((ta, tb)[which][i], 0, 0))
    down = lambda which: pl.BlockSpec((1, D_EXPERT, D_MODEL), lambda i, ta, tb, nv: ((ta, tb)[which][i], 0, 0))
    grid_spec = pltpu.PrefetchScalarGridSpec(
        num_scalar_prefetch=3,
        grid=(n_tiles,),
        in_specs=[
            pl.BlockSpec(memory_space=pl.ANY),
            pl.BlockSpec(memory_space=pl.ANY),
            pl.BlockSpec(memory_space=pl.ANY),
            pl.BlockSpec((tm, TOP_K), lambda i, ta, tb, nv: (i, 0)),
            up(0), up(0), down(0), up(1), up(1), down(1),
        ],
        out_specs=pl.BlockSpec(memory_space=pl.ANY),
        scratch_shapes=[
            pltpu.SMEM((2 * tm,), jnp.int32),
            pltpu.SMEM((3 * tm,), jnp.int32),
            pltpu.VMEM((2, tm // 8, SLAB, 8, LANES), F32),
            pltpu.VMEM((2, tm // 8, SLAB, 8, LANES), F32),
            pltpu.SemaphoreType.DMA((2,)),
            pltpu.SemaphoreType.DMA((3,)),
            pltpu.SemaphoreType.DMA((2,)),
            pltpu.SemaphoreType.DMA((2,)),
        ],
    )
    j = jnp.arange(tm, dtype=jnp.int32)[None, :]
    tok = jnp.concatenate([jnp.zeros((1, tm), jnp.int32), tok, jnp.zeros((1, tm), jnp.int32)])
    dst = jnp.concatenate([n_tok + tm + j, dst, n_tok + j])
    wg, wu, wd = w_gate.astype(BF16), w_up.astype(BF16), w_down.astype(BF16)
    return pl.pallas_call(
        _moe_kernel,
        grid_spec=grid_spec,
        out_shape=jax.ShapeDtypeStruct((n_tok + 2 * tm, SLAB, LANES), F32),
        compiler_params=_params("arbitrary"),
        name="moe_experts",
    )(tile_a, tile_b, valid, tok, dst, xn.reshape(n_tok, SLAB, LANES), row_gates, wg, wu, wd, wg, wu, wd)


def _moe_out_spec(S, tm):
    return pl.BlockSpec((tm, SLAB, LANES), lambda b, t: (b * (S // tm) + t, 0, 0))


def _residue_perm(tm, d):
    i = jnp.arange(tm)
    src = (i % (tm // d)) * d + i // (tm // d)
    return (src[:, None] == jnp.arange(tm)[None, :]).astype(BF16)


def _dil_in_kernel(h_ref, y_ref, kvg_ref, bg_ref, wk_ref, wv_ref, wq_ref, kgain_ref,
                   qgain_ref, p1_ref, p2_ref, hsum_ref, h2_ref, *qkv_refs):
    tm = h_ref.shape[1]
    w = B_OUT_WIDTH
    h2 = h_ref[0] + _rows_from_slabs(y_ref)
    h2_ref[0] = h2
    u = h2 * lax.rsqrt(jnp.mean(h2 * h2, axis=-1, keepdims=True) + EPS)
    xkv = (u * kvg_ref[...]).astype(BF16)
    xq = (u * bg_ref[...]).astype(BF16)

    def project(g):
        cols = slice(g * w, (g + 1) * w)
        return _dot(xq, wq_ref[:, cols]), _dot(xkv, wk_ref[:, cols]), _dot(xkv, wv_ref[:, cols])

    def finish(g, d, qkv):
        q, k, v = qkv
        cat = jnp.concatenate([(_head_rms(q, hsum_ref) * qgain_ref[...]).astype(BF16),
                               (_head_rms(k, hsum_ref) * kgain_ref[...]).astype(BF16),
                               v.astype(BF16)], axis=1)
        if d > 1:
            cat = _dot((p1_ref if g == 1 else p2_ref)[...], cat).astype(BF16)
        rows = tm // d
        for t in range(3):
            ref = qkv_refs[3 * g + t]
            for r in range(d):
                ref[0, r] = cat[r * rows:(r + 1) * rows, t * w:(t + 1) * w]

    nxt = project(0)
    for g, d in enumerate(B_DILATIONS):
        cur = nxt
        if g + 1 < B_GROUPS:
            nxt = project(g + 1)
        finish(g, d, cur)


def _dil_in(h, y, kv_norm, b_norm, kv_w, w_q, k_gain, q_gain):
    B, S, _ = h.shape
    tm = TOK_TILE
    const = lambda b, t: (0, 0)
    tok = lambda w: pl.BlockSpec((1, tm, w), lambda b, t: (b, t, 0))
    qkv_specs, qkv_shapes = [], []
    for d in B_DILATIONS:
        qkv_specs += [pl.BlockSpec((1, d, tm // d, B_OUT_WIDTH), lambda b, t: (b, 0, t, 0))] * 3
        qkv_shapes += [jax.ShapeDtypeStruct((B, d, S // d, B_OUT_WIDTH), BF16)] * 3
    outs = pl.pallas_call(
        _dil_in_kernel,
        grid=(B, S // tm),
        in_specs=[
            tok(D_MODEL),
            _moe_out_spec(S, tm),
            pl.BlockSpec((1, D_MODEL), const),
            pl.BlockSpec((1, D_MODEL), const),
            pl.BlockSpec((D_MODEL, B_WIDTH), const),
            pl.BlockSpec((D_MODEL, B_WIDTH), const),
            pl.BlockSpec((D_MODEL, B_WIDTH), const),
            pl.BlockSpec((1, B_OUT_WIDTH), const),
            pl.BlockSpec((1, B_OUT_WIDTH), const),
            pl.BlockSpec((tm, tm), const),
            pl.BlockSpec((tm, tm), const),
            pl.BlockSpec((HSUM_COLS, HSUM_COLS), const),
        ],
        out_specs=[tok(D_MODEL)] + qkv_specs,
        out_shape=[jax.ShapeDtypeStruct((B, S, D_MODEL), F32)] + qkv_shapes,
        compiler_params=_params("arbitrary", "arbitrary"),
        name="dil_in",
    )(h, y, kv_norm.reshape(1, D_MODEL), b_norm.reshape(1, D_MODEL),
      kv_w[:, :B_WIDTH].astype(BF16), kv_w[:, B_WIDTH:].astype(BF16), w_q.astype(BF16),
      jnp.tile(k_gain, B_HEADS_PER_GROUP).reshape(1, B_OUT_WIDTH),
      jnp.tile(q_gain * (SCALE * LOG2E), B_HEADS_PER_GROUP).reshape(1, B_OUT_WIDTH),
      _residue_perm(tm, B_DILATIONS[1]), _residue_perm(tm, B_DILATIONS[2]), _head_sum_matrix())
    return outs[0], [outs[1 + 3 * g:4 + 3 * g] for g in range(B_GROUPS)]


def _t5_bucket(dist):
    max_exact = NUM_BUCKETS // 2
    d_f = jnp.maximum(dist, max_exact).astype(F32)
    large = max_exact + (jnp.log(d_f / max_exact) / math.log(MAX_DISTANCE / max_exact)
                         * (NUM_BUCKETS - max_exact)).astype(jnp.int32)
    large = jnp.minimum(large, NUM_BUCKETS - 1)
    return jnp.where(dist < max_exact, dist, large)


def _branch_bias(rel_bias, g, d):
    a = jnp.arange(BLOCK)[:, None]
    b = jnp.arange(2 * BLOCK)[None, :]
    n = BLOCK + a - b
    band = (n >= 0) & (n <= B_WINDOWS[g] // d)
    bucket = _t5_bucket(jnp.maximum(n, 0) * d)[..., None]
    ids = jnp.arange(NUM_BUCKETS)
    onehot = ((bucket >= ids) & (bucket < ids + 1)).astype(F32)
    table = rel_bias[:, g * B_HEADS_PER_GROUP:(g + 1) * B_HEADS_PER_GROUP].astype(F32)
    bias = jnp.einsum("abk,kh->hab", onehot, table, precision=lax.Precision.HIGHEST) * LOG2E
    return jnp.where(band[None], bias, NEG)


DIL_SUB = 4
DIL_AHEAD = 3


def _dil_attn_kernel(q_ref, kp_ref, kc_ref, vp_ref, vc_ref, bias_ref, o_ref, lse_ref):
    n = pl.program_id(2)
    lane = lax.broadcasted_iota(jnp.int32, (1, LANES), 1)
    first = lane < HEAD_DIM
    col = lax.broadcasted_iota(jnp.int32, (1, 2 * BLOCK), 1)
    dead = (n == 0) & (col < BLOCK)
    pairs = B_HEADS_PER_GROUP // 2

    def band(cur_ref, prev_ref, i, j):
        sl = slice(j * LANES, (j + 1) * LANES)
        if i == 0:
            return jnp.concatenate([prev_ref[0, 0, :, sl], cur_ref[0, 0, :BLOCK, sl]], axis=0)
        return cur_ref[0, 0, (i - 1) * BLOCK:(i + 1) * BLOCK, sl]

    def scores(i, j, hh):
        q = q_ref[0, 0, i * BLOCK:(i + 1) * BLOCK, j * LANES:(j + 1) * LANES]
        keep = jnp.where(first if hh == 0 else jnp.logical_not(first), 1.0, 0.0).astype(BF16)
        return _dot_nt(q * keep, band(kc_ref, kp_ref, i, j))

    items = [(i, j, hh) for i in range(q_ref.shape[2] // BLOCK) for j in range(pairs) for hh in range(2)]
    st, sb, mx, pr, dn, outs = {}, {}, {}, {}, {}, {}
    state = {"m_blk": None, "den_blk": None}

    def stage_max(t):
        i, j, hh = t
        bias = bias_ref[2 * j + hh]
        sb[t] = st.pop(t) + (jnp.where(dead, NEG, bias) if i == 0 else bias)
        mx[t] = jnp.max(sb[t], axis=-1, keepdims=True)

    def stage_exp(t):
        pr[t] = jnp.exp2(sb.pop(t) - mx[t])
        dn[t] = jnp.sum(pr[t], axis=-1, keepdims=True)

    def stage_out(t):
        i, j, hh = t
        m, den = mx.pop(t), dn.pop(t)
        outs[hh] = _dot(pr.pop(t).astype(BF16), band(vc_ref, vp_ref, i, j)) * (1.0 / den)
        if j == 0 and hh == 0:
            state["m_blk"] = jnp.zeros((BLOCK, LANES), F32)
            state["den_blk"] = jnp.ones((BLOCK, LANES), F32)
        state["m_blk"] = jnp.where(lane == 2 * j + hh, m, state["m_blk"])
        state["den_blk"] = jnp.where(lane == 2 * j + hh, den, state["den_blk"])
        if hh == 1:
            o_ref[0, 0, i * BLOCK:(i + 1) * BLOCK, j * LANES:(j + 1) * LANES] = (
                jnp.where(first, outs[0], outs[1]).astype(BF16))
            if j == pairs - 1:
                lse = jnp.where(lane < B_HEADS_PER_GROUP, state["m_blk"] + jnp.log2(state["den_blk"]), 0.0)
                hi, mid, lo = _split3(lse)
                lse_ref[0, 0, i * BLOCK:(i + 1) * BLOCK, :] = (
                    hi + pltpu.roll(mid, 8, 1) + pltpu.roll(lo, 16, 1)).astype(BF16)

    n_items = len(items)
    for step in range(n_items + DIL_AHEAD + 2):
        if step < n_items:
            st[items[step]] = scores(*items[step])
        if 0 <= step - DIL_AHEAD < n_items:
            stage_max(items[step - DIL_AHEAD])
        if 0 <= step - DIL_AHEAD - 1 < n_items:
            stage_exp(items[step - DIL_AHEAD - 1])
        if 0 <= step - DIL_AHEAD - 2 < n_items:
            stage_out(items[step - DIL_AHEAD - 2])


def _dil_attn(q, k, v, bias, g):
    B, d, L, w = q.shape
    sub = min(DIL_SUB, L // BLOCK)
    rows = sub * BLOCK
    cur = pl.BlockSpec((1, 1, rows, w), lambda b, r, n: (b, r, n, 0))
    prev = pl.BlockSpec((1, 1, BLOCK, w), lambda b, r, n: (b, r, jnp.maximum(n * sub - 1, 0), 0))
    return pl.pallas_call(
        _dil_attn_kernel,
        grid=(B, d, L // rows),
        in_specs=[cur, prev, cur, prev, cur,
                  pl.BlockSpec((B_HEADS_PER_GROUP, BLOCK, 2 * BLOCK), lambda b, r, n: (0, 0, 0))],
        out_specs=[cur, pl.BlockSpec((1, 1, rows, LANES), lambda b, r, n: (b, r, n, 0))],
        out_shape=[
            jax.ShapeDtypeStruct((B, d, L, w), BF16),
            jax.ShapeDtypeStruct((B, d, L, LANES), BF16),
        ],
        compiler_params=_params("arbitrary", "arbitrary", "arbitrary"),
        name=f"dil_attn_{g}",
    )(q, k, k, v, v, bias)


def _dil_out_kernel(o0_ref, o1_ref, o2_ref, l0_ref, l1_ref, l2_ref, h_ref, wo_ref, fg_ref,
                    wrh_ref, wrl_ref, br_ref, p1t_ref, p2t_ref, h_out_ref, xn_hbm, route_ref, xbuf, sem):
    tm = h_ref.shape[1]

    def natural(ref, pt_ref):
        x = ref[0].reshape(tm, ref.shape[3])
        return x.astype(F32) if pt_ref is None else _dot(pt_ref[...], x)

    def lse_of(ref, pt_ref):
        x = natural(ref, pt_ref)
        return x + pltpu.roll(x, LANES - 8, 1) + pltpu.roll(x, LANES - 16, 1)

    o0, o1, o2 = natural(o0_ref, None), natural(o1_ref, p1t_ref), natural(o2_ref, p2t_ref)
    l0, l1, l2 = lse_of(l0_ref, None), lse_of(l1_ref, p1t_ref), lse_of(l2_ref, p2t_ref)
    m = jnp.maximum(jnp.maximum(l0, l1), l2)
    e0, e1, e2 = jnp.exp2(l0 - m), jnp.exp2(l1 - m), jnp.exp2(l2 - m)
    den = e0 + e1 + e2
    row = lax.broadcasted_iota(jnp.int32, (LANES, B_OUT_WIDTH), 0)
    col = lax.broadcasted_iota(jnp.int32, (LANES, B_OUT_WIDTH), 1)
    spread = (jnp.right_shift(col, 6) == row).astype(BF16)

    def widen(a):
        hi = a.astype(BF16)
        lo = (a - hi.astype(F32)).astype(BF16)
        return _dot(hi, spread) + _dot(lo, spread)

    merged = widen(e0 / den) * o0 + widen(e1 / den) * o1 + widen(e2 / den) * o2
    h = h_ref[0] + _dot(merged.astype(BF16), wo_ref[...])
    _ffn_in(h, fg_ref, wrh_ref, wrl_ref, br_ref, h_out_ref, xn_hbm, route_ref, xbuf, sem)


def _dil_out(os_, lses, h, w_out, ffn_g, router):
    B, S, _ = h.shape
    tm = TOK_TILE
    wrh, wrl, br = router
    const = lambda b, t: (0, 0)
    tok = lambda w: pl.BlockSpec((1, tm, w), lambda b, t: (b, t, 0))
    res = lambda w: [pl.BlockSpec((1, d, tm // d, w), lambda b, t: (b, 0, t, 0)) for d in B_DILATIONS]
    out_specs, out_shape, scratch = _token_out_specs(B, S, tm)
    return pl.pallas_call(
        _dil_out_kernel,
        grid=(B, S // tm),
        in_specs=res(B_OUT_WIDTH) + res(LANES) + [
            tok(D_MODEL),
            pl.BlockSpec((B_OUT_WIDTH, D_MODEL), const),
            pl.BlockSpec((1, D_MODEL), const),
            pl.BlockSpec((D_MODEL, LANES), const),
            pl.BlockSpec((D_MODEL, LANES), const),
            pl.BlockSpec((1, LANES), const),
            pl.BlockSpec((tm, tm), const),
            pl.BlockSpec((tm, tm), const),
        ],
        out_specs=out_specs,
        out_shape=out_shape,
        scratch_shapes=scratch,
        compiler_params=_params("arbitrary", "arbitrary"),
        name="dil_out",
    )(*os_, *lses, h, w_out.astype(BF16), ffn_g.reshape(1, D_MODEL), wrh, wrl, br,
      _residue_perm(tm, B_DILATIONS[1]).T, _residue_perm(tm, B_DILATIONS[2]).T)


def _final_kernel(h_ref, y_ref, o_ref):
    o_ref[0] = h_ref[0] + _rows_from_slabs(y_ref)


def _final(h, y):
    B, S, _ = h.shape
    tm = TOK_TILE
    tok = lambda w: pl.BlockSpec((1, tm, w), lambda b, t: (b, t, 0))
    return pl.pallas_call(
        _final_kernel,
        grid=(B, S // tm),
        in_specs=[tok(D_MODEL), _moe_out_spec(S, tm)],
        out_specs=tok(D_MODEL),
        out_shape=jax.ShapeDtypeStruct((B, S, D_MODEL), F32),
        compiler_params=_params("arbitrary", "arbitrary"),
        name="moe_final",
    )(h, y)


def kernel(x, a_norm, a_w_in, a_b_f, a_q_gain, a_k_gain, a_w_out, kv_norm, kv_w, kv_k_gain, rel_bias, b_norm, b_w_q, b_q_gain, b_w_out, ffn_norm, moe_w_group, moe_b_group, moe_w_expert, moe_b_expert, moe_w_gate, moe_w_up, moe_w_down):
    routers = [_router_weights(moe_w_group[l], moe_b_group[l], moe_w_expert[l], moe_b_expert[l])
               for l in range(2)]
    q, k, vt = _fox_in(x, a_norm[0], a_w_in[0], a_b_f[0], a_q_gain[0], a_k_gain[0])
    ot = _fox_attn(q, k, vt)
    h1, xn1, route1 = _fox_out(ot, x, a_w_out[0], ffn_norm[0], routers[0])
    y1 = _moe(xn1, route1, moe_w_gate[0], moe_w_up[0], moe_w_down[0])
    h2, qkv = _dil_in(h1, y1, kv_norm, b_norm[0], kv_w, b_w_q[0], kv_k_gain, b_q_gain[0])
    outs, lses = [], []
    for g, d in enumerate(B_DILATIONS):
        o, lse = _dil_attn(*qkv[g], _branch_bias(rel_bias, g, d), g)
        outs.append(o)
        lses.append(lse)
    h3, xn3, route3 = _dil_out(outs, lses, h2, b_w_out[0], ffn_norm[1], routers[1])
    y3 = _moe(xn3, route3, moe_w_gate[1], moe_w_up[1], moe_w_down[1])
    return _final(h3, y3)
```

```python
import functools
import math

import jax
import jax.numpy as jnp
from jax import lax
from jax.experimental import pallas as pl
from jax.experimental.pallas import tpu as pltpu

F32 = jnp.float32
BF16 = jnp.bfloat16

D_MODEL = 1024
HEAD_DIM = 64
A_HEADS = 16
B_GROUPS = 3
B_HEADS_PER_GROUP = 8
B_HEADS = 24
B_WIDTH = B_HEADS * HEAD_DIM
B_OUT_WIDTH = B_HEADS_PER_GROUP * HEAD_DIM
B_WINDOWS = (128, 512, 2048)
B_DILATIONS = (1, 4, 16)
BLOCK = 128
NUM_BUCKETS = 32
MAX_DISTANCE = 2048
N_GROUPS = 4
EXPERTS_PER_GROUP = 4
N_EXPERTS = 16
TOP_K = 2
D_EXPERT = 512
EPS = 1e-6
NEG = -1e30
SCALE = HEAD_DIM ** -0.5
LOG2E = 1.4426950408889634

LANES = 128
TOK_TILE = 512
VMEM_LIMIT = 56 * 1024 * 1024


def _params(*sem):
    return pltpu.CompilerParams(dimension_semantics=sem, vmem_limit_bytes=VMEM_LIMIT)


def _rms(x, g):
    return x * lax.rsqrt(jnp.mean(x * x, axis=-1, keepdims=True) + EPS) * g


def _split3(x):
    hi = x.astype(BF16).astype(F32)
    r = x - hi
    mid = r.astype(BF16).astype(F32)
    return hi, mid, r - mid


SLAB = D_MODEL // LANES


def _rows_from_slabs(ref):
    return jnp.concatenate([ref[:, c, :] for c in range(SLAB)], axis=1)


def _rows_to_slabs(ref, x):
    for c in range(SLAB):
        ref[:, c, :] = x[:, c * LANES:(c + 1) * LANES]


def _dot(a, b):
    return jnp.dot(a, b, preferred_element_type=F32)


def _dot_nt(a, b):
    return lax.dot_general(a, b, (((1,), (1,)), ((), ())), preferred_element_type=F32)


def _dot_tn(a, b):
    return lax.dot_general(a, b, (((0,), (0,)), ((), ())), preferred_element_type=F32)


HSUM_COLS = 256


def _head_sum_matrix():
    i = jnp.arange(HSUM_COLS) // HEAD_DIM
    return (i[:, None] == i[None, :]).astype(BF16)


def _head_rms(p, hsum_ref):
    outs = []
    for j in range(p.shape[1] // HSUM_COLS):
        s = p[:, j * HSUM_COLS:(j + 1) * HSUM_COLS]
        ss = _dot((s * s).astype(BF16), hsum_ref[...])
        outs.append(s * lax.rsqrt(ss * (1.0 / HEAD_DIM) + EPS))
    return outs[0] if len(outs) == 1 else jnp.concatenate(outs, axis=1)


FOX_IN_COLS = 256
FOX_IN_AHEAD = 2


def _fox_in_kernel(x_ref, g_ref, wqk_ref, wvt_ref, wf_ref, bf_ref, qg_ref, kg_ref, sel_ref, hsum_ref,
                   q_ref, k_ref, vt_ref, carry_ref):
    tm = x_ref.shape[1]
    aw = A_HEADS * HEAD_DIM

    @pl.when(pl.program_id(1) == 0)
    def _():
        carry_ref[...] = jnp.zeros_like(carry_ref)

    xb = _rms(x_ref[0], g_ref[...]).astype(BF16)
    lane = lax.broadcasted_iota(jnp.int32, (1, LANES), 1)
    z = _dot(xb, wf_ref[...]) + bf_ref[...]
    lf = jnp.minimum(z, 0.0) - jnp.log1p(jnp.exp(-jnp.abs(z)))
    lf = jnp.where(lane < A_HEADS, lf, 0.0)
    row = lax.broadcasted_iota(jnp.int32, (tm, tm), 0)
    col = lax.broadcasted_iota(jnp.int32, (tm, tm), 1)
    tri = (col <= row).astype(BF16)
    hi, mid, lo = _split3(lf)
    cum = (_dot(tri, hi.astype(BF16)) + _dot(tri, mid.astype(BF16))
           + _dot(tri, lo.astype(BF16))) + carry_ref[...]
    carry_ref[...] = cum[tm - 1:tm, :]
    c_hi, c_mid, c_lo = _split3(cum * (-LOG2E))
    cc = (c_hi + pltpu.roll(c_mid, A_HEADS, 1) + pltpu.roll(c_lo, 2 * A_HEADS, 1)).astype(BF16)

    head_lane = lane < HEAD_DIM
    ones_vec = jnp.where((lane >= HEAD_DIM) & (lane < HEAD_DIM + 3), 1.0, 0.0)
    per = FOX_IN_COLS // HEAD_DIM
    n_tiles = 2 * aw // FOX_IN_COLS

    def project(n):
        return _dot(xb, wqk_ref[:, n * FOX_IN_COLS:(n + 1) * FOX_IN_COLS])

    def finish(n, p):
        is_k = n * FOX_IN_COLS >= aw
        h0 = (n * FOX_IN_COLS - (aw if is_k else 0)) // HEAD_DIM
        if is_k:
            aug = _dot(cc, sel_ref[:, h0 * LANES:(h0 + per) * LANES])
        pn = _head_rms(p, hsum_ref)
        for i in range(per):
            s = pn[:, (i // 2) * LANES:(i // 2 + 1) * LANES]
            if i % 2:
                s = pltpu.roll(s, HEAD_DIM, 1)
            dst = slice((h0 + i) * LANES, (h0 + i + 1) * LANES)
            if is_k:
                k_ref[0, :, dst] = jnp.where(head_lane, s * kg_ref[...], aug[:, i * LANES:(i + 1) * LANES]).astype(BF16)
            else:
                q_ref[0, :, dst] = jnp.where(head_lane, s * qg_ref[...], ones_vec).astype(BF16)

    pending = {n: project(n) for n in range(min(FOX_IN_AHEAD, n_tiles))}
    for n in range(n_tiles):
        if n + FOX_IN_AHEAD < n_tiles:
            pending[n + FOX_IN_AHEAD] = project(n + FOX_IN_AHEAD)
        elif n + FOX_IN_AHEAD == n_tiles:
            vt_ref[0, 0] = _dot_nt(wvt_ref[...], xb).astype(BF16)
        finish(n, pending.pop(n))


def _fox_in(x, a_norm, w_in, b_f, q_gain, k_gain):
    B, S, _ = x.shape
    tm = TOK_TILE
    nt = S // tm
    aw = A_HEADS * HEAD_DIM
    wqk = w_in[:, :2 * aw].astype(BF16)
    wvt = w_in[:, 2 * aw:3 * aw].T.astype(BF16)
    wf = jnp.pad(w_in[:, 3 * aw:], ((0, 0), (0, LANES - A_HEADS))).astype(BF16)
    bf = jnp.pad(b_f, (0, LANES - A_HEADS)).reshape(1, LANES)
    qg = jnp.pad(q_gain * (SCALE * LOG2E), (0, LANES - HEAD_DIM)).reshape(1, LANES)
    kg = jnp.pad(k_gain, (0, LANES - HEAD_DIM)).reshape(1, LANES)
    src = jnp.arange(LANES)
    dst = (src % A_HEADS) * LANES + HEAD_DIM + src // A_HEADS
    sel = ((dst[:, None] == jnp.arange(A_HEADS * LANES)[None, :]) & (src[:, None] < 3 * A_HEADS)).astype(BF16)
    const = lambda b, t: (0, 0)
    return pl.pallas_call(
        _fox_in_kernel,
        grid=(B, nt),
        in_specs=[
            pl.BlockSpec((1, tm, D_MODEL), lambda b, t: (b, t, 0)),
            pl.BlockSpec((1, D_MODEL), const),
            pl.BlockSpec((D_MODEL, 2 * aw), const),
            pl.BlockSpec((aw, D_MODEL), const),
            pl.BlockSpec((D_MODEL, LANES), const),
            pl.BlockSpec((1, LANES), const),
            pl.BlockSpec((1, LANES), const),
            pl.BlockSpec((1, LANES), const),
            pl.BlockSpec((LANES, A_HEADS * LANES), const),
            pl.BlockSpec((HSUM_COLS, HSUM_COLS), const),
        ],
        out_specs=[
            pl.BlockSpec((1, tm, A_HEADS * LANES), lambda b, t: (b, t, 0)),
            pl.BlockSpec((1, tm, A_HEADS * LANES), lambda b, t: (b, t, 0)),
            pl.BlockSpec((1, 1, aw, tm), lambda b, t: (b, t, 0, 0)),
        ],
        out_shape=[
            jax.ShapeDtypeStruct((B, S, A_HEADS * LANES), BF16),
            jax.ShapeDtypeStruct((B, S, A_HEADS * LANES), BF16),
            jax.ShapeDtypeStruct((B, nt, aw, tm), BF16),
        ],
        scratch_shapes=[pltpu.VMEM((1, LANES), F32)],
        compiler_params=_params("arbitrary", "arbitrary"),
        name="fox_in",
    )(x, a_norm.reshape(1, D_MODEL), wqk, wvt, wf, bf, qg, kg, sel, _head_sum_matrix())


FOX_TQ = 4096
FOX_QCHUNK = 256
FOX_UNROLL = 4
FOX_AHEAD = 8
FOX_KSUB = 256


def _fox_attn_kernel(q_ref, k_ref, vt_ref, o_ref, *scratch):
    tq = q_ref.shape[1]
    tk = vt_ref.shape[3]
    qc = FOX_QCHUNK
    nc = tq // qc
    ratio = tq // tk
    qi = pl.program_id(2)
    acc_refs, m_refs = scratch[:nc], scratch[nc:]
    for c in range(nc):
        m_refs[c][...] = jnp.full_like(m_refs[c], NEG)
        acc_refs[c][...] = jnp.zeros_like(acc_refs[c])

    def scores(kj, c):
        return _dot_nt(kj, q_ref[0, c * qc:(c + 1) * qc, :])

    def update(st, vj, c, mask):
        if mask is not None:
            st = jnp.where(mask, st, NEG)
        m_prev = m_refs[c][...]
        m_new = jnp.maximum(m_prev, jnp.max(st, axis=0, keepdims=True))
        p = jnp.exp2(st - m_new).astype(BF16)
        alpha = jnp.exp2(m_prev - m_new)
        acc_refs[c][...] = alpha * acc_refs[c][...] + _dot(vj, p)
        m_refs[c][...] = m_new

    ks = FOX_KSUB
    ones = jnp.ones((16, ks), BF16)

    def load(j, part):
        kj = k_ref[0, pl.ds(pl.multiple_of(j * tk, tk) + part * ks, ks), :]
        vj = jnp.concatenate([vt_ref[0, j, :, part * ks:(part + 1) * ks], ones], axis=0)
        return kj, vj

    def run(work):
        st = [None] * len(work)
        for i in range(min(FOX_AHEAD, len(work))):
            st[i] = scores(work[i][0], work[i][2])
        for i, (kj, vj, c, mask) in enumerate(work):
            if i + FOX_AHEAD < len(work):
                st[i + FOX_AHEAD] = scores(work[i + FOX_AHEAD][0], work[i + FOX_AHEAD][2])
            update(st[i], vj, c, mask)
            st[i] = None

    def body(jj, carry):
        work = []
        for u in range(FOX_UNROLL):
            for part in range(tk // ks):
                kj, vj = load(jj * FOX_UNROLL + u, part)
                work += [(kj, vj, c, None) for c in range(nc)]
        run(work)
        return carry

    lax.fori_loop(0, qi * (ratio // FOX_UNROLL), body, 0)
    work = []
    for d in range(ratio * (tk // ks)):
        kj, vj = load(qi * ratio + d // (tk // ks), d % (tk // ks))
        for c in range(nc):
            if d * ks > (c + 1) * qc - 1:
                continue
            if (d + 1) * ks - 1 <= c * qc:
                work.append((kj, vj, c, None))
            else:
                kpos = d * ks + lax.broadcasted_iota(jnp.int32, (ks, qc), 0)
                qpos = c * qc + lax.broadcasted_iota(jnp.int32, (ks, qc), 1)
                work.append((kj, vj, c, kpos <= qpos))
    run(work)
    for c in range(nc):
        acc = acc_refs[c][...]
        o_ref[0, :, c * qc:(c + 1) * qc] = (acc[:HEAD_DIM] / acc[HEAD_DIM:HEAD_DIM + 1]).astype(BF16)


def _fox_attn(q, k, vt):
    B, S, _ = q.shape
    nt, tk = vt.shape[1], vt.shape[3]
    tq = FOX_TQ
    assert tq % (tk * FOX_UNROLL) == 0 and tq % FOX_QCHUNK == 0
    return pl.pallas_call(
        _fox_attn_kernel,
        grid=(B, A_HEADS, S // tq),
        in_specs=[
            pl.BlockSpec((1, tq, LANES), lambda b, h, i: (b, i, h)),
            pl.BlockSpec((1, S, LANES), lambda b, h, i: (b, 0, h)),
            pl.BlockSpec((1, nt, HEAD_DIM, tk), lambda b, h, i: (b, 0, h, 0)),
        ],
        out_specs=pl.BlockSpec((1, HEAD_DIM, tq), lambda b, h, i: (b, h, i)),
        out_shape=jax.ShapeDtypeStruct((B, A_HEADS * HEAD_DIM, S), BF16),
        scratch_shapes=([pltpu.VMEM((HEAD_DIM + 16, FOX_QCHUNK), F32)] * (tq // FOX_QCHUNK)
                        + [pltpu.VMEM((1, FOX_QCHUNK), F32)] * (tq // FOX_QCHUNK)),
        compiler_params=_params("arbitrary", "arbitrary", "arbitrary"),
        name="fox_attn",
    )(q, k, vt)


def _route(xn, wrh_ref, wrl_ref, br_ref):
    xh = xn.astype(BF16)
    xl = (xn - xh.astype(F32)).astype(BF16)
    wh = wrh_ref[...]
    logits = _dot(xh, wh) + _dot(xl, wh) + _dot(xh, wrl_ref[...]) + br_ref[...]
    lane = lax.broadcasted_iota(jnp.int32, (1, LANES), 1)
    lanef = lane.astype(F32)
    far = float(LANES)

    gl = jnp.where(lane < N_GROUPS, logits, NEG)
    gm = jnp.max(gl, axis=-1, keepdims=True)
    g_val = 1.0 / jnp.sum(jnp.exp(gl - gm), axis=-1, keepdims=True)
    g_idx = jnp.min(jnp.where(gl == gm, lanef, far), axis=-1, keepdims=True)

    lo = N_GROUPS + EXPERTS_PER_GROUP * g_idx
    el = jnp.where((lanef >= lo) & (lanef < lo + EXPERTS_PER_GROUP), logits, NEG)
    em1 = jnp.max(el, axis=-1, keepdims=True)
    ez = jnp.sum(jnp.exp(el - em1), axis=-1, keepdims=True)
    i1 = jnp.min(jnp.where(el == em1, lanef, far), axis=-1, keepdims=True)
    el2 = jnp.where(lanef == i1, NEG, el)
    em2 = jnp.max(el2, axis=-1, keepdims=True)
    i2 = jnp.min(jnp.where(el2 == em2, lanef, far), axis=-1, keepdims=True)
    p1 = 1.0 / ez
    p2 = jnp.exp(em2 - em1) / ez
    den = p1 + p2
    gate1 = g_val * (p1 / den)
    gate2 = g_val * (p2 / den)
    return jnp.where(lane == 0, i1 - N_GROUPS,
                     jnp.where(lane == 1, i2 - N_GROUPS,
                               jnp.where(lane == 2, gate1,
                                         jnp.where(lane == 3, gate2, 0.0))))


def _router_weights(w_group, b_group, w_expert, b_expert):
    w = jnp.pad(jnp.concatenate([w_group, w_expert], axis=1),
                ((0, 0), (0, LANES - N_GROUPS - N_EXPERTS)))
    b = jnp.pad(jnp.concatenate([b_group, b_expert]), (0, LANES - N_GROUPS - N_EXPERTS))
    wh = w.astype(BF16)
    wl = (w - wh.astype(F32)).astype(BF16)
    return wh, wl, b.reshape(1, LANES)


def _ffn_in(h, fg_ref, wrh_ref, wrl_ref, br_ref, h_ref, xn_hbm, route_ref, xbuf, sem):
    tm = h.shape[0]
    groups = tm // 8
    b, t = pl.program_id(0), pl.program_id(1)
    step = b * pl.num_programs(1) + t
    last = pl.num_programs(0) * pl.num_programs(1) - 1
    slot = lax.rem(step, 2)

    def copies(s, bb, tt):
        return [pltpu.make_async_copy(xbuf.at[s, :, c], xn_hbm.at[bb, pl.ds(tt * groups, groups), :, c, :], sem.at[s])
                for c in range(SLAB)]

    def wait_slot(s):
        pltpu.make_async_copy(xbuf.at[s], xbuf.at[s], sem.at[s]).wait()

    h_ref[0] = h
    xn = _rms(h, fg_ref[...])

    @pl.when(step >= 2)
    def _():
        wait_slot(slot)

    for c in range(SLAB):
        xbuf[slot, :, c] = xn[:, c * LANES:(c + 1) * LANES].reshape(groups, 8, LANES)
    for cp in copies(slot, b, t):
        cp.start()
    route_ref[0] = _route(xn, wrh_ref, wrl_ref, br_ref)

    @pl.when(step == last)
    def _():
        @pl.when(step >= 1)
        def _():
            wait_slot(1 - slot)
        wait_slot(slot)


def _fox_out_kernel(ot_ref, x_ref, wo_ref, fg_ref, wrh_ref, wrl_ref, br_ref,
                    h_ref, xn_hbm, route_ref, xbuf, sem):
    h = x_ref[0] + _dot(ot_ref[0].T, wo_ref[...])
    _ffn_in(h, fg_ref, wrh_ref, wrl_ref, br_ref, h_ref, xn_hbm, route_ref, xbuf, sem)


def _token_out_specs(B, S, tm):
    specs = [
        pl.BlockSpec((1, tm, D_MODEL), lambda b, t: (b, t, 0)),
        pl.BlockSpec(memory_space=pl.ANY),
        pl.BlockSpec((1, tm, LANES), lambda b, t: (b, t, 0)),
    ]
    shapes = [
        jax.ShapeDtypeStruct((B, S, D_MODEL), F32),
        jax.ShapeDtypeStruct((B, S // 8, 8, SLAB, LANES), F32),
        jax.ShapeDtypeStruct((B, S, LANES), F32),
    ]
    scratch = [pltpu.VMEM((2, tm // 8, SLAB, 8, LANES), F32), pltpu.SemaphoreType.DMA((2,))]
    return specs, shapes, scratch


def _fox_out(ot, x, w_out, ffn_g, router):
    B, S, _ = x.shape
    tm = TOK_TILE
    wrh, wrl, br = router
    const = lambda b, t: (0, 0)
    out_specs, out_shape, scratch = _token_out_specs(B, S, tm)
    return pl.pallas_call(
        _fox_out_kernel,
        grid=(B, S // tm),
        in_specs=[
            pl.BlockSpec((1, A_HEADS * HEAD_DIM, tm), lambda b, t: (b, 0, t)),
            pl.BlockSpec((1, tm, D_MODEL), lambda b, t: (b, t, 0)),
            pl.BlockSpec((A_HEADS * HEAD_DIM, D_MODEL), const),
            pl.BlockSpec((1, D_MODEL), const),
            pl.BlockSpec((D_MODEL, LANES), const),
            pl.BlockSpec((D_MODEL, LANES), const),
            pl.BlockSpec((1, LANES), const),
        ],
        out_specs=out_specs,
        out_shape=out_shape,
        scratch_shapes=scratch,
        compiler_params=_params("arbitrary", "arbitrary"),
        name="fox_out",
    )(ot, x, w_out.astype(BF16), ffn_g.reshape(1, D_MODEL), wrh, wrl, br)


MOE_TILE = 256
MOE_CHUNKS = 4
PAIRS_PER_GROUP = EXPERTS_PER_GROUP * (EXPERTS_PER_GROUP - 1) // 2
N_CLASSES = N_GROUPS * PAIRS_PER_GROUP
_PAIR_LO = [a for a in range(EXPERTS_PER_GROUP) for b in range(a + 1, EXPERTS_PER_GROUP)]
_PAIR_HI = [b for a in range(EXPERTS_PER_GROUP) for b in range(a + 1, EXPERTS_PER_GROUP)]


def _moe_plan(route, n_tok):
    tm = MOE_TILE
    n_tiles = n_tok // tm + N_CLASSES
    e1, e2 = route[:, 0].astype(jnp.int32), route[:, 1].astype(jnp.int32)
    swap = e2 < e1
    lo, hi = jnp.where(swap, e2, e1), jnp.where(swap, e1, e2)
    gates = jnp.stack([jnp.where(swap, route[:, 3], route[:, 2]),
                       jnp.where(swap, route[:, 2], route[:, 3])], axis=1)
    la, lb = lo % EXPERTS_PER_GROUP, hi % EXPERTS_PER_GROUP
    pair = (la * (2 * EXPERTS_PER_GROUP - 1 - la)) // 2 + (lb - la - 1)
    cls = (lo // EXPERTS_PER_GROUP) * PAIRS_PER_GROUP + pair
    counts = jnp.sum(cls[:, None] == jnp.arange(N_CLASSES)[None, :], axis=0).astype(jnp.int32)
    tiles_per = (counts + tm - 1) // tm
    tile_end = jnp.cumsum(tiles_per)
    n_used = tile_end[-1]
    tid = jnp.arange(n_tiles, dtype=jnp.int32)
    tid_c = jnp.minimum(tid, n_used - 1)
    tile_c = jnp.minimum(jnp.sum(tile_end[None, :] <= tid_c[:, None], axis=1), N_CLASSES - 1).astype(jnp.int32)
    base = (tile_c // PAIRS_PER_GROUP) * EXPERTS_PER_GROUP
    tile_a = (base + jnp.array(_PAIR_LO, jnp.int32)[tile_c % PAIRS_PER_GROUP]).astype(jnp.int32)
    tile_b = (base + jnp.array(_PAIR_HI, jnp.int32)[tile_c % PAIRS_PER_GROUP]).astype(jnp.int32)
    within = (tid_c - (tile_end - tiles_per)[tile_c]) * tm
    valid = jnp.where(tid < n_used, jnp.clip(counts[tile_c] - within, 0, tm), 0).astype(jnp.int32)
    j = jnp.arange(tm, dtype=jnp.int32)[None, :]
    unused = 2 * N_CLASSES + 1
    pad_key = jnp.where(j < (tiles_per * tm - counts)[:, None],
                        2 * jnp.arange(N_CLASSES, dtype=jnp.int32)[:, None] + 1, unused).reshape(-1)
    n_pad = N_CLASSES * tm
    keys = jnp.concatenate([2 * cls, pad_key])
    vals = jnp.concatenate([jnp.arange(n_tok, dtype=jnp.int32), jnp.zeros((n_pad,), jnp.int32)])
    g_lo = jnp.concatenate([gates[:, 0], jnp.zeros((n_pad,), F32)])
    g_hi = jnp.concatenate([gates[:, 1], jnp.zeros((n_pad,), F32)])
    keys, tok, g_lo, g_hi = lax.sort((keys, vals, g_lo, g_hi), num_keys=1, is_stable=True)
    live = (keys % 2 == 0).reshape(n_tiles, tm)
    tok = tok.reshape(n_tiles, tm)
    dst = jnp.where(live, tok, n_tok + (tid % 2)[:, None] * tm + j).astype(jnp.int32)
    return tile_a, tile_b, valid, tok, dst, jnp.stack([g_lo, g_hi], axis=1)


def _moe_kernel(ta_ref, tb_ref, nv_ref, tok_hbm, dst_hbm, xn_hbm, gate_ref, wga_ref, wua_ref, wda_ref,
                wgb_ref, wub_ref, wdb_ref, out_hbm, tok_s, dst_s, xbuf, ybuf, tsem, dsem, gsem, ssem):
    del ta_ref, tb_ref
    groups = xbuf.shape[1]
    tm = groups * 8
    i = pl.program_id(0)
    n_tiles = pl.num_programs(0)
    slot = lax.rem(i, 2)
    other = 1 - slot

    def table_copies(t):
        mod = (lambda a, n: a % n) if isinstance(t, int) else lax.rem
        ts, ds_ = mod(t, 2), mod(t + 1, 3)
        return (pltpu.make_async_copy(tok_hbm.at[t + 1], tok_s.at[pl.ds(ts * tm, tm)], tsem.at[ts]),
                pltpu.make_async_copy(dst_hbm.at[t + 1], dst_s.at[pl.ds(ds_ * tm, tm)], dsem.at[ds_]))

    def gather_row(s, base, g, j):
        pltpu.make_async_copy(xn_hbm.at[tok_s[base + j]], xbuf.at[s, g, :, j, :], gsem.at[s]).start(priority=j % 2)

    def scatter_row(s, base, g, j):
        pltpu.make_async_copy(ybuf.at[s, g, :, j, :], out_hbm.at[dst_s[base + j]], ssem.at[s]).start(priority=j % 2)

    def gather_loop(s):
        def body(g, c):
            for j in range(8):
                gather_row(s, s * tm + g * 8, g, j)
            return c
        lax.fori_loop(0, groups, body, 0)

    def scatter_loop(s, dslot):
        def body(g, c):
            for j in range(8):
                scatter_row(s, dslot * tm + g * 8, g, j)
            return c
        lax.fori_loop(0, groups, body, 0)

    def wait_rows(buf, sem, s):
        pltpu.make_async_copy(buf.at[s], buf.at[s], sem.at[s]).wait()

    @pl.when(i == 0)
    def _():
        for cp in table_copies(-1) + table_copies(0):
            cp.start()
        for cp in table_copies(-1) + table_copies(0):
            cp.wait()
        gather_loop(0)
        for cp in table_copies(1):
            cp.start()
        ybuf[1] = jnp.zeros(ybuf.shape[1:], F32)

    for cp in table_copies(i + 1):
        cp.wait()
    wait_rows(xbuf, gsem, slot)

    @pl.when(i >= 1)
    def _():
        wait_rows(ybuf, ssem, slot)

    prev_dst = lax.rem(i, 3)

    def neighbour_dmas(chunk):
        per = groups // MOE_CHUNKS
        for g in range(chunk * per, (chunk + 1) * per):
            for j in range(8):
                scatter_row(other, prev_dst * tm + g * 8, g, j)
                gather_row(other, other * tm + g * 8, g, j)

    @pl.when(nv_ref[i] > 0)
    def _():
        xb = jnp.concatenate([xbuf[slot, :, c].reshape(tm, LANES) for c in range(SLAB)], axis=1).astype(BF16)
        neighbour_dmas(0)
        ga, ua = _dot(xb, wga_ref[0]), _dot(xb, wua_ref[0])
        neighbour_dmas(1)
        gb, ub = _dot(xb, wgb_ref[0]), _dot(xb, wub_ref[0])
        neighbour_dmas(2)
        ya = _dot((jax.nn.silu(ga) * ua).astype(BF16), wda_ref[0])
        neighbour_dmas(3)
        yb = _dot((jax.nn.silu(gb) * ub).astype(BF16), wdb_ref[0])
        gates = gate_ref[...]
        y = gates[:, 0:1] * ya + gates[:, 1:2] * yb
        for c in range(SLAB):
            ybuf[slot, :, c] = y[:, c * LANES:(c + 1) * LANES].reshape(groups, 8, LANES)

    @pl.when(nv_ref[i] <= 0)
    def _():
        scatter_loop(other, prev_dst)
        gather_loop(other)

    @pl.when(i + 2 <= n_tiles)
    def _():
        for cp in table_copies(i + 2):
            cp.start()

    @pl.when(i == n_tiles - 1)
    def _():
        scatter_loop(slot, lax.rem(i + 1, 3))
        wait_rows(ybuf, ssem, other)
        wait_rows(ybuf, ssem, slot)
        wait_rows(xbuf, gsem, other)


def _moe(xn, route, w_gate, w_up, w_down):
    n_tok = xn.shape[0] * xn.shape[1] * xn.shape[2]
    tm = MOE_TILE
    tile_a, tile_b, valid, tok, dst, row_gates = _moe_plan(route.reshape(n_tok, LANES), n_tok)
    n_tiles = tile_a.shape[0]
    up = lambda which: pl.BlockSpec((1, D_MODEL, D_EXPERT), lambda i, ta, tb, nv: ((ta, tb)[which][i], 0, 0))
    down = lambda which: pl.BlockSpec((1, D_EXPERT, D_MODEL), lambda i, ta, tb, nv: ((ta, tb)[which][i], 0, 0))
    grid_spec = pltpu.PrefetchScalarGridSpec(
        num_scalar_prefetch=3,
        grid=(n_tiles,),
        in_specs=[
            pl.BlockSpec(memory_space=pl.ANY),
            pl.BlockSpec(memory_space=pl.ANY),
            pl.BlockSpec(memory_space=pl.ANY),
            pl.BlockSpec((tm, TOP_K), lambda i, ta, tb, nv: (i, 0)),
            up(0), up(0), down(0), up(1), up(1), down(1),
        ],
        out_specs=pl.BlockSpec(memory_space=pl.ANY),
        scratch_shapes=[
            pltpu.SMEM((2 * tm,), jnp.int32),
            pltpu.SMEM((3 * tm,), jnp.int32),
            pltpu.VMEM((2, tm // 8, SLAB, 8, LANES), F32),
            pltpu.VMEM((2, tm // 8, SLAB, 8, LANES), F32),
            pltpu.SemaphoreType.DMA((2,)),
            pltpu.SemaphoreType.DMA((3,)),
            pltpu.SemaphoreType.DMA((2,)),
            pltpu.SemaphoreType.DMA((2,)),
        ],
    )
    j = jnp.arange(tm, dtype=jnp.int32)[None, :]
    tok = jnp.concatenate([jnp.zeros((1, tm), jnp.int32), tok, jnp.zeros((1, tm), jnp.int32)])
    dst = jnp.concatenate([n_tok + tm + j, dst, n_tok + j])
    wg, wu, wd = w_gate.astype(BF16), w_up.astype(BF16), w_down.astype(BF16)
    return pl.pallas_call(
        _moe_kernel,
        grid_spec=grid_spec,
        out_shape=jax.ShapeDtypeStruct((n_tok + 2 * tm, SLAB, LANES), F32),
        compiler_params=_params("arbitrary"),
        name="moe_experts",
    )(tile_a, tile_b, valid, tok, dst, xn.reshape(n_tok, SLAB, LANES), row_gates, wg, wu, wd, wg, wu, wd)


def _moe_out_spec(S, tm):
    return pl.BlockSpec((tm, SLAB, LANES), lambda b, t: (b * (S // tm) + t, 0, 0))


def _residue_perm(tm, d):
    i = jnp.arange(tm)
    src = (i % (tm // d)) * d + i // (tm // d)
    return (src[:, None] == jnp.arange(tm)[None, :]).astype(BF16)


def _dil_in_kernel(h_ref, y_ref, kvg_ref, bg_ref, wk_ref, wv_ref, wq_ref, kgain_ref,
                   qgain_ref, p1_ref, p2_ref, hsum_ref, h2_ref, *qkv_refs):
    tm = h_ref.shape[1]
    w = B_OUT_WIDTH
    h2 = h_ref[0] + _rows_from_slabs(y_ref)
    h2_ref[0] = h2
    u = h2 * lax.rsqrt(jnp.mean(h2 * h2, axis=-1, keepdims=True) + EPS)
    xkv = (u * kvg_ref[...]).astype(BF16)
    xq = (u * bg_ref[...]).astype(BF16)

    def project(g):
        cols = slice(g * w, (g + 1) * w)
        return _dot(xq, wq_ref[:, cols]), _dot(xkv, wk_ref[:, cols]), _dot(xkv, wv_ref[:, cols])

    def finish(g, d, qkv):
        q, k, v = qkv
        cat = jnp.concatenate([(_head_rms(q, hsum_ref) * qgain_ref[...]).astype(BF16),
                               (_head_rms(k, hsum_ref) * kgain_ref[...]).astype(BF16),
                               v.astype(BF16)], axis=1)
        if d > 1:
            cat = _dot((p1_ref if g == 1 else p2_ref)[...], cat).astype(BF16)
        rows = tm // d
        for t in range(3):
            ref = qkv_refs[3 * g + t]
            for r in range(d):
                ref[0, r] = cat[r * rows:(r + 1) * rows, t * w:(t + 1) * w]

    nxt = project(0)
    for g, d in enumerate(B_DILATIONS):
        cur = nxt
        if g + 1 < B_GROUPS:
            nxt = project(g + 1)
        finish(g, d, cur)


def _dil_in(h, y, kv_norm, b_norm, kv_w, w_q, k_gain, q_gain):
    B, S, _ = h.shape
    tm = TOK_TILE
    const = lambda b, t: (0, 0)
    tok = lambda w: pl.BlockSpec((1, tm, w), lambda b, t: (b, t, 0))
    qkv_specs, qkv_shapes = [], []
    for d in B_DILATIONS:
        qkv_specs += [pl.BlockSpec((1, d, tm // d, B_OUT_WIDTH), lambda b, t: (b, 0, t, 0))] * 3
        qkv_shapes += [jax.ShapeDtypeStruct((B, d, S // d, B_OUT_WIDTH), BF16)] * 3
    outs = pl.pallas_call(
        _dil_in_kernel,
        grid=(B, S // tm),
        in_specs=[
            tok(D_MODEL),
            _moe_out_spec(S, tm),
            pl.BlockSpec((1, D_MODEL), const),
            pl.BlockSpec((1, D_MODEL), const),
            pl.BlockSpec((D_MODEL, B_WIDTH), const),
            pl.BlockSpec((D_MODEL, B_WIDTH), const),
            pl.BlockSpec((D_MODEL, B_WIDTH), const),
            pl.BlockSpec((1, B_OUT_WIDTH), const),
            pl.BlockSpec((1, B_OUT_WIDTH), const),
            pl.BlockSpec((tm, tm), const),
            pl.BlockSpec((tm, tm), const),
            pl.BlockSpec((HSUM_COLS, HSUM_COLS), const),
        ],
        out_specs=[tok(D_MODEL)] + qkv_specs,
        out_shape=[jax.ShapeDtypeStruct((B, S, D_MODEL), F32)] + qkv_shapes,
        compiler_params=_params("arbitrary", "arbitrary"),
        name="dil_in",
    )(h, y, kv_norm.reshape(1, D_MODEL), b_norm.reshape(1, D_MODEL),
      kv_w[:, :B_WIDTH].astype(BF16), kv_w[:, B_WIDTH:].astype(BF16), w_q.astype(BF16),
      jnp.tile(k_gain, B_HEADS_PER_GROUP).reshape(1, B_OUT_WIDTH),
      jnp.tile(q_gain * (SCALE * LOG2E), B_HEADS_PER_GROUP).reshape(1, B_OUT_WIDTH),
      _residue_perm(tm, B_DILATIONS[1]), _residue_perm(tm, B_DILATIONS[2]), _head_sum_matrix())
    return outs[0], [outs[1 + 3 * g:4 + 3 * g] for g in range(B_GROUPS)]


def _t5_bucket(dist):
    max_exact = NUM_BUCKETS // 2
    d_f = jnp.maximum(dist, max_exact).astype(F32)
    large = max_exact + (jnp.log(d_f / max_exact) / math.log(MAX_DISTANCE / max_exact)
                         * (NUM_BUCKETS - max_exact)).astype(jnp.int32)
    large = jnp.minimum(large, NUM_BUCKETS - 1)
    return jnp.where(dist < max_exact, dist, large)


def _branch_bias(rel_bias, g, d):
    a = jnp.arange(BLOCK)[:, None]
    b = jnp.arange(2 * BLOCK)[None, :]
    n = BLOCK + a - b
    band = (n >= 0) & (n <= B_WINDOWS[g] // d)
    bucket = _t5_bucket(jnp.maximum(n, 0) * d)[..., None]
    ids = jnp.arange(NUM_BUCKETS)
    onehot = ((bucket >= ids) & (bucket < ids + 1)).astype(F32)
    table = rel_bias[:, g * B_HEADS_PER_GROUP:(g + 1) * B_HEADS_PER_GROUP].astype(F32)
    bias = jnp.einsum("abk,kh->hab", onehot, table, precision=lax.Precision.HIGHEST) * LOG2E
    return jnp.where(band[None], bias, NEG)


DIL_SUB = 4
DIL_AHEAD = 3


def _dil_attn_kernel(q_ref, kp_ref, kc_ref, vp_ref, vc_ref, bias_ref, o_ref, lse_ref):
    n = pl.program_id(2)
    lane = lax.broadcasted_iota(jnp.int32, (1, LANES), 1)
    first = lane < HEAD_DIM
    col = lax.broadcasted_iota(jnp.int32, (1, 2 * BLOCK), 1)
    dead = (n == 0) & (col < BLOCK)
    pairs = B_HEADS_PER_GROUP // 2

    def band(cur_ref, prev_ref, i, j):
        sl = slice(j * LANES, (j + 1) * LANES)
        if i == 0:
            return jnp.concatenate([prev_ref[0, 0, :, sl], cur_ref[0, 0, :BLOCK, sl]], axis=0)
        return cur_ref[0, 0, (i - 1) * BLOCK:(i + 1) * BLOCK, sl]

    def scores(i, j, hh):
        q = q_ref[0, 0, i * BLOCK:(i + 1) * BLOCK, j * LANES:(j + 1) * LANES]
        keep = jnp.where(first if hh == 0 else jnp.logical_not(first), 1.0, 0.0).astype(BF16)
        return _dot_nt(q * keep, band(kc_ref, kp_ref, i, j))

    items = [(i, j, hh) for i in range(q_ref.shape[2] // BLOCK) for j in range(pairs) for hh in range(2)]
    st, sb, mx, pr, dn, outs = {}, {}, {}, {}, {}, {}
    state = {"m_blk": None, "den_blk": None}

    def stage_max(t):
        i, j, hh = t
        bias = bias_ref[2 * j + hh]
        sb[t] = st.pop(t) + (jnp.where(dead, NEG, bias) if i == 0 else bias)
        mx[t] = jnp.max(sb[t], axis=-1, keepdims=True)

    def stage_exp(t):
        pr[t] = jnp.exp2(sb.pop(t) - mx[t])
        dn[t] = jnp.sum(pr[t], axis=-1, keepdims=True)

    def stage_out(t):
        i, j, hh = t
        m, den = mx.pop(t), dn.pop(t)
        outs[hh] = _dot(pr.pop(t).astype(BF16), band(vc_ref, vp_ref, i, j)) * (1.0 / den)
        if j == 0 and hh == 0:
            state["m_blk"] = jnp.zeros((BLOCK, LANES), F32)
            state["den_blk"] = jnp.ones((BLOCK, LANES), F32)
        state["m_blk"] = jnp.where(lane == 2 * j + hh, m, state["m_blk"])
        state["den_blk"] = jnp.where(lane == 2 * j + hh, den, state["den_blk"])
        if hh == 1:
            o_ref[0, 0, i * BLOCK:(i + 1) * BLOCK, j * LANES:(j + 1) * LANES] = (
                jnp.where(first, outs[0], outs[1]).astype(BF16))
            if j == pairs - 1:
                lse = jnp.where(lane < B_HEADS_PER_GROUP, state["m_blk"] + jnp.log2(state["den_blk"]), 0.0)
                hi, mid, lo = _split3(lse)
                lse_ref[0, 0, i * BLOCK:(i + 1) * BLOCK, :] = (
                    hi + pltpu.roll(mid, 8, 1) + pltpu.roll(lo, 16, 1)).astype(BF16)

    n_items = len(items)
    for step in range(n_items + DIL_AHEAD + 2):
        if step < n_items:
            st[items[step]] = scores(*items[step])
        if 0 <= step - DIL_AHEAD < n_items:
            stage_max(items[step - DIL_AHEAD])
        if 0 <= step - DIL_AHEAD - 1 < n_items:
            stage_exp(items[step - DIL_AHEAD - 1])
        if 0 <= step - DIL_AHEAD - 2 < n_items:
            stage_out(items[step - DIL_AHEAD - 2])


def _dil_attn(q, k, v, bias, g):
    B, d, L, w = q.shape
    sub = min(DIL_SUB, L // BLOCK)
    rows = sub * BLOCK
    cur = pl.BlockSpec((1, 1, rows, w), lambda b, r, n: (b, r, n, 0))
    prev = pl.BlockSpec((1, 1, BLOCK, w), lambda b, r, n: (b, r, jnp.maximum(n * sub - 1, 0), 0))
    return pl.pallas_call(
        _dil_attn_kernel,
        grid=(B, d, L // rows),
        in_specs=[cur, prev, cur, prev, cur,
                  pl.BlockSpec((B_HEADS_PER_GROUP, BLOCK, 2 * BLOCK), lambda b, r, n: (0, 0, 0))],
        out_specs=[cur, pl.BlockSpec((1, 1, rows, LANES), lambda b, r, n: (b, r, n, 0))],
        out_shape=[
            jax.ShapeDtypeStruct((B, d, L, w), BF16),
            jax.ShapeDtypeStruct((B, d, L, LANES), BF16),
        ],
        compiler_params=_params("arbitrary", "arbitrary", "arbitrary"),
        name=f"dil_attn_{g}",
    )(q, k, k, v, v, bias)


def _dil_out_kernel(o0_ref, o1_ref, o2_ref, l0_ref, l1_ref, l2_ref, h_ref, wo_ref, fg_ref,
                    wrh_ref, wrl_ref, br_ref, p1t_ref, p2t_ref, h_out_ref, xn_hbm, route_ref, xbuf, sem):
    tm = h_ref.shape[1]

    def natural(ref, pt_ref):
        x = ref[0].reshape(tm, ref.shape[3])
        return x.astype(F32) if pt_ref is None else _dot(pt_ref[...], x)

    def lse_of(ref, pt_ref):
        x = natural(ref, pt_ref)
        return x + pltpu.roll(x, LANES - 8, 1) + pltpu.roll(x, LANES - 16, 1)

    o0, o1, o2 = natural(o0_ref, None), natural(o1_ref, p1t_ref), natural(o2_ref, p2t_ref)
    l0, l1, l2 = lse_of(l0_ref, None), lse_of(l1_ref, p1t_ref), lse_of(l2_ref, p2t_ref)
    m = jnp.maximum(jnp.maximum(l0, l1), l2)
    e0, e1, e2 = jnp.exp2(l0 - m), jnp.exp2(l1 - m), jnp.exp2(l2 - m)
    den = e0 + e1 + e2
    row = lax.broadcasted_iota(jnp.int32, (LANES, B_OUT_WIDTH), 0)
    col = lax.broadcasted_iota(jnp.int32, (LANES, B_OUT_WIDTH), 1)
    spread = (jnp.right_shift(col, 6) == row).astype(BF16)

    def widen(a):
        hi = a.astype(BF16)
        lo = (a - hi.astype(F32)).astype(BF16)
        return _dot(hi, spread) + _dot(lo, spread)

    merged = widen(e0 / den) * o0 + widen(e1 / den) * o1 + widen(e2 / den) * o2
    h = h_ref[0] + _dot(merged.astype(BF16), wo_ref[...])
    _ffn_in(h, fg_ref, wrh_ref, wrl_ref, br_ref, h_out_ref, xn_hbm, route_ref, xbuf, sem)


def _dil_out(os_, lses, h, w_out, ffn_g, router):
    B, S, _ = h.shape
    tm = TOK_TILE
    wrh, wrl, br = router
    const = lambda b, t: (0, 0)
    tok = lambda w: pl.BlockSpec((1, tm, w), lambda b, t: (b, t, 0))
    res = lambda w: [pl.BlockSpec((1, d, tm // d, w), lambda b, t: (b, 0, t, 0)) for d in B_DILATIONS]
    out_specs, out_shape, scratch = _token_out_specs(B, S, tm)
    return pl.pallas_call(
        _dil_out_kernel,
        grid=(B, S // tm),
        in_specs=res(B_OUT_WIDTH) + res(LANES) + [
            tok(D_MODEL),
            pl.BlockSpec((B_OUT_WIDTH, D_MODEL), const),
            pl.BlockSpec((1, D_MODEL), const),
            pl.BlockSpec((D_MODEL, LANES), const),
            pl.BlockSpec((D_MODEL, LANES), const),
            pl.BlockSpec((1, LANES), const),
            pl.BlockSpec((tm, tm), const),
            pl.BlockSpec((tm, tm), const),
        ],
        out_specs=out_specs,
        out_shape=out_shape,
        scratch_shapes=scratch,
        compiler_params=_params("arbitrary", "arbitrary"),
        name="dil_out",
    )(*os_, *lses, h, w_out.astype(BF16), ffn_g.reshape(1, D_MODEL), wrh, wrl, br,
      _residue_perm(tm, B_DILATIONS[1]).T, _residue_perm(tm, B_DILATIONS[2]).T)


def _final_kernel(h_ref, y_ref, o_ref):
    o_ref[0] = h_ref[0] + _rows_from_slabs(y_ref)


def _final(h, y):
    B, S, _ = h.shape
    tm = TOK_TILE
    tok = lambda w: pl.BlockSpec((1, tm, w), lambda b, t: (b, t, 0))
    return pl.pallas_call(
        _final_kernel,
        grid=(B, S // tm),
        in_specs=[tok(D_MODEL), _moe_out_spec(S, tm)],
        out_specs=tok(D_MODEL),
        out_shape=jax.ShapeDtypeStruct((B, S, D_MODEL), F32),
        compiler_params=_params("arbitrary", "arbitrary"),
        name="moe_final",
    )(h, y)


def kernel(x, a_norm, a_w_in, a_b_f, a_q_gain, a_k_gain, a_w_out, kv_norm, kv_w, kv_k_gain, rel_bias, b_norm, b_w_q, b_q_gain, b_w_out, ffn_norm, moe_w_group, moe_b_group, moe_w_expert, moe_b_expert, moe_w_gate, moe_w_up, moe_w_down):
    routers = [_router_weights(moe_w_group[l], moe_b_group[l], moe_w_expert[l], moe_b_expert[l])
               for l in range(2)]
    q, k, vt = _fox_in(x, a_norm[0], a_w_in[0], a_b_f[0], a_q_gain[0], a_k_gain[0])
    ot = _fox_attn(q, k, vt)
    h1, xn1, route1 = _fox_out(ot, x, a_w_out[0], ffn_norm[0], routers[0])
    y1 = _moe(xn1, route1, moe_w_gate[0], moe_w_up[0], moe_w_down[0])
    h2, qkv = _dil_in(h1, y1, kv_norm, b_norm[0], kv_w, b_w_q[0], kv_k_gain, b_q_gain[0])
    outs, lses = [], []
    for g, d in enumerate(B_DILATIONS):
        o, lse = _dil_attn(*qkv[g], _branch_bias(rel_bias, g, d), g)
        outs.append(o)
        lses.append(lse)
    h3, xn3, route3 = _dil_out(outs, lses, h2, b_w_out[0], ffn_norm[1], routers[1])
    y3 = _moe(xn3, route3, moe_w_gate[1], moe_w_up[1], moe_w_down[1])
    return _final(h3, y3)
```
